```python
import math
import jax, jax.numpy as jnp
from jax import lax
import numpy as np

D_MODEL = 1024
BATCH = 32
SEQ = 2048
DEPTH = 4

HEAD_DIM = 64
A_HEADS = 4
A_WIDTH = A_HEADS * HEAD_DIM
B_WIDTH = 256
SHORT_CONV = 3
C_WIDTH = 256
CONF_CONV = 31
D_Q_HEADS = 8
D_KV_HEADS = 2
D_GROUP = D_Q_HEADS // D_KV_HEADS
D_WIDTH = D_Q_HEADS * HEAD_DIM
D_KV_WIDTH = D_KV_HEADS * HEAD_DIM
WINDOW = 128
Q_BLOCK = 128
N_BRANCH = 4
REL_BUCKETS = 32
REL_MAX_DIST = 128
PLE_DIM = 256
D_FF = 2816
EPS = 1e-6
NEG_INF = -1e30

A_QKV_END = 3 * A_WIDTH
A_F_END = A_QKV_END + A_HEADS
B_END = A_F_END + 3 * B_WIDTH
C_END = B_END + 2 * C_WIDTH
D_END = C_END + D_WIDTH + 2 * D_KV_WIDTH
N_IN = D_END + N_BRANCH * D_MODEL

kernel_name = 'hybrid_gated_parallel_mixer'


def rms_norm(x, g):
    xf = x.astype(jnp.float32)
    y = xf * lax.rsqrt(jnp.mean(xf * xf, axis=-1, keepdims=True) + EPS)
    return (y * g.astype(jnp.float32)).astype(x.dtype)


def layer_norm(x, g, b):
    xf = x.astype(jnp.float32)
    mu = jnp.mean(xf, axis=-1, keepdims=True)
    xc = xf - mu
    y = xc * lax.rsqrt(jnp.mean(xc * xc, axis=-1, keepdims=True) + EPS)
    return (y * g.astype(jnp.float32) + b.astype(jnp.float32)).astype(x.dtype)


def swiglu_ffn(x, w_gu, w_down):
    gate, up = jnp.split(x @ w_gu, 2, axis=-1)
    return (jax.nn.silu(gate) * up) @ w_down


def causal_depthwise_conv(x, w):
    K, C = w.shape
    return lax.conv_general_dilated(
        x, w[:, None, :].astype(x.dtype), window_strides=(1,), padding=[(K - 1, 0)],
        dimension_numbers=('NWC', 'WIO', 'NWC'), feature_group_count=C)


def t5_causal_bucket(dist):
    max_exact = REL_BUCKETS // 2
    large = max_exact + (jnp.log(jnp.maximum(dist, 1).astype(jnp.float32) / max_exact)
                         / math.log(REL_MAX_DIST / max_exact)
                         * (REL_BUCKETS - max_exact)).astype(jnp.int32)
    large = jnp.minimum(large, REL_BUCKETS - 1)
    return jnp.where(dist < max_exact, dist, large)


def forgetting_attention(q, k, v, log_f):
    S, d = q.shape[1], q.shape[3]
    c = jnp.transpose(jnp.cumsum(log_f, axis=1), (0, 2, 1))
    scale = d ** -0.5
    outs = []
    for blk in range(S // Q_BLOCK):
        q0 = blk * Q_BLOCK
        end = q0 + Q_BLOCK
        s = jnp.einsum('bqhd,bkhd->bhqk', q[:, q0:end], k[:, :end]).astype(jnp.float32) * scale
        decay = c[:, :, q0:end, None] - c[:, :, None, :end]
        causal = (q0 + jnp.arange(Q_BLOCK))[:, None] >= jnp.arange(end)[None, :]
        s = jnp.where(causal, s + decay, NEG_INF)
        pr = jax.nn.softmax(s, axis=-1).astype(v.dtype)
        outs.append(jnp.einsum('bhqk,bkhd->bqhd', pr, v[:, :end]))
    return jnp.concatenate(outs, axis=1)


def sliding_window_attention(q, k, v, sinks, band_bias):
    Bn, S, _, d = q.shape
    nb = S // Q_BLOCK
    qb = q.reshape(Bn, nb, Q_BLOCK, D_KV_HEADS, D_GROUP, d)

    def band(t):
        tb = t.reshape(Bn, nb, Q_BLOCK, D_KV_HEADS, d)
        prev = jnp.concatenate([jnp.zeros_like(tb[:, :1]), tb[:, :-1]], axis=1)
        return jnp.concatenate([prev, tb], axis=2)

    kb, vb = band(k), band(v)
    s = jnp.einsum('bnqkgd,bnskd->bnkgqs', qb, kb).astype(jnp.float32) * d ** -0.5
    s = s + band_bias.reshape(D_KV_HEADS, D_GROUP, Q_BLOCK, 2 * Q_BLOCK)
    kj = jnp.arange(2 * Q_BLOCK)[None, :]
    dist = jnp.arange(Q_BLOCK)[:, None] + Q_BLOCK - kj
    in_window = (dist >= 0) & (dist < WINDOW)
    key_pos = jnp.arange(nb)[:, None] * Q_BLOCK - Q_BLOCK + kj
    valid = in_window[None] & (key_pos >= 0)[:, None, :]
    s = jnp.where(valid[None, :, None, None], s, NEG_INF)
    sink = sinks.astype(jnp.float32).reshape(1, 1, D_KV_HEADS, D_GROUP, 1, 1)
    m = jnp.maximum(jnp.max(s, axis=-1, keepdims=True), sink)
    e = jnp.exp(s - m)
    pr = (e / (jnp.sum(e, axis=-1, keepdims=True) + jnp.exp(sink - m))).astype(v.dtype)
    out = jnp.einsum('bnkgqs,bnskd->bnqkgd', pr, vb)
    return out.reshape(Bn, S, D_Q_HEADS * d)


def _fwd_setup_inputs(seed: int = 0) -> dict:
    key = jax.random.key(seed)
    ks = iter(jax.random.split(key, 48))

    def nrm(shape, scale):
        return scale * jax.random.normal(next(ks), shape, jnp.float32)

    def gain(shape):
        return 1.0 + nrm(shape, 0.05)

    D, F = D_MODEL, D_FF
    return {
        'x': nrm((BATCH, SEQ, D), 1.0),
        'p': nrm((DEPTH, BATCH, SEQ, PLE_DIM), 1.0),
        'ffn1_norm_pre': gain((DEPTH, D)),
        'ffn1_w_gu': nrm((DEPTH, D, 2 * F), D ** -0.5),
        'ffn1_w_down': nrm((DEPTH, F, D), F ** -0.5),
        'ffn1_norm_post': gain((DEPTH, D)),
        'mix_norm_pre': gain((DEPTH, D)),
        'w_in': nrm((DEPTH, D, N_IN), D ** -0.5),
        'b_forget': 3.0 + nrm((DEPTH, A_HEADS), 1.0),
        'b_gate': nrm((DEPTH, N_BRANCH * D), 0.1),
        'conv_short': nrm((DEPTH, SHORT_CONV, B_WIDTH), SHORT_CONV ** -0.5),
        'conv_dw': nrm((DEPTH, CONF_CONV, C_WIDTH), CONF_CONV ** -0.5),
        'conv_dw_bias': nrm((DEPTH, C_WIDTH), 0.02),
        'conv_ln_gain': gain((DEPTH, C_WIDTH)),
        'conv_ln_bias': nrm((DEPTH, C_WIDTH), 0.02),
        'attn_sinks': nrm((DEPTH, D_Q_HEADS), 0.5),
        'rel_bias': nrm((REL_BUCKETS, D_Q_HEADS), 0.5),
        'w_br_a': nrm((DEPTH, A_WIDTH, D), A_WIDTH ** -0.5),
        'w_br_b': nrm((DEPTH, B_WIDTH, D), B_WIDTH ** -0.5),
        'w_br_c': nrm((DEPTH, C_WIDTH, D), C_WIDTH ** -0.5),
        'w_br_d': nrm((DEPTH, D_WIDTH, D), D_WIDTH ** -0.5),
        'w_o': nrm((DEPTH, D, D), D ** -0.5),
        'mix_norm_post': gain((DEPTH, D)),
        'ffn2_norm_pre': gain((DEPTH, D)),
        'ffn2_w_gu': nrm((DEPTH, D, 2 * F), D ** -0.5),
        'ffn2_w_down': nrm((DEPTH, F, D), F ** -0.5),
        'ffn2_norm_post': gain((DEPTH, D)),
        'ple_norm_gate': gain((DEPTH, D)),
        'w_ple_gate': nrm((DEPTH, D, D), D ** -0.5),
        'w_ple': nrm((DEPTH, PLE_DIM, D), PLE_DIM ** -0.5),
        'ple_norm_post': gain((DEPTH, D)),
    }


def _fwd_reference(x, p, ffn1_norm_pre, ffn1_w_gu, ffn1_w_down, ffn1_norm_post,
              mix_norm_pre, w_in, b_forget, b_gate, conv_short, conv_dw, conv_dw_bias,
              conv_ln_gain, conv_ln_bias, attn_sinks, rel_bias,
              w_br_a, w_br_b, w_br_c, w_br_d, w_o, mix_norm_post,
              ffn2_norm_pre, ffn2_w_gu, ffn2_w_down, ffn2_norm_post,
              ple_norm_gate, w_ple_gate, w_ple, ple_norm_post):
    Bn, S, _ = x.shape
    band_dist = jnp.maximum(jnp.arange(Q_BLOCK)[:, None] + Q_BLOCK - jnp.arange(2 * Q_BLOCK)[None, :], 0)
    band_bias = jnp.transpose(rel_bias[t5_causal_bucket(band_dist)], (2, 0, 1)).astype(jnp.float32)

    h = x
    for i in range(DEPTH):
        f1 = swiglu_ffn(rms_norm(h, ffn1_norm_pre[i]), ffn1_w_gu[i], ffn1_w_down[i])
        h = h + 0.5 * rms_norm(f1, ffn1_norm_post[i])

        u = rms_norm(h, mix_norm_pre[i])
        proj = u @ w_in[i]
        a_qkv = proj[..., :A_QKV_END]
        a_f = proj[..., A_QKV_END:A_F_END]
        b_in = proj[..., A_F_END:B_END]
        c_in = proj[..., B_END:C_END]
        d_qkv = proj[..., C_END:D_END]
        gates = proj[..., D_END:]

        qa, ka, va = [t.reshape(Bn, S, A_HEADS, HEAD_DIM) for t in jnp.split(a_qkv, 3, axis=-1)]
        log_f = jax.nn.log_sigmoid(a_f.astype(jnp.float32) + b_forget[i].astype(jnp.float32))
        ya = forgetting_attention(qa, ka, va, log_f).reshape(Bn, S, A_WIDTH)

        bg, cg, xb = jnp.split(b_in, 3, axis=-1)
        yb = bg * causal_depthwise_conv(cg * xb, conv_short[i])

        glu = c_in[..., :C_WIDTH] * jax.nn.sigmoid(c_in[..., C_WIDTH:])
        yc = causal_depthwise_conv(glu, conv_dw[i]) + conv_dw_bias[i]
        yc = jax.nn.silu(layer_norm(yc, conv_ln_gain[i], conv_ln_bias[i]))

        qd = d_qkv[..., :D_WIDTH].reshape(Bn, S, D_Q_HEADS, HEAD_DIM)
        kd = d_qkv[..., D_WIDTH:D_WIDTH + D_KV_WIDTH].reshape(Bn, S, D_KV_HEADS, HEAD_DIM)
        vd = d_qkv[..., D_WIDTH + D_KV_WIDTH:].reshape(Bn, S, D_KV_HEADS, HEAD_DIM)
        yd = sliding_window_attention(qd, kd, vd, attn_sinks[i], band_bias)

        g = jax.nn.sigmoid(gates + b_gate[i]).reshape(Bn, S, N_BRANCH, D_MODEL)
        merged = (g[..., 0, :] * (ya @ w_br_a[i]) + g[..., 1, :] * (yb @ w_br_b[i])
                  + g[..., 2, :] * (yc @ w_br_c[i]) + g[..., 3, :] * (yd @ w_br_d[i]))
        h = h + rms_norm(merged @ w_o[i], mix_norm_post[i])

        f2 = swiglu_ffn(rms_norm(h, ffn2_norm_pre[i]), ffn2_w_gu[i], ffn2_w_down[i])
        h = h + 0.5 * rms_norm(f2, ffn2_norm_post[i])

        pg = jax.nn.sigmoid(rms_norm(h, ple_norm_gate[i]) @ w_ple_gate[i])
        h = h + pg * rms_norm(p[i] @ w_ple[i], ple_norm_post[i])
    return h


import jax as _jax
import jax.numpy as _jnp

TWIN_FORMAT = 'train_step'
FWD_PARAMS = ['x', 'p', 'ffn1_norm_pre', 'ffn1_w_gu', 'ffn1_w_down', 'ffn1_norm_post', 'mix_norm_pre', 'w_in', 'b_forget', 'b_gate', 'conv_short', 'conv_dw', 'conv_dw_bias', 'conv_ln_gain', 'conv_ln_bias', 'attn_sinks', 'rel_bias', 'w_br_a', 'w_br_b', 'w_br_c', 'w_br_d', 'w_o', 'mix_norm_post', 'ffn2_norm_pre', 'ffn2_w_gu', 'ffn2_w_down', 'ffn2_norm_post', 'ple_norm_gate', 'w_ple_gate', 'w_ple', 'ple_norm_post']
TWIN_WEIGHTS = ['ffn1_norm_pre', 'ffn1_w_gu', 'ffn1_w_down', 'ffn1_norm_post', 'mix_norm_pre', 'w_in', 'b_forget', 'b_gate', 'conv_short', 'conv_dw', 'conv_dw_bias', 'conv_ln_gain', 'conv_ln_bias', 'attn_sinks', 'rel_bias', 'w_br_a', 'w_br_b', 'w_br_c', 'w_br_d', 'w_o', 'mix_norm_post', 'ffn2_norm_pre', 'ffn2_w_gu', 'ffn2_w_down', 'ffn2_norm_post', 'ple_norm_gate', 'w_ple_gate', 'w_ple', 'ple_norm_post']
TWIN_DIFF_INPUT = 'x'
TWIN_INPUTS = ['x', 'p', 'ffn1_norm_pre', 'ffn1_w_gu', 'ffn1_w_down', 'ffn1_norm_post', 'mix_norm_pre', 'w_in', 'b_forget', 'b_gate', 'conv_short', 'conv_dw', 'conv_dw_bias', 'conv_ln_gain', 'conv_ln_bias', 'attn_sinks', 'rel_bias', 'w_br_a', 'w_br_b', 'w_br_c', 'w_br_d', 'w_o', 'mix_norm_post', 'ffn2_norm_pre', 'ffn2_w_gu', 'ffn2_w_down', 'ffn2_norm_post', 'ple_norm_gate', 'w_ple_gate', 'w_ple', 'ple_norm_post', 'loss_target', 'm_ffn1_norm_pre', 'm_ffn1_w_gu', 'm_ffn1_w_down', 'm_ffn1_norm_post', 'm_mix_norm_pre', 'm_w_in', 'm_b_forget', 'm_b_gate', 'm_conv_short', 'm_conv_dw', 'm_conv_dw_bias', 'm_conv_ln_gain', 'm_conv_ln_bias', 'm_attn_sinks', 'm_rel_bias', 'm_w_br_a', 'm_w_br_b', 'm_w_br_c', 'm_w_br_d', 'm_w_o', 'm_mix_norm_post', 'm_ffn2_norm_pre', 'm_ffn2_w_gu', 'm_ffn2_w_down', 'm_ffn2_norm_post', 'm_ple_norm_gate', 'm_w_ple_gate', 'm_w_ple', 'm_ple_norm_post', 'v_ffn1_norm_pre', 'v_ffn1_w_gu', 'v_ffn1_w_down', 'v_ffn1_norm_post', 'v_mix_norm_pre', 'v_w_in', 'v_b_forget', 'v_b_gate', 'v_conv_short', 'v_conv_dw', 'v_conv_dw_bias', 'v_conv_ln_gain', 'v_conv_ln_bias', 'v_attn_sinks', 'v_rel_bias', 'v_w_br_a', 'v_w_br_b', 'v_w_br_c', 'v_w_br_d', 'v_w_o', 'v_mix_norm_post', 'v_ffn2_norm_pre', 'v_ffn2_w_gu', 'v_ffn2_w_down', 'v_ffn2_norm_post', 'v_ple_norm_gate', 'v_w_ple_gate', 'v_w_ple', 'v_ple_norm_post']
TWIN_OUTPUTS = ['loss', 'grad_x', 'grad_ffn1_norm_pre', 'grad_ffn1_w_gu', 'grad_ffn1_w_down', 'grad_ffn1_norm_post', 'grad_mix_norm_pre', 'grad_w_in', 'grad_b_forget', 'grad_b_gate', 'grad_conv_short', 'grad_conv_dw', 'grad_conv_dw_bias', 'grad_conv_ln_gain', 'grad_conv_ln_bias', 'grad_attn_sinks', 'grad_rel_bias', 'grad_w_br_a', 'grad_w_br_b', 'grad_w_br_c', 'grad_w_br_d', 'grad_w_o', 'grad_mix_norm_post', 'grad_ffn2_norm_pre', 'grad_ffn2_w_gu', 'grad_ffn2_w_down', 'grad_ffn2_norm_post', 'grad_ple_norm_gate', 'grad_w_ple_gate', 'grad_w_ple', 'grad_ple_norm_post', 'delta_ffn1_norm_pre', 'delta_ffn1_w_gu', 'delta_ffn1_w_down', 'delta_ffn1_norm_post', 'delta_mix_norm_pre', 'delta_w_in', 'delta_b_forget', 'delta_b_gate', 'delta_conv_short', 'delta_conv_dw', 'delta_conv_dw_bias', 'delta_conv_ln_gain', 'delta_conv_ln_bias', 'delta_attn_sinks', 'delta_rel_bias', 'delta_w_br_a', 'delta_w_br_b', 'delta_w_br_c', 'delta_w_br_d', 'delta_w_o', 'delta_mix_norm_post', 'delta_ffn2_norm_pre', 'delta_ffn2_w_gu', 'delta_ffn2_w_down', 'delta_ffn2_norm_post', 'delta_ple_norm_gate', 'delta_w_ple_gate', 'delta_w_ple', 'delta_ple_norm_post', 'new_m_ffn1_norm_pre', 'new_m_ffn1_w_gu', 'new_m_ffn1_w_down', 'new_m_ffn1_norm_post', 'new_m_mix_norm_pre', 'new_m_w_in', 'new_m_b_forget', 'new_m_b_gate', 'new_m_conv_short', 'new_m_conv_dw', 'new_m_conv_dw_bias', 'new_m_conv_ln_gain', 'new_m_conv_ln_bias', 'new_m_attn_sinks', 'new_m_rel_bias', 'new_m_w_br_a', 'new_m_w_br_b', 'new_m_w_br_c', 'new_m_w_br_d', 'new_m_w_o', 'new_m_mix_norm_post', 'new_m_ffn2_norm_pre', 'new_m_ffn2_w_gu', 'new_m_ffn2_w_down', 'new_m_ffn2_norm_post', 'new_m_ple_norm_gate', 'new_m_w_ple_gate', 'new_m_w_ple', 'new_m_ple_norm_post', 'new_v_ffn1_norm_pre', 'new_v_ffn1_w_gu', 'new_v_ffn1_w_down', 'new_v_ffn1_norm_post', 'new_v_mix_norm_pre', 'new_v_w_in', 'new_v_b_forget', 'new_v_b_gate', 'new_v_conv_short', 'new_v_conv_dw', 'new_v_conv_dw_bias', 'new_v_conv_ln_gain', 'new_v_conv_ln_bias', 'new_v_attn_sinks', 'new_v_rel_bias', 'new_v_w_br_a', 'new_v_w_br_b', 'new_v_w_br_c', 'new_v_w_br_d', 'new_v_w_o', 'new_v_mix_norm_post', 'new_v_ffn2_norm_pre', 'new_v_ffn2_w_gu', 'new_v_ffn2_w_down', 'new_v_ffn2_norm_post', 'new_v_ple_norm_gate', 'new_v_w_ple_gate', 'new_v_w_ple', 'new_v_ple_norm_post']
TWIN_LEAF_KINDS = {'loss': 'loss', 'grad_x': 'grad_x', 'grad_ffn1_norm_pre': 'grad_w', 'grad_ffn1_w_gu': 'grad_w', 'grad_ffn1_w_down': 'grad_w', 'grad_ffn1_norm_post': 'grad_w', 'grad_mix_norm_pre': 'grad_w', 'grad_w_in': 'grad_w', 'grad_b_forget': 'grad_w', 'grad_b_gate': 'grad_w', 'grad_conv_short': 'grad_w', 'grad_conv_dw': 'grad_w', 'grad_conv_dw_bias': 'grad_w', 'grad_conv_ln_gain': 'grad_w', 'grad_conv_ln_bias': 'grad_w', 'grad_attn_sinks': 'grad_w', 'grad_rel_bias': 'grad_w', 'grad_w_br_a': 'grad_w', 'grad_w_br_b': 'grad_w', 'grad_w_br_c': 'grad_w', 'grad_w_br_d': 'grad_w', 'grad_w_o': 'grad_w', 'grad_mix_norm_post': 'grad_w', 'grad_ffn2_norm_pre': 'grad_w', 'grad_ffn2_w_gu': 'grad_w', 'grad_ffn2_w_down': 'grad_w', 'grad_ffn2_norm_post': 'grad_w', 'grad_ple_norm_gate': 'grad_w', 'grad_w_ple_gate': 'grad_w', 'grad_w_ple': 'grad_w', 'grad_ple_norm_post': 'grad_w', 'delta_ffn1_norm_pre': 'delta_w', 'delta_ffn1_w_gu': 'delta_w', 'delta_ffn1_w_down': 'delta_w', 'delta_ffn1_norm_post': 'delta_w', 'delta_mix_norm_pre': 'delta_w', 'delta_w_in': 'delta_w', 'delta_b_forget': 'delta_w', 'delta_b_gate': 'delta_w', 'delta_conv_short': 'delta_w', 'delta_conv_dw': 'delta_w', 'delta_conv_dw_bias': 'delta_w', 'delta_conv_ln_gain': 'delta_w', 'delta_conv_ln_bias': 'delta_w', 'delta_attn_sinks': 'delta_w', 'delta_rel_bias': 'delta_w', 'delta_w_br_a': 'delta_w', 'delta_w_br_b': 'delta_w', 'delta_w_br_c': 'delta_w', 'delta_w_br_d': 'delta_w', 'delta_w_o': 'delta_w', 'delta_mix_norm_post': 'delta_w', 'delta_ffn2_norm_pre': 'delta_w', 'delta_ffn2_w_gu': 'delta_w', 'delta_ffn2_w_down': 'delta_w', 'delta_ffn2_norm_post': 'delta_w', 'delta_ple_norm_gate': 'delta_w', 'delta_w_ple_gate': 'delta_w', 'delta_w_ple': 'delta_w', 'delta_ple_norm_post': 'delta_w', 'new_m_ffn1_norm_pre': 'new_m', 'new_m_ffn1_w_gu': 'new_m', 'new_m_ffn1_w_down': 'new_m', 'new_m_ffn1_norm_post': 'new_m', 'new_m_mix_norm_pre': 'new_m', 'new_m_w_in': 'new_m', 'new_m_b_forget': 'new_m', 'new_m_b_gate': 'new_m', 'new_m_conv_short': 'new_m', 'new_m_conv_dw': 'new_m', 'new_m_conv_dw_bias': 'new_m', 'new_m_conv_ln_gain': 'new_m', 'new_m_conv_ln_bias': 'new_m', 'new_m_attn_sinks': 'new_m', 'new_m_rel_bias': 'new_m', 'new_m_w_br_a': 'new_m', 'new_m_w_br_b': 'new_m', 'new_m_w_br_c': 'new_m', 'new_m_w_br_d': 'new_m', 'new_m_w_o': 'new_m', 'new_m_mix_norm_post': 'new_m', 'new_m_ffn2_norm_pre': 'new_m', 'new_m_ffn2_w_gu': 'new_m', 'new_m_ffn2_w_down': 'new_m', 'new_m_ffn2_norm_post': 'new_m', 'new_m_ple_norm_gate': 'new_m', 'new_m_w_ple_gate': 'new_m', 'new_m_w_ple': 'new_m', 'new_m_ple_norm_post': 'new_m', 'new_v_ffn1_norm_pre': 'new_v', 'new_v_ffn1_w_gu': 'new_v', 'new_v_ffn1_w_down': 'new_v', 'new_v_ffn1_norm_post': 'new_v', 'new_v_mix_norm_pre': 'new_v', 'new_v_w_in': 'new_v', 'new_v_b_forget': 'new_v', 'new_v_b_gate': 'new_v', 'new_v_conv_short': 'new_v', 'new_v_conv_dw': 'new_v', 'new_v_conv_dw_bias': 'new_v', 'new_v_conv_ln_gain': 'new_v', 'new_v_conv_ln_bias': 'new_v', 'new_v_attn_sinks': 'new_v', 'new_v_rel_bias': 'new_v', 'new_v_w_br_a': 'new_v', 'new_v_w_br_b': 'new_v', 'new_v_w_br_c': 'new_v', 'new_v_w_br_d': 'new_v', 'new_v_w_o': 'new_v', 'new_v_mix_norm_post': 'new_v', 'new_v_ffn2_norm_pre': 'new_v', 'new_v_ffn2_w_gu': 'new_v', 'new_v_ffn2_w_down': 'new_v', 'new_v_ffn2_norm_post': 'new_v', 'new_v_ple_norm_gate': 'new_v', 'new_v_w_ple_gate': 'new_v', 'new_v_w_ple': 'new_v', 'new_v_ple_norm_post': 'new_v'}


def _forward(args):
    return _fwd_reference(*[args[k] for k in FWD_PARAMS])


def _output_shape():
    out = _jax.eval_shape(lambda: _forward(_fwd_setup_inputs(0)))
    return out.shape, out.dtype

N_MICROBATCH = 1
ADAM_LR = 0.001
ADAM_B1 = 0.9
ADAM_B2 = 0.999
ADAM_EPS = 1e-08
ADAM_WD = 0.01
ADAM_STEP = 10
PER_EXAMPLE_BATCH_AXIS = {'x': 0, 'p': 1, 'loss_target': 0}
SHARED_INPUTS = []
_WEIGHT_DTYPES = {'ffn1_norm_pre': _jnp.float32, 'ffn1_w_gu': _jnp.float32, 'ffn1_w_down': _jnp.float32, 'ffn1_norm_post': _jnp.float32, 'mix_norm_pre': _jnp.float32, 'w_in': _jnp.float32, 'b_forget': _jnp.float32, 'b_gate': _jnp.float32, 'conv_short': _jnp.float32, 'conv_dw': _jnp.float32, 'conv_dw_bias': _jnp.float32, 'conv_ln_gain': _jnp.float32, 'conv_ln_bias': _jnp.float32, 'attn_sinks': _jnp.float32, 'rel_bias': _jnp.float32, 'w_br_a': _jnp.float32, 'w_br_b': _jnp.float32, 'w_br_c': _jnp.float32, 'w_br_d': _jnp.float32, 'w_o': _jnp.float32, 'mix_norm_post': _jnp.float32, 'ffn2_norm_pre': _jnp.float32, 'ffn2_w_gu': _jnp.float32, 'ffn2_w_down': _jnp.float32, 'ffn2_norm_post': _jnp.float32, 'ple_norm_gate': _jnp.float32, 'w_ple_gate': _jnp.float32, 'w_ple': _jnp.float32, 'ple_norm_post': _jnp.float32}
MOMENT_SCALE = {'ffn1_norm_pre': 1.753731e+00, 'ffn1_w_gu': 7.584980e-01, 'ffn1_w_down': 1.270779e+00, 'ffn1_norm_post': 1.394360e+01, 'mix_norm_pre': 2.706412e+00, 'w_in': 1.028179e+00, 'b_forget': 3.773870e+00, 'b_gate': 5.814401e-01, 'conv_short': 2.412659e+00, 'conv_dw': 2.022606e+00, 'conv_dw_bias': 2.967816e+01, 'conv_ln_gain': 1.127642e+01, 'conv_ln_bias': 1.883612e+01, 'attn_sinks': 2.236526e-01, 'rel_bias': 9.401521e-01, 'w_br_a': 9.752468e-01, 'w_br_b': 1.236735e+00, 'w_br_c': 2.883007e+00, 'w_br_d': 8.953059e-01, 'w_o': 2.902355e+00, 'mix_norm_post': 6.294085e+01, 'ffn2_norm_pre': 1.222042e+00, 'ffn2_w_gu': 4.943392e-01, 'ffn2_w_down': 8.768564e-01, 'ffn2_norm_post': 1.500549e+01, 'ple_norm_gate': 1.482388e+00, 'w_ple_gate': 4.773744e-01, 'w_ple': 7.479462e-01, 'ple_norm_post': 1.775683e+01}


def _to_microbatches(a, axis):
    t = _jnp.moveaxis(a, axis, 0)
    t = t.reshape((N_MICROBATCH, t.shape[0] // N_MICROBATCH) + t.shape[1:])
    return _jnp.moveaxis(t, 1, axis + 1)


def setup_inputs(seed: int = 0) -> dict:
    inp = _fwd_setup_inputs(seed)
    key = _jax.random.fold_in(_jax.random.key(seed), 7919)
    shape, _ = _output_shape()
    out = dict(inp)
    out["loss_target"] = _jax.random.normal(_jax.random.fold_in(key, 0), shape, _jnp.float32)
    for i, name in enumerate(TWIN_WEIGHTS):
        w = inp[name].astype(_jnp.float32)
        if MOMENT_SCALE is None:
            s = _jnp.sqrt(_jnp.mean(_jnp.square(w)) + 1e-30)
        else:
            s = MOMENT_SCALE[name]
        km, kv = _jax.random.split(_jax.random.fold_in(key, i + 1))
        out[name] = w
        out["m_" + name] = s * _jax.random.normal(km, w.shape, _jnp.float32)
        out["v_" + name] = (s * s) * _jax.random.uniform(kv, w.shape, _jnp.float32, 0.5, 1.5)
    if N_MICROBATCH > 1:
        for name, axis in PER_EXAMPLE_BATCH_AXIS.items():
            out[name] = _to_microbatches(out[name], axis)
    return {'x': out['x'], 'p': out['p'], 'ffn1_norm_pre': out['ffn1_norm_pre'], 'ffn1_w_gu': out['ffn1_w_gu'], 'ffn1_w_down': out['ffn1_w_down'], 'ffn1_norm_post': out['ffn1_norm_post'], 'mix_norm_pre': out['mix_norm_pre'], 'w_in': out['w_in'], 'b_forget': out['b_forget'], 'b_gate': out['b_gate'], 'conv_short': out['conv_short'], 'conv_dw': out['conv_dw'], 'conv_dw_bias': out['conv_dw_bias'], 'conv_ln_gain': out['conv_ln_gain'], 'conv_ln_bias': out['conv_ln_bias'], 'attn_sinks': out['attn_sinks'], 'rel_bias': out['rel_bias'], 'w_br_a': out['w_br_a'], 'w_br_b': out['w_br_b'], 'w_br_c': out['w_br_c'], 'w_br_d': out['w_br_d'], 'w_o': out['w_o'], 'mix_norm_post': out['mix_norm_post'], 'ffn2_norm_pre': out['ffn2_norm_pre'], 'ffn2_w_gu': out['ffn2_w_gu'], 'ffn2_w_down': out['ffn2_w_down'], 'ffn2_norm_post': out['ffn2_norm_post'], 'ple_norm_gate': out['ple_norm_gate'], 'w_ple_gate': out['w_ple_gate'], 'w_ple': out['w_ple'], 'ple_norm_post': out['ple_norm_post'], 'loss_target': out['loss_target'], 'm_ffn1_norm_pre': out['m_ffn1_norm_pre'], 'm_ffn1_w_gu': out['m_ffn1_w_gu'], 'm_ffn1_w_down': out['m_ffn1_w_down'], 'm_ffn1_norm_post': out['m_ffn1_norm_post'], 'm_mix_norm_pre': out['m_mix_norm_pre'], 'm_w_in': out['m_w_in'], 'm_b_forget': out['m_b_forget'], 'm_b_gate': out['m_b_gate'], 'm_conv_short': out['m_conv_short'], 'm_conv_dw': out['m_conv_dw'], 'm_conv_dw_bias': out['m_conv_dw_bias'], 'm_conv_ln_gain': out['m_conv_ln_gain'], 'm_conv_ln_bias': out['m_conv_ln_bias'], 'm_attn_sinks': out['m_attn_sinks'], 'm_rel_bias': out['m_rel_bias'], 'm_w_br_a': out['m_w_br_a'], 'm_w_br_b': out['m_w_br_b'], 'm_w_br_c': out['m_w_br_c'], 'm_w_br_d': out['m_w_br_d'], 'm_w_o': out['m_w_o'], 'm_mix_norm_post': out['m_mix_norm_post'], 'm_ffn2_norm_pre': out['m_ffn2_norm_pre'], 'm_ffn2_w_gu': out['m_ffn2_w_gu'], 'm_ffn2_w_down': out['m_ffn2_w_down'], 'm_ffn2_norm_post': out['m_ffn2_norm_post'], 'm_ple_norm_gate': out['m_ple_norm_gate'], 'm_w_ple_gate': out['m_w_ple_gate'], 'm_w_ple': out['m_w_ple'], 'm_ple_norm_post': out['m_ple_norm_post'], 'v_ffn1_norm_pre': out['v_ffn1_norm_pre'], 'v_ffn1_w_gu': out['v_ffn1_w_gu'], 'v_ffn1_w_down': out['v_ffn1_w_down'], 'v_ffn1_norm_post': out['v_ffn1_norm_post'], 'v_mix_norm_pre': out['v_mix_norm_pre'], 'v_w_in': out['v_w_in'], 'v_b_forget': out['v_b_forget'], 'v_b_gate': out['v_b_gate'], 'v_conv_short': out['v_conv_short'], 'v_conv_dw': out['v_conv_dw'], 'v_conv_dw_bias': out['v_conv_dw_bias'], 'v_conv_ln_gain': out['v_conv_ln_gain'], 'v_conv_ln_bias': out['v_conv_ln_bias'], 'v_attn_sinks': out['v_attn_sinks'], 'v_rel_bias': out['v_rel_bias'], 'v_w_br_a': out['v_w_br_a'], 'v_w_br_b': out['v_w_br_b'], 'v_w_br_c': out['v_w_br_c'], 'v_w_br_d': out['v_w_br_d'], 'v_w_o': out['v_w_o'], 'v_mix_norm_post': out['v_mix_norm_post'], 'v_ffn2_norm_pre': out['v_ffn2_norm_pre'], 'v_ffn2_w_gu': out['v_ffn2_w_gu'], 'v_ffn2_w_down': out['v_ffn2_w_down'], 'v_ffn2_norm_post': out['v_ffn2_norm_post'], 'v_ple_norm_gate': out['v_ple_norm_gate'], 'v_w_ple_gate': out['v_w_ple_gate'], 'v_w_ple': out['v_w_ple'], 'v_ple_norm_post': out['v_ple_norm_post']}


def _loss(weights, diff, rest, loss_target):
    with _jax.named_scope("forward"):
        args = {**rest, TWIN_DIFF_INPUT: diff, **{k: w.astype(_WEIGHT_DTYPES[k]) for k, w in weights.items()}}
        y = _forward(args)
    with _jax.named_scope("loss_head"):
        err = _jnp.square(y.astype(_jnp.float32) - loss_target)
        return 0.5 * _jnp.sum(_jnp.mean(err, axis=-1)) if err.ndim else 0.5 * err


def _adamw(w, g, m, v):
    m = ADAM_B1 * m + (1.0 - ADAM_B1) * g
    v = ADAM_B2 * v + (1.0 - ADAM_B2) * _jnp.square(g)
    m_hat = m / (1.0 - ADAM_B1 ** ADAM_STEP)
    v_hat = v / (1.0 - ADAM_B2 ** ADAM_STEP)
    delta = -ADAM_LR * (m_hat / (_jnp.sqrt(v_hat) + ADAM_EPS) + ADAM_WD * w)
    return delta, m, v


def reference(x, p, ffn1_norm_pre, ffn1_w_gu, ffn1_w_down, ffn1_norm_post, mix_norm_pre, w_in, b_forget, b_gate, conv_short, conv_dw, conv_dw_bias, conv_ln_gain, conv_ln_bias, attn_sinks, rel_bias, w_br_a, w_br_b, w_br_c, w_br_d, w_o, mix_norm_post, ffn2_norm_pre, ffn2_w_gu, ffn2_w_down, ffn2_norm_post, ple_norm_gate, w_ple_gate, w_ple, ple_norm_post, loss_target, m_ffn1_norm_pre, m_ffn1_w_gu, m_ffn1_w_down, m_ffn1_norm_post, m_mix_norm_pre, m_w_in, m_b_forget, m_b_gate, m_conv_short, m_conv_dw, m_conv_dw_bias, m_conv_ln_gain, m_conv_ln_bias, m_attn_sinks, m_rel_bias, m_w_br_a, m_w_br_b, m_w_br_c, m_w_br_d, m_w_o, m_mix_norm_post, m_ffn2_norm_pre, m_ffn2_w_gu, m_ffn2_w_down, m_ffn2_norm_post, m_ple_norm_gate, m_w_ple_gate, m_w_ple, m_ple_norm_post, v_ffn1_norm_pre, v_ffn1_w_gu, v_ffn1_w_down, v_ffn1_norm_post, v_mix_norm_pre, v_w_in, v_b_forget, v_b_gate, v_conv_short, v_conv_dw, v_conv_dw_bias, v_conv_ln_gain, v_conv_ln_bias, v_attn_sinks, v_rel_bias, v_w_br_a, v_w_br_b, v_w_br_c, v_w_br_d, v_w_o, v_mix_norm_post, v_ffn2_norm_pre, v_ffn2_w_gu, v_ffn2_w_down, v_ffn2_norm_post, v_ple_norm_gate, v_w_ple_gate, v_w_ple, v_ple_norm_post):
    given = dict(x=x, p=p, ffn1_norm_pre=ffn1_norm_pre, ffn1_w_gu=ffn1_w_gu, ffn1_w_down=ffn1_w_down, ffn1_norm_post=ffn1_norm_post, mix_norm_pre=mix_norm_pre, w_in=w_in, b_forget=b_forget, b_gate=b_gate, conv_short=conv_short, conv_dw=conv_dw, conv_dw_bias=conv_dw_bias, conv_ln_gain=conv_ln_gain, conv_ln_bias=conv_ln_bias, attn_sinks=attn_sinks, rel_bias=rel_bias, w_br_a=w_br_a, w_br_b=w_br_b, w_br_c=w_br_c, w_br_d=w_br_d, w_o=w_o, mix_norm_post=mix_norm_post, ffn2_norm_pre=ffn2_norm_pre, ffn2_w_gu=ffn2_w_gu, ffn2_w_down=ffn2_w_down, ffn2_norm_post=ffn2_norm_post, ple_norm_gate=ple_norm_gate, w_ple_gate=w_ple_gate, w_ple=w_ple, ple_norm_post=ple_norm_post, loss_target=loss_target, m_ffn1_norm_pre=m_ffn1_norm_pre, m_ffn1_w_gu=m_ffn1_w_gu, m_ffn1_w_down=m_ffn1_w_down, m_ffn1_norm_post=m_ffn1_norm_post, m_mix_norm_pre=m_mix_norm_pre, m_w_in=m_w_in, m_b_forget=m_b_forget, m_b_gate=m_b_gate, m_conv_short=m_conv_short, m_conv_dw=m_conv_dw, m_conv_dw_bias=m_conv_dw_bias, m_conv_ln_gain=m_conv_ln_gain, m_conv_ln_bias=m_conv_ln_bias, m_attn_sinks=m_attn_sinks, m_rel_bias=m_rel_bias, m_w_br_a=m_w_br_a, m_w_br_b=m_w_br_b, m_w_br_c=m_w_br_c, m_w_br_d=m_w_br_d, m_w_o=m_w_o, m_mix_norm_post=m_mix_norm_post, m_ffn2_norm_pre=m_ffn2_norm_pre, m_ffn2_w_gu=m_ffn2_w_gu, m_ffn2_w_down=m_ffn2_w_down, m_ffn2_norm_post=m_ffn2_norm_post, m_ple_norm_gate=m_ple_norm_gate, m_w_ple_gate=m_w_ple_gate, m_w_ple=m_w_ple, m_ple_norm_post=m_ple_norm_post, v_ffn1_norm_pre=v_ffn1_norm_pre, v_ffn1_w_gu=v_ffn1_w_gu, v_ffn1_w_down=v_ffn1_w_down, v_ffn1_norm_post=v_ffn1_norm_post, v_mix_norm_pre=v_mix_norm_pre, v_w_in=v_w_in, v_b_forget=v_b_forget, v_b_gate=v_b_gate, v_conv_short=v_conv_short, v_conv_dw=v_conv_dw, v_conv_dw_bias=v_conv_dw_bias, v_conv_ln_gain=v_conv_ln_gain, v_conv_ln_bias=v_conv_ln_bias, v_attn_sinks=v_attn_sinks, v_rel_bias=v_rel_bias, v_w_br_a=v_w_br_a, v_w_br_b=v_w_br_b, v_w_br_c=v_w_br_c, v_w_br_d=v_w_br_d, v_w_o=v_w_o, v_mix_norm_post=v_mix_norm_post, v_ffn2_norm_pre=v_ffn2_norm_pre, v_ffn2_w_gu=v_ffn2_w_gu, v_ffn2_w_down=v_ffn2_w_down, v_ffn2_norm_post=v_ffn2_norm_post, v_ple_norm_gate=v_ple_norm_gate, v_w_ple_gate=v_w_ple_gate, v_w_ple=v_w_ple, v_ple_norm_post=v_ple_norm_post)
    weights = {n: given[n] for n in TWIN_WEIGHTS}
    shared = {n: given[n] for n in SHARED_INPUTS}
    per_example = {n: given[n] for n in ['x', 'p']}
    grad_fn = _jax.value_and_grad(_loss, argnums=(0, 1))

    def one_microbatch(ex, loss_target):
        ex = dict(ex)
        diff = ex.pop(TWIN_DIFF_INPUT)
        return grad_fn(weights, diff, {**shared, **ex}, loss_target)

    if N_MICROBATCH == 1:
        loss, (grad_w, grad_x) = one_microbatch(per_example, given["loss_target"])
    else:
        def body(carry, xs):
            loss_sum, grad_sum = carry
            l_k, (gw_k, gx_k) = one_microbatch(xs[0], xs[1])
            with _jax.named_scope("update"):
                return (loss_sum + l_k, _jax.tree.map(_jnp.add, grad_sum, gw_k)), gx_k

        init = (_jnp.zeros((), _jnp.float32), _jax.tree.map(_jnp.zeros_like, weights))
        (loss, grad_w), grad_x = _jax.lax.scan(body, init, (per_example, given["loss_target"]))
    with _jax.named_scope("update"):
        delta_w, new_m, new_v = {}, {}, {}
        for n in TWIN_WEIGHTS:
            delta_w[n], new_m[n], new_v[n] = _adamw(weights[n], grad_w[n], given["m_" + n], given["v_" + n])
    return (loss, grad_x, *[grad_w[n] for n in TWIN_WEIGHTS], *[delta_w[n] for n in TWIN_WEIGHTS],
            *[new_m[n] for n in TWIN_WEIGHTS], *[new_v[n] for n in TWIN_WEIGHTS])
```

```python
import functools
import math

import jax
import jax.numpy as jnp
from jax import lax
from jax.experimental import pallas as pl
from jax.experimental.pallas import tpu as pltpu

F32 = jnp.float32
BF16 = jnp.bfloat16

EPS = 1e-6
NEG_INF = -1e30
HEAD_DIM = 64
A_HEADS = 4
A_WIDTH = A_HEADS * HEAD_DIM
B_WIDTH = 256
C_WIDTH = 256
SHORT_CONV = 3
CONF_CONV = 31
D_Q_HEADS = 8
D_KV_HEADS = 2
D_GROUP = D_Q_HEADS // D_KV_HEADS
D_WIDTH = D_Q_HEADS * HEAD_DIM
D_KV_WIDTH = D_KV_HEADS * HEAD_DIM
WINDOW = 128
Q_BLOCK = 128
N_BRANCH = 4
REL_BUCKETS = 32
REL_MAX_DIST = 128
AF_PAD = 256
AF_LANES = 128

ADAM_LR = 0.001
ADAM_B1 = 0.9
ADAM_B2 = 0.999
ADAM_EPS = 1e-08
ADAM_WD = 0.01
ADAM_STEP = 10

N_DEV = 8
MESH_AXES = ("x", "y", "c")
VMEM_LIMIT_V7X = 56 * 2**20
MESH = pl.DeviceIdType.MESH

WEIGHTS = ['ffn1_norm_pre', 'ffn1_w_gu', 'ffn1_w_down', 'ffn1_norm_post', 'mix_norm_pre', 'w_in', 'b_forget',
           'b_gate', 'conv_short', 'conv_dw', 'conv_dw_bias', 'conv_ln_gain', 'conv_ln_bias', 'attn_sinks',
           'rel_bias', 'w_br_a', 'w_br_b', 'w_br_c', 'w_br_d', 'w_o', 'mix_norm_post', 'ffn2_norm_pre',
           'ffn2_w_gu', 'ffn2_w_down', 'ffn2_norm_post', 'ple_norm_gate', 'w_ple_gate', 'w_ple', 'ple_norm_post']
ARG_NAMES = ['x', 'p'] + WEIGHTS + ['loss_target'] + ['m_' + n for n in WEIGHTS] + ['v_' + n for n in WEIGHTS]
BIG = ['ffn1_w_gu', 'ffn1_w_down', 'w_in', 'w_br_a', 'w_br_b', 'w_br_c', 'w_br_d', 'w_o', 'ffn2_w_gu',
       'ffn2_w_down', 'w_ple_gate', 'w_ple']
COL_SHARDED = ('w_br_a', 'w_br_b', 'w_br_c', 'w_br_d', 'w_ple')
CONV_SHARDED = ('conv_short', 'conv_dw')
SMALL = [n for n in WEIGHTS if n not in BIG and n not in CONV_SHARDED]


def _tile(n, *prefs):
    for t in prefs:
        if n % t == 0:
            return t
    return n


def _params(*sem):
    return pltpu.CompilerParams(dimension_semantics=sem, vmem_limit_bytes=VMEM_LIMIT_V7X)


def _dot(a, b):
    return jnp.dot(a, b, preferred_element_type=F32)


def _dot_nt(a, b):
    return lax.dot_general(a, b, (((1,), (1,)), ((), ())), preferred_element_type=F32)


def _dot_tn(a, b):
    return lax.dot_general(a, b, (((0,), (0,)), ((), ())), preferred_element_type=F32)


def _rstd(x):
    return lax.rsqrt(jnp.mean(x * x, axis=-1, keepdims=True) + EPS)


def _rms_bwd(dy, x, r, g):
    xh = x * r
    dxh = dy * g
    dx = r * (dxh - xh * jnp.mean(dxh * xh, axis=-1, keepdims=True))
    return dx, dy * xh


def _colsum(v):
    return jnp.sum(v, axis=0, keepdims=True)


def _rms_mm(h, g, w, *, nt, out_dtype, save_n, name):
    T, D = h.shape
    P = w.shape[0]
    N = w.shape[1] if nt else w.shape[2]
    tm = _tile(T, 512, 256, 128)
    tn = _tile(N, 2816, 1792, 2048, 1408, 1024, 512, 256, 128)

    def body(h_ref, g_ref, w_ref, y_ref, *rest):
        n_scr = rest[-1]

        @pl.when((pl.program_id(1) == 0) & (pl.program_id(2) == 0))
        def _():
            x = h_ref[...]
            n = (x * _rstd(x) * g_ref[...]).astype(BF16)
            n_scr[...] = n
            if save_n:
                rest[0][...] = n

        wt = w_ref[...]
        y = _dot_nt(n_scr[...], wt) if nt else _dot(n_scr[...], wt)
        y_ref[...] = y.astype(out_dtype)

    w_spec = (pl.BlockSpec((None, tn, D), lambda i, p, j: (p, j, 0)) if nt
              else pl.BlockSpec((None, D, tn), lambda i, p, j: (p, 0, j)))
    out_shape = [jax.ShapeDtypeStruct((P, T, N), out_dtype)]
    out_specs = [pl.BlockSpec((None, tm, tn), lambda i, p, j: (p, i, j))]
    if save_n:
        out_shape.append(jax.ShapeDtypeStruct((T, D), BF16))
        out_specs.append(pl.BlockSpec((tm, D), lambda i, p, j: (i, 0)))
    res = pl.pallas_call(
        body, name=name, grid=(T // tm, P, N // tn),
        in_specs=[pl.BlockSpec((tm, D), lambda i, p, j: (i, 0)), pl.BlockSpec((1, D), lambda i, p, j: (0, 0)), w_spec],
        out_specs=out_specs, out_shape=out_shape,
        scratch_shapes=[pltpu.VMEM((tm, D), BF16)],
        compiler_params=_params("parallel", "arbitrary", "arbitrary"),
    )(h, g, w)
    return res if save_n else res[0]


def _ffn_down(gu, wd, h, gpost, *, name):
    _, T, F = gu.shape
    D = wd.shape[1]
    tm = _tile(T, 512, 256, 128)
    tk = _tile(F, 2816, 1408, 1024, 512, 256, 128)
    nk = F // tk

    def body(g_ref, u_ref, wd_ref, h_ref, gp_ref, hn_ref, f_ref, acc):
        k = pl.program_id(1)

        @pl.when(k == 0)
        def _():
            acc[...] = jnp.zeros_like(acc)

        gt = g_ref[...].astype(F32)
        a = (gt * jax.nn.sigmoid(gt) * u_ref[...].astype(F32)).astype(BF16)
        acc[...] += _dot(a, wd_ref[...])

        @pl.when(k == nk - 1)
        def _():
            f = acc[...]
            f_ref[...] = f
            hn_ref[...] = h_ref[...] + 0.5 * (f * _rstd(f) * gp_ref[...])

    return pl.pallas_call(
        body, name=name, grid=(T // tm, nk),
        in_specs=[pl.BlockSpec((None, tm, tk), lambda i, k: (0, i, k)), pl.BlockSpec((None, tm, tk), lambda i, k: (1, i, k)),
                  pl.BlockSpec((tk, D), lambda i, k: (k, 0)), pl.BlockSpec((tm, D), lambda i, k: (i, 0)),
                  pl.BlockSpec((1, D), lambda i, k: (0, 0))],
        out_specs=[pl.BlockSpec((tm, D), lambda i, k: (i, 0)), pl.BlockSpec((tm, D), lambda i, k: (i, 0))],
        out_shape=[jax.ShapeDtypeStruct((T, D), F32), jax.ShapeDtypeStruct((T, D), F32)],
        scratch_shapes=[pltpu.VMEM((tm, D), F32)],
        compiler_params=_params("parallel", "arbitrary"),
    )(gu, gu, wd, h, gpost)


def _ffn_down_bwd(dh, f, gpost, wd, gu, *, name):
    _, T, F = gu.shape
    D = wd.shape[1]
    tm = _tile(T, 256, 128)
    tn = _tile(F, 2816, 1408, 1024, 512, 256, 128)

    def body(dh_ref, f_ref, gp_ref, wd_ref, g_ref, u_ref, dgu_ref, df_ref, dgp_ref, df_scr):
        i, j = pl.program_id(0), pl.program_id(1)

        @pl.when((i == 0) & (j == 0))
        def _():
            dgp_ref[...] = jnp.zeros_like(dgp_ref)

        @pl.when(j == 0)
        def _():
            x = f_ref[...]
            dx, dgn = _rms_bwd(0.5 * dh_ref[...], x, _rstd(x), gp_ref[...])
            dgp_ref[...] += _colsum(dgn)
            df = dx.astype(BF16)
            df_scr[...] = df
            df_ref[...] = df

        dact = _dot_nt(df_scr[...], wd_ref[...])
        gt = g_ref[...].astype(F32)
        ut = u_ref[...].astype(F32)
        sg = jax.nn.sigmoid(gt)
        dgu_ref[0] = (dact * ut * (sg * (1.0 + gt * (1.0 - sg)))).astype(BF16)
        dgu_ref[1] = (dact * (gt * sg)).astype(BF16)

    return pl.pallas_call(
        body, name=name, grid=(T // tm, F // tn),
        in_specs=[pl.BlockSpec((tm, D), lambda i, j: (i, 0)), pl.BlockSpec((tm, D), lambda i, j: (i, 0)),
                  pl.BlockSpec((1, D), lambda i, j: (0, 0)), pl.BlockSpec((tn, D), lambda i, j: (j, 0)),
                  pl.BlockSpec((None, tm, tn), lambda i, j: (0, i, j)), pl.BlockSpec((None, tm, tn), lambda i, j: (1, i, j))],
        out_specs=[pl.BlockSpec((2, tm, tn), lambda i, j: (0, i, j)), pl.BlockSpec((tm, D), lambda i, j: (i, 0)),
                   pl.BlockSpec((1, D), lambda i, j: (0, 0))],
        out_shape=[jax.ShapeDtypeStruct((2, T, F), BF16), jax.ShapeDtypeStruct((T, D), BF16),
                   jax.ShapeDtypeStruct((1, D), F32)],
        scratch_shapes=[pltpu.VMEM((tm, D), BF16)],
        compiler_params=_params("arbitrary", "arbitrary"),
    )(dh, f, gpost, wd, gu, gu)


def _mm_rmsbwd(a, b, dh_in, h, g, *, nt, name):
    P, T, K = a.shape
    D = h.shape[1]
    tm = _tile(T, 512, 256, 128)
    tk = _tile(K, 1792, 1408, 2048, 1024, 512, 256, 128)
    nk = K // tk

    def body(a_ref, b_ref, dh_ref, h_ref, g_ref, out_ref, dg_ref, acc):
        i, p, k = pl.program_id(0), pl.program_id(1), pl.program_id(2)

        @pl.when((i == 0) & (p == 0) & (k == 0))
        def _():
            dg_ref[...] = jnp.zeros_like(dg_ref)

        @pl.when((p == 0) & (k == 0))
        def _():
            acc[...] = jnp.zeros_like(acc)

        acc[...] += _dot_nt(a_ref[...], b_ref[...]) if nt else _dot(a_ref[...], b_ref[...])

        @pl.when((p == P - 1) & (k == nk - 1))
        def _():
            x = h_ref[...]
            dx, dgn = _rms_bwd(acc[...], x, _rstd(x), g_ref[...])
            dg_ref[...] += _colsum(dgn)
            out_ref[...] = dh_ref[...] + dx

    b_spec = (pl.BlockSpec((None, D, tk), lambda i, p, k: (p, 0, k)) if nt
              else pl.BlockSpec((None, tk, D), lambda i, p, k: (p, k, 0)))
    return pl.pallas_call(
        body, name=name, grid=(T // tm, P, nk),
        in_specs=[pl.BlockSpec((None, tm, tk), lambda i, p, k: (p, i, k)), b_spec,
                  pl.BlockSpec((tm, D), lambda i, p, k: (i, 0)), pl.BlockSpec((tm, D), lambda i, p, k: (i, 0)),
                  pl.BlockSpec((1, D), lambda i, p, k: (0, 0))],
        out_specs=[pl.BlockSpec((tm, D), lambda i, p, k: (i, 0)), pl.BlockSpec((1, D), lambda i, p, k: (0, 0))],
        out_shape=[jax.ShapeDtypeStruct((T, D), F32), jax.ShapeDtypeStruct((1, D), F32)],
        scratch_shapes=[pltpu.VMEM((tm, D), F32)],
        compiler_params=_params("arbitrary", "arbitrary", "arbitrary"),
    )(a, b, dh_in, h, g)


def _mm_tn(a, b, buf, layer, n_layers, *, swiglu=False, b_plane=0, name):
    T, N = b.shape[-2:]
    K = a.shape[2]
    P = 1 if swiglu else a.shape[0]
    tk = _tile(K, 2816, 1024, 512, 256, 128)
    tn = _tile(N, 1792, 1024, 512, 256, 128)
    tt = _tile(T, 512, 256, 128)
    nt_ = T // tt
    nkb = K // tk

    def body(*refs):
        if swiglu:
            g_ref, u_ref, b_ref = refs[:3]
        else:
            a_ref, b_ref = refs[:2]
        out_ref, acc = refs[-2], refs[-1]
        t = pl.program_id(3)

        @pl.when(t == 0)
        def _():
            acc[...] = jnp.zeros_like(acc)

        if swiglu:
            gt = g_ref[...].astype(F32)
            at = (gt * jax.nn.sigmoid(gt) * u_ref[...].astype(F32)).astype(BF16)
        else:
            at = a_ref[...].astype(BF16)
        acc[...] += _dot_tn(at, b_ref[...].astype(BF16))

        @pl.when(t == nt_ - 1)
        def _():
            out_ref[...] = acc[...].astype(BF16)

    if swiglu:
        a_specs = [pl.BlockSpec((None, tt, tk), lambda p, i, j, t: (0, t, i)),
                   pl.BlockSpec((None, tt, tk), lambda p, i, j, t: (1, t, i))]
        a_args = [a, a]
    else:
        a_specs = [pl.BlockSpec((None, tt, tk), lambda p, i, j, t: (p, t, i))]
        a_args = [a]
    if b.ndim == 3:
        in_specs = a_specs + [pl.BlockSpec((None, tt, tn), lambda p, i, j, t: (b_plane, t, j))]
    else:
        in_specs = a_specs + [pl.BlockSpec((tt, tn), lambda p, i, j, t: (t, j))]
    args = a_args + [b]
    aliases = {}
    if buf is not None:
        in_specs.append(pl.BlockSpec(memory_space=pl.ANY))
        args.append(buf)
        aliases = {len(args) - 1: 0}
    return pl.pallas_call(
        body, name=name, grid=(P, nkb, N // tn, nt_),
        in_specs=in_specs,
        out_specs=pl.BlockSpec((None, tk, tn), lambda p, i, j, t: (layer, p * nkb + i, j)),
        out_shape=jax.ShapeDtypeStruct((n_layers, P * K, N), BF16),
        scratch_shapes=[pltpu.VMEM((tk, tn), F32)],
        input_output_aliases=aliases,
        compiler_params=_params("arbitrary", "arbitrary", "arbitrary", "arbitrary"),
    )(*args)


def _row(D):
    return pl.BlockSpec((1, D), lambda i: (0, 0))


def _full(shape):
    return pl.BlockSpec(shape, lambda i: (0,) * len(shape))


def _merge_out(ys, proj, bgate, wbrs, wo, h, gpost, *, name):
    T, D = h.shape
    tm = _tile(T, 512, 256, 128)

    def body(ya, yb, yc, yd, g0, g1, g2, g3, bg_ref, wa, wb, wc, wd_, wo_ref, h_ref, gp_ref, hn_ref, o_ref, mg_ref):
        merged = jnp.zeros((tm, D), F32)
        for b, (y_ref, gt_ref, w_ref) in enumerate(zip((ya, yb, yc, yd), (g0, g1, g2, g3), (wa, wb, wc, wd_))):
            gate = jax.nn.sigmoid(gt_ref[...].astype(F32) + bg_ref[b:b + 1, :])
            merged = merged + gate * _dot(y_ref[...], w_ref[...])
        mb = merged.astype(BF16)
        mg_ref[...] = mb
        o = _dot(mb, wo_ref[...])
        o_ref[...] = o
        hn_ref[...] = h_ref[...] + o * _rstd(o) * gp_ref[...]

    tok = lambda w: pl.BlockSpec((tm, w), lambda i: (i, 0))
    gate_specs = [pl.BlockSpec((tm, D), lambda i, b=b: (i, b)) for b in range(N_BRANCH)]
    return pl.pallas_call(
        body, name=name, grid=(T // tm,),
        in_specs=[tok(A_WIDTH), tok(B_WIDTH), tok(C_WIDTH), tok(D_WIDTH)] + gate_specs
        + [_full((N_BRANCH, D))] + [_full(w.shape) for w in wbrs] + [_full((D, D)), tok(D), _row(D)],
        out_specs=[tok(D), tok(D), tok(D)],
        out_shape=[jax.ShapeDtypeStruct((T, D), F32), jax.ShapeDtypeStruct((T, D), F32), jax.ShapeDtypeStruct((T, D), BF16)],
        compiler_params=_params("parallel"),
    )(*ys, proj, proj, proj, proj, bgate, *wbrs, wo, h, gpost)


def _merge_out_bwd(dh, o, gpost, wo, ys, proj, bgate, wbrs, *, name):
    T, D = o.shape
    tm = _tile(T, 256, 128)
    widths = (A_WIDTH, B_WIDTH, C_WIDTH, D_WIDTH)

    def body(dh_ref, o_ref, gp_ref, wo_ref, ya, yb, yc, yd, g0, g1, g2, g3, bg_ref, wa, wb, wc, wd_,
             do_ref, dz_ref, dya, dyb, dyc, dyd, dgt_ref, dgp_ref, dbg_ref):
        @pl.when(pl.program_id(0) == 0)
        def _():
            dgp_ref[...] = jnp.zeros_like(dgp_ref)
            dbg_ref[...] = jnp.zeros_like(dbg_ref)

        x = o_ref[...]
        do, dgn = _rms_bwd(dh_ref[...], x, _rstd(x), gp_ref[...])
        dgp_ref[...] += _colsum(dgn)
        dob = do.astype(BF16)
        do_ref[...] = dob
        dmerged = _dot_nt(dob, wo_ref[...])
        for b, (y_ref, gt_ref, w_ref, dy_ref) in enumerate(zip((ya, yb, yc, yd), (g0, g1, g2, g3), (wa, wb, wc, wd_),
                                                               (dya, dyb, dyc, dyd))):
            gate = jax.nn.sigmoid(gt_ref[...].astype(F32) + bg_ref[b:b + 1, :])
            z = _dot(y_ref[...], w_ref[...])
            dz = (dmerged * gate).astype(BF16)
            dz_ref[b] = dz
            dy_ref[...] = _dot_nt(dz, w_ref[...]).astype(BF16)
            dgate = dmerged * z * gate * (1.0 - gate)
            dgt_ref[:, b * D:(b + 1) * D] = dgate.astype(BF16)
            dbg_ref[b:b + 1, :] += _colsum(dgate)

    tok = lambda w: pl.BlockSpec((tm, w), lambda i: (i, 0))
    gate_specs = [pl.BlockSpec((tm, D), lambda i, b=b: (i, b)) for b in range(N_BRANCH)]
    return pl.pallas_call(
        body, name=name, grid=(T // tm,),
        in_specs=[tok(D), tok(D), _row(D), _full((D, D))] + [tok(w) for w in widths] + gate_specs
        + [_full((N_BRANCH, D))] + [_full(w.shape) for w in wbrs],
        out_specs=[tok(D), pl.BlockSpec((N_BRANCH, tm, D), lambda i: (0, i, 0))] + [tok(w) for w in widths]
        + [tok(N_BRANCH * D), _row(D), _full((N_BRANCH, D))],
        out_shape=[jax.ShapeDtypeStruct((T, D), BF16), jax.ShapeDtypeStruct((N_BRANCH, T, D), BF16)]
        + [jax.ShapeDtypeStruct((T, w), BF16) for w in widths]
        + [jax.ShapeDtypeStruct((T, N_BRANCH * D), BF16), jax.ShapeDtypeStruct((1, D), F32),
           jax.ShapeDtypeStruct((N_BRANCH, D), F32)],
        compiler_params=_params("arbitrary"),
    )(dh, o, gpost, wo, *ys, proj, proj, proj, proj, bgate, *wbrs)


def _ple(h, pe, ggate, wpg, wple, gpost, *, name):
    T, D = h.shape
    E = pe.shape[1]
    tm = _tile(T, 512, 256, 128)

    def body(h_ref, p_ref, gg_ref, wpg_ref, wple_ref, gp_ref, out_ref):
        x = h_ref[...]
        n = (x * _rstd(x) * gg_ref[...]).astype(BF16)
        pg = jax.nn.sigmoid(_dot(n, wpg_ref[...]))
        e = _dot(p_ref[...].astype(BF16), wple_ref[...])
        out_ref[...] = x + pg * (e * _rstd(e) * gp_ref[...])

    tok = lambda w: pl.BlockSpec((tm, w), lambda i: (i, 0))
    return pl.pallas_call(
        body, name=name, grid=(T // tm,),
        in_specs=[tok(D), tok(E), _row(D), _full((D, D)), _full((E, D)), _row(D)],
        out_specs=tok(D), out_shape=jax.ShapeDtypeStruct((T, D), F32),
        compiler_params=_params("parallel"),
    )(h, pe, ggate, wpg, wple, gpost)


def _ple_bwd(dh, h, pe, ggate, wpg, wple, gpost, *, name):
    T, D = h.shape
    E = pe.shape[1]
    tm = _tile(T, 256, 128)

    def body(dh_ref, h_ref, p_ref, gg_ref, wpg_ref, wple_ref, gp_ref, dhi_ref, de_ref, dpgl_ref, n_ref, dgg_ref, dgp_ref):
        @pl.when(pl.program_id(0) == 0)
        def _():
            dgg_ref[...] = jnp.zeros_like(dgg_ref)
            dgp_ref[...] = jnp.zeros_like(dgp_ref)

        dh = dh_ref[...]
        x = h_ref[...]
        r = _rstd(x)
        n = (x * r * gg_ref[...]).astype(BF16)
        n_ref[...] = n
        pg = jax.nn.sigmoid(_dot(n, wpg_ref[...]))
        e = _dot(p_ref[...].astype(BF16), wple_ref[...])
        re = _rstd(e)
        de, dgn = _rms_bwd(dh * pg, e, re, gp_ref[...])
        dgp_ref[...] += _colsum(dgn)
        de_ref[...] = de.astype(BF16)
        dpgl = (dh * (e * re * gp_ref[...]) * pg * (1.0 - pg)).astype(BF16)
        dpgl_ref[...] = dpgl
        dn = _dot_nt(dpgl, wpg_ref[...])
        dx, dgn2 = _rms_bwd(dn, x, r, gg_ref[...])
        dgg_ref[...] += _colsum(dgn2)
        dhi_ref[...] = dh + dx

    tok = lambda w: pl.BlockSpec((tm, w), lambda i: (i, 0))
    return pl.pallas_call(
        body, name=name, grid=(T // tm,),
        in_specs=[tok(D), tok(D), tok(E), _row(D), _full((D, D)), _full((E, D)), _row(D)],
        out_specs=[tok(D), tok(D), tok(D), tok(D), _row(D), _row(D)],
        out_shape=[jax.ShapeDtypeStruct((T, D), F32), jax.ShapeDtypeStruct((T, D), BF16), jax.ShapeDtypeStruct((T, D), BF16),
                   jax.ShapeDtypeStruct((T, D), BF16), jax.ShapeDtypeStruct((1, D), F32), jax.ShapeDtypeStruct((1, D), F32)],
        compiler_params=_params("arbitrary"),
    )(dh, h, pe, ggate, wpg, wple, gpost)


def _loss_head(y, target, *, name):
    T, D = y.shape
    tm = _tile(T, 512, 256, 128)

    def body(y_ref, t_ref, l_ref, dy_ref):
        @pl.when(pl.program_id(0) == 0)
        def _():
            l_ref[...] = jnp.zeros_like(l_ref)

        err = y_ref[...] - t_ref[...]
        dy_ref[...] = err / D
        l_ref[...] += 0.5 * jnp.sum(jnp.mean(err * err, axis=-1, keepdims=True), axis=0, keepdims=True)

    tok = pl.BlockSpec((tm, D), lambda i: (i, 0))
    return pl.pallas_call(
        body, name=name, grid=(T // tm,),
        in_specs=[tok, tok], out_specs=[_full((8, 128)), tok],
        out_shape=[jax.ShapeDtypeStruct((8, 128), F32), jax.ShapeDtypeStruct((T, D), F32)],
        compiler_params=_params("arbitrary"),
    )(y, target)


def _layout(D):
    off = {'gates': 0}
    off['a'] = N_BRANCH * D
    off['b'] = off['a'] + 3 * A_WIDTH
    off['c'] = off['b'] + 3 * B_WIDTH
    off['d'] = off['c'] + 2 * C_WIDTH
    off['af'] = off['d'] + D_WIDTH + 2 * D_KV_WIDTH
    off['end'] = off['af'] + AF_PAD
    return off


def _seq_spec(S, width, col):
    assert col % width == 0
    return pl.BlockSpec((None, S, width), lambda b: (b, 0, col // width))


def _split3(x):
    hi = x.astype(BF16)
    r1 = x - hi.astype(F32)
    mid = r1.astype(BF16)
    lo = (r1 - mid.astype(F32)).astype(BF16)
    return hi, mid, lo


def _fox_cumsum(af_ref, bf_ref, c_scr, ct_scr):
    S = af_ref.shape[0]
    cb = _tile(S, 256, 128)
    tril = (lax.broadcasted_iota(jnp.int32, (cb, cb), 0) >= lax.broadcasted_iota(jnp.int32, (cb, cb), 1)).astype(BF16)
    carry = jnp.zeros((1, AF_LANES), F32)
    for j in range(S // cb):
        rows = slice(j * cb, (j + 1) * cb)
        hi, mid, lo = _split3(jax.nn.log_sigmoid(af_ref[rows, :] + bf_ref[...]))
        cblk = _dot(tril, hi) + _dot(tril, mid) + _dot(tril, lo) + carry
        c_scr[rows, :] = cblk
        carry = cblk[cb - 1:cb, :]
    ct_scr[...] = c_scr[...].T


def _fox_probs(q_ref, k_ref, c_scr, ct_scr, h, i, bq):
    end = (i + 1) * bq
    qs, hs = slice(i * bq, end), slice(HEAD_DIM * h, HEAD_DIM * (h + 1))
    s = _dot_nt(q_ref[qs, hs], k_ref[0:end, hs]) * HEAD_DIM ** -0.5
    s = s + (c_scr[qs, h:h + 1] - ct_scr[h:h + 1, 0:end])
    row = i * bq + lax.broadcasted_iota(jnp.int32, (bq, end), 0)
    col = lax.broadcasted_iota(jnp.int32, (bq, end), 1)
    s = jnp.where(row >= col, s, NEG_INF)
    e = jnp.exp(s - jnp.max(s, axis=-1, keepdims=True))
    return e / jnp.sum(e, axis=-1, keepdims=True)


def _fox_fwd(proj3, af3, bfor, off, *, name):
    Bn, S, _ = proj3.shape
    bq = _tile(S, 256, 128)

    def body(q_ref, k_ref, v_ref, af_ref, bf_ref, o_ref, c_scr, ct_scr):
        _fox_cumsum(af_ref, bf_ref, c_scr, ct_scr)
        for h in range(A_HEADS):
            hs = slice(HEAD_DIM * h, HEAD_DIM * (h + 1))
            for i in range(S // bq):
                p = _fox_probs(q_ref, k_ref, c_scr, ct_scr, h, i, bq)
                o_ref[i * bq:(i + 1) * bq, hs] = _dot(p.astype(BF16), v_ref[0:(i + 1) * bq, hs]).astype(BF16)

    return pl.pallas_call(
        body, name=name, grid=(Bn,),
        in_specs=[_seq_spec(S, A_WIDTH, off['a']), _seq_spec(S, A_WIDTH, off['a'] + A_WIDTH),
                  _seq_spec(S, A_WIDTH, off['a'] + 2 * A_WIDTH), _seq_spec(S, AF_LANES, 0), _full((1, AF_LANES))],
        out_specs=_seq_spec(S, A_WIDTH, 0), out_shape=jax.ShapeDtypeStruct((Bn, S, A_WIDTH), BF16),
        scratch_shapes=[pltpu.VMEM((S, AF_LANES), F32), pltpu.VMEM((AF_LANES, S), F32)],
        compiler_params=_params("parallel"),
    )(proj3, proj3, proj3, af3, bfor)


def _fox_bwd(proj3, af3, bfor, dya3, off, *, name):
    Bn, S, _ = proj3.shape
    bq = _tile(S, 256, 128)
    cb = _tile(S, 256, 128)
    scale = HEAD_DIM ** -0.5

    def body(q_ref, k_ref, v_ref, af_ref, bf_ref, do_ref, dqkv_ref, da_ref, dbf_ref,
             c_scr, ct_scr, dk_scr, dv_scr, dc_scr, dct_scr):
        @pl.when(pl.program_id(0) == 0)
        def _():
            dbf_ref[...] = jnp.zeros_like(dbf_ref)

        _fox_cumsum(af_ref, bf_ref, c_scr, ct_scr)
        dk_scr[...] = jnp.zeros_like(dk_scr)
        dv_scr[...] = jnp.zeros_like(dv_scr)
        dc_scr[...] = jnp.zeros_like(dc_scr)
        dct_scr[...] = jnp.zeros_like(dct_scr)
        for h in range(A_HEADS):
            hs = slice(HEAD_DIM * h, HEAD_DIM * (h + 1))
            for i in range(S // bq):
                end = (i + 1) * bq
                qs = slice(i * bq, end)
                p = _fox_probs(q_ref, k_ref, c_scr, ct_scr, h, i, bq)
                doh = do_ref[qs, hs]
                dp = _dot_nt(doh, v_ref[0:end, hs])
                ds = p * (dp - jnp.sum(p * dp, axis=-1, keepdims=True))
                dsb = ds.astype(BF16)
                dqkv_ref[qs, hs] = (_dot(dsb, k_ref[0:end, hs]) * scale).astype(BF16)
                dk_scr[0:end, hs] += _dot_tn(dsb, q_ref[qs, hs]) * scale
                dv_scr[0:end, hs] += _dot_tn(p.astype(BF16), doh)
                dc_scr[qs, h:h + 1] += jnp.sum(ds, axis=-1, keepdims=True)
                dct_scr[h:h + 1, 0:end] += -jnp.sum(ds, axis=0, keepdims=True)
        dqkv_ref[:, A_WIDTH:2 * A_WIDTH] = dk_scr[...].astype(BF16)
        dqkv_ref[:, 2 * A_WIDTH:3 * A_WIDTH] = dv_scr[...].astype(BF16)
        dc_scr[...] += dct_scr[...].T
        triu = (lax.broadcasted_iota(jnp.int32, (cb, cb), 0) <= lax.broadcasted_iota(jnp.int32, (cb, cb), 1)).astype(BF16)
        carry = jnp.zeros((1, AF_LANES), F32)
        dbf = jnp.zeros((1, AF_LANES), F32)
        for j in reversed(range(S // cb)):
            rows = slice(j * cb, (j + 1) * cb)
            hi, mid, lo = _split3(dc_scr[rows, :])
            dlf = _dot(triu, hi) + _dot(triu, mid) + _dot(triu, lo) + carry
            carry = dlf[0:1, :]
            da = dlf * jax.nn.sigmoid(-(af_ref[rows, :] + bf_ref[...]))
            dbf = dbf + _colsum(da)
            da_ref[rows, 0:AF_LANES] = da.astype(BF16)
        da_ref[:, AF_LANES:AF_PAD] = jnp.zeros((S, AF_PAD - AF_LANES), BF16)
        dbf_ref[...] += dbf

    return pl.pallas_call(
        body, name=name, grid=(Bn,),
        in_specs=[_seq_spec(S, A_WIDTH, off['a']), _seq_spec(S, A_WIDTH, off['a'] + A_WIDTH),
                  _seq_spec(S, A_WIDTH, off['a'] + 2 * A_WIDTH), _seq_spec(S, AF_LANES, 0), _full((1, AF_LANES)),
                  _seq_spec(S, A_WIDTH, 0)],
        out_specs=[_seq_spec(S, 3 * A_WIDTH, 0), _seq_spec(S, AF_PAD, 0), _full((1, AF_LANES))],
        out_shape=[jax.ShapeDtypeStruct((Bn, S, 3 * A_WIDTH), BF16), jax.ShapeDtypeStruct((Bn, S, AF_PAD), BF16),
                   jax.ShapeDtypeStruct((1, AF_LANES), F32)],
        scratch_shapes=[pltpu.VMEM((S, AF_LANES), F32), pltpu.VMEM((AF_LANES, S), F32), pltpu.VMEM((S, A_WIDTH), F32),
                        pltpu.VMEM((S, A_WIDTH), F32), pltpu.VMEM((S, AF_LANES), F32), pltpu.VMEM((AF_LANES, S), F32)],
        compiler_params=_params("arbitrary"),
    )(proj3, proj3, proj3, af3, bfor, dya3)


def _shift_down(z, s):
    if s == 0:
        return z
    row = lax.broadcasted_iota(jnp.int32, z.shape, 0)
    return jnp.where(row >= s, pltpu.roll(z, s, 0), 0.0)


def _shift_up(z, s):
    if s == 0:
        return z
    n = z.shape[0]
    row = lax.broadcasted_iota(jnp.int32, z.shape, 0)
    return jnp.where(row < n - s, pltpu.roll(z, n - s, 0), 0.0)


def _conv_fwd(z, w_ref, K):
    acc = jnp.zeros_like(z)
    for k in range(K):
        acc = acc + w_ref[k:k + 1, :] * _shift_down(z, K - 1 - k)
    return acc


def _conv_bwd(dy, z, w_ref, dw_ref, K):
    dz = jnp.zeros_like(z)
    for k in range(K):
        dz = dz + w_ref[k:k + 1, :] * _shift_up(dy, K - 1 - k)
        dw_ref[k:k + 1, :] += _colsum(dy * _shift_down(z, K - 1 - k))
    return dz


def _sconv_fwd(proj3, w, off, *, name):
    Bn, S, _ = proj3.shape

    def body(bg_ref, cg_ref, xb_ref, w_ref, o_ref):
        z = cg_ref[...].astype(F32) * xb_ref[...].astype(F32)
        o_ref[...] = (bg_ref[...].astype(F32) * _conv_fwd(z, w_ref, SHORT_CONV)).astype(BF16)

    return pl.pallas_call(
        body, name=name, grid=(Bn,),
        in_specs=[_seq_spec(S, B_WIDTH, off['b'] + j * B_WIDTH) for j in range(3)] + [_full((SHORT_CONV, B_WIDTH))],
        out_specs=_seq_spec(S, B_WIDTH, 0), out_shape=jax.ShapeDtypeStruct((Bn, S, B_WIDTH), BF16),
        compiler_params=_params("parallel"),
    )(proj3, proj3, proj3, w)


def _sconv_bwd(proj3, w, dyb3, off, *, name):
    Bn, S, _ = proj3.shape

    def body(bg_ref, cg_ref, xb_ref, w_ref, do_ref, din_ref, dw_ref):
        @pl.when(pl.program_id(0) == 0)
        def _():
            dw_ref[...] = jnp.zeros_like(dw_ref)

        cg, xb = cg_ref[...].astype(F32), xb_ref[...].astype(F32)
        z = cg * xb
        do = do_ref[...].astype(F32)
        din_ref[:, 0:B_WIDTH] = (do * _conv_fwd(z, w_ref, SHORT_CONV)).astype(BF16)
        dz = _conv_bwd(do * bg_ref[...].astype(F32), z, w_ref, dw_ref, SHORT_CONV)
        din_ref[:, B_WIDTH:2 * B_WIDTH] = (dz * xb).astype(BF16)
        din_ref[:, 2 * B_WIDTH:3 * B_WIDTH] = (dz * cg).astype(BF16)

    return pl.pallas_call(
        body, name=name, grid=(Bn,),
        in_specs=[_seq_spec(S, B_WIDTH, off['b'] + j * B_WIDTH) for j in range(3)]
        + [_full((SHORT_CONV, B_WIDTH)), _seq_spec(S, B_WIDTH, 0)],
        out_specs=[_seq_spec(S, 3 * B_WIDTH, 0), _full((SHORT_CONV, B_WIDTH))],
        out_shape=[jax.ShapeDtypeStruct((Bn, S, 3 * B_WIDTH), BF16), jax.ShapeDtypeStruct((SHORT_CONV, B_WIDTH), F32)],
        compiler_params=_params("arbitrary"),
    )(proj3, proj3, proj3, w, dyb3)


def _cconv_pre(cin_ref, w_ref, cb_ref):
    x = cin_ref[...].astype(F32)
    a, gt = x[:, 0:C_WIDTH], x[:, C_WIDTH:2 * C_WIDTH]
    sg = jax.nn.sigmoid(gt)
    glu = a * sg
    y0 = _conv_fwd(glu, w_ref, CONF_CONV) + cb_ref[...]
    mu = jnp.mean(y0, axis=-1, keepdims=True)
    xc = y0 - mu
    rs = lax.rsqrt(jnp.mean(xc * xc, axis=-1, keepdims=True) + EPS)
    return a, sg, glu, xc * rs, rs


def _cconv_fwd(proj3, w, cbias, lg, lb, off, *, name):
    Bn, S, _ = proj3.shape

    def body(cin_ref, w_ref, cb_ref, lg_ref, lb_ref, o_ref):
        _, _, _, xh, _ = _cconv_pre(cin_ref, w_ref, cb_ref)
        ln = xh * lg_ref[...] + lb_ref[...]
        o_ref[...] = (ln * jax.nn.sigmoid(ln)).astype(BF16)

    return pl.pallas_call(
        body, name=name, grid=(Bn,),
        in_specs=[_seq_spec(S, 2 * C_WIDTH, off['c']), _full((CONF_CONV, C_WIDTH)), _full((1, C_WIDTH)),
                  _full((1, C_WIDTH)), _full((1, C_WIDTH))],
        out_specs=_seq_spec(S, C_WIDTH, 0), out_shape=jax.ShapeDtypeStruct((Bn, S, C_WIDTH), BF16),
        compiler_params=_params("parallel"),
    )(proj3, w, cbias, lg, lb)


def _cconv_bwd(proj3, w, cbias, lg, lb, dyc3, off, *, name):
    Bn, S, _ = proj3.shape

    def body(cin_ref, w_ref, cb_ref, lg_ref, lb_ref, do_ref, din_ref, dw_ref, dcb_ref, dlg_ref, dlb_ref):
        @pl.when(pl.program_id(0) == 0)
        def _():
            for r in (dw_ref, dcb_ref, dlg_ref, dlb_ref):
                r[...] = jnp.zeros_like(r)

        a, sg, glu, xh, rs = _cconv_pre(cin_ref, w_ref, cb_ref)
        ln = xh * lg_ref[...] + lb_ref[...]
        sl = jax.nn.sigmoid(ln)
        dln = do_ref[...].astype(F32) * (sl * (1.0 + ln * (1.0 - sl)))
        dlg_ref[...] += _colsum(dln * xh)
        dlb_ref[...] += _colsum(dln)
        dxh = dln * lg_ref[...]
        dy0 = rs * (dxh - jnp.mean(dxh, axis=-1, keepdims=True) - xh * jnp.mean(dxh * xh, axis=-1, keepdims=True))
        dcb_ref[...] += _colsum(dy0)
        dglu = _conv_bwd(dy0, glu, w_ref, dw_ref, CONF_CONV)
        din_ref[:, 0:C_WIDTH] = (dglu * sg).astype(BF16)
        din_ref[:, C_WIDTH:2 * C_WIDTH] = (dglu * a * sg * (1.0 - sg)).astype(BF16)

    vec = _full((1, C_WIDTH))
    return pl.pallas_call(
        body, name=name, grid=(Bn,),
        in_specs=[_seq_spec(S, 2 * C_WIDTH, off['c']), _full((CONF_CONV, C_WIDTH)), vec, vec, vec, _seq_spec(S, C_WIDTH, 0)],
        out_specs=[_seq_spec(S, 2 * C_WIDTH, 0), _full((CONF_CONV, C_WIDTH)), vec, vec, vec],
        out_shape=[jax.ShapeDtypeStruct((Bn, S, 2 * C_WIDTH), BF16), jax.ShapeDtypeStruct((CONF_CONV, C_WIDTH), F32)]
        + [jax.ShapeDtypeStruct((1, C_WIDTH), F32)] * 3,
        compiler_params=_params("arbitrary"),
    )(proj3, w, cbias, lg, lb, dyc3)


def _swa_band(x_ref, g, nb):
    xb = x_ref[:, HEAD_DIM * g:HEAD_DIM * (g + 1)].reshape(nb, Q_BLOCK, HEAD_DIM)
    prev = jnp.concatenate([jnp.zeros((1, Q_BLOCK, HEAD_DIM), xb.dtype), xb[:-1]], axis=0)
    return jnp.concatenate([prev, xb], axis=1)


def _swa_probs(q_ref, kband, bias_ref, sk_ref, h, nb):
    qh = q_ref[:, HEAD_DIM * h:HEAD_DIM * (h + 1)].reshape(nb, Q_BLOCK, HEAD_DIM)
    s = jnp.einsum('nqd,nsd->nqs', qh, kband, preferred_element_type=F32) * HEAD_DIM ** -0.5 + bias_ref[h][None]
    shape = (nb, Q_BLOCK, 2 * Q_BLOCK)
    n = lax.broadcasted_iota(jnp.int32, shape, 0)
    dist = lax.broadcasted_iota(jnp.int32, shape, 1) + Q_BLOCK - lax.broadcasted_iota(jnp.int32, shape, 2)
    col = lax.broadcasted_iota(jnp.int32, shape, 2)
    valid = (dist >= 0) & (dist < WINDOW) & ((n > 0) | (col >= Q_BLOCK))
    s = jnp.where(valid, s, NEG_INF)
    sink = sk_ref[h:h + 1, 0:1].reshape(1, 1, 1)
    m = jnp.maximum(jnp.max(s, axis=-1, keepdims=True), sink)
    e = jnp.exp(s - m)
    es = jnp.exp(sink - m)
    den = jnp.sum(e, axis=-1, keepdims=True) + es
    return qh, e / den, es / den


def _swa_fwd(proj3, band_bias, sinks, off, *, name):
    Bn, S, _ = proj3.shape
    nb = S // Q_BLOCK

    def body(q_ref, k_ref, v_ref, bias_ref, sk_ref, o_ref):
        for g in range(D_KV_HEADS):
            kband, vband = _swa_band(k_ref, g, nb), _swa_band(v_ref, g, nb)
            for h in range(g * D_GROUP, (g + 1) * D_GROUP):
                _, p, _ = _swa_probs(q_ref, kband, bias_ref, sk_ref, h, nb)
                out = jnp.einsum('nqs,nsd->nqd', p.astype(BF16), vband, preferred_element_type=F32)
                o_ref[:, HEAD_DIM * h:HEAD_DIM * (h + 1)] = out.reshape(S, HEAD_DIM).astype(BF16)

    kcol = off['d'] + D_WIDTH
    return pl.pallas_call(
        body, name=name, grid=(Bn,),
        in_specs=[_seq_spec(S, D_WIDTH, off['d']), _seq_spec(S, D_KV_WIDTH, kcol), _seq_spec(S, D_KV_WIDTH, kcol + D_KV_WIDTH),
                  _full((D_Q_HEADS, Q_BLOCK, 2 * Q_BLOCK)), _full((D_Q_HEADS, 128))],
        out_specs=_seq_spec(S, D_WIDTH, 0), out_shape=jax.ShapeDtypeStruct((Bn, S, D_WIDTH), BF16),
        compiler_params=_params("parallel"),
    )(proj3, proj3, proj3, band_bias, sinks)


def _swa_bwd(proj3, band_bias, sinks, dyd3, off, *, name):
    Bn, S, _ = proj3.shape
    nb = S // Q_BLOCK
    scale = HEAD_DIM ** -0.5

    def body(q_ref, k_ref, v_ref, bias_ref, sk_ref, do_ref, dqkv_ref, dband_ref, dsk_ref):
        @pl.when(pl.program_id(0) == 0)
        def _():
            dband_ref[...] = jnp.zeros_like(dband_ref)
            dsk_ref[...] = jnp.zeros_like(dsk_ref)

        def unband(acc):
            prev, cur = acc[:, 0:Q_BLOCK, :], acc[:, Q_BLOCK:2 * Q_BLOCK, :]
            nxt = jnp.concatenate([prev[1:], jnp.zeros((1, Q_BLOCK, HEAD_DIM), F32)], axis=0)
            return (cur + nxt).reshape(S, HEAD_DIM).astype(BF16)

        for g in range(D_KV_HEADS):
            kband, vband = _swa_band(k_ref, g, nb), _swa_band(v_ref, g, nb)
            dkb = jnp.zeros((nb, 2 * Q_BLOCK, HEAD_DIM), F32)
            dvb = jnp.zeros((nb, 2 * Q_BLOCK, HEAD_DIM), F32)
            for h in range(g * D_GROUP, (g + 1) * D_GROUP):
                hs = slice(HEAD_DIM * h, HEAD_DIM * (h + 1))
                qh, p, ps = _swa_probs(q_ref, kband, bias_ref, sk_ref, h, nb)
                doh = do_ref[:, hs].reshape(nb, Q_BLOCK, HEAD_DIM)
                dp = jnp.einsum('nqd,nsd->nqs', doh, vband, preferred_element_type=F32)
                delta = jnp.sum(p * dp, axis=-1, keepdims=True)
                ds = p * (dp - delta)
                dsink = jnp.sum(jnp.sum(-ps * delta, axis=0), axis=0, keepdims=True)
                dsk_ref[h:h + 1, :] += jnp.broadcast_to(dsink, (1, 128))
                dband_ref[h] += jnp.sum(ds, axis=0)
                dsb = ds.astype(BF16)
                dq = jnp.einsum('nqs,nsd->nqd', dsb, kband, preferred_element_type=F32) * scale
                dqkv_ref[:, hs] = dq.reshape(S, HEAD_DIM).astype(BF16)
                dkb = dkb + jnp.einsum('nqs,nqd->nsd', dsb, qh, preferred_element_type=F32) * scale
                dvb = dvb + jnp.einsum('nqs,nqd->nsd', p.astype(BF16), doh, preferred_element_type=F32)
            dqkv_ref[:, D_WIDTH + HEAD_DIM * g:D_WIDTH + HEAD_DIM * (g + 1)] = unband(dkb)
            dqkv_ref[:, D_WIDTH + D_KV_WIDTH + HEAD_DIM * g:D_WIDTH + D_KV_WIDTH + HEAD_DIM * (g + 1)] = unband(dvb)

    kcol = off['d'] + D_WIDTH
    wq = D_WIDTH + 2 * D_KV_WIDTH
    return pl.pallas_call(
        body, name=name, grid=(Bn,),
        in_specs=[_seq_spec(S, D_WIDTH, off['d']), _seq_spec(S, D_KV_WIDTH, kcol), _seq_spec(S, D_KV_WIDTH, kcol + D_KV_WIDTH),
                  _full((D_Q_HEADS, Q_BLOCK, 2 * Q_BLOCK)), _full((D_Q_HEADS, 128)), _seq_spec(S, D_WIDTH, 0)],
        out_specs=[_seq_spec(S, wq, 0), _full((D_Q_HEADS, Q_BLOCK, 2 * Q_BLOCK)), _full((D_Q_HEADS, 128))],
        out_shape=[jax.ShapeDtypeStruct((Bn, S, wq), BF16), jax.ShapeDtypeStruct((D_Q_HEADS, Q_BLOCK, 2 * Q_BLOCK), F32),
                   jax.ShapeDtypeStruct((D_Q_HEADS, 128), F32)],
        compiler_params=_params("arbitrary"),
    )(proj3, proj3, proj3, band_bias, sinks, dyd3)


def _assemble(pieces, *, name):
    T = pieces[0].shape[0]
    widths = [p.shape[1] for p in pieces]
    tm = _tile(T, 512, 256, 128)

    def body(*refs):
        out_ref = refs[-1]
        col = 0
        for r, w in zip(refs[:-1], widths):
            out_ref[:, col:col + w] = r[...]
            col += w

    return pl.pallas_call(
        body, name=name, grid=(T // tm,),
        in_specs=[pl.BlockSpec((tm, w), lambda i: (i, 0)) for w in widths],
        out_specs=pl.BlockSpec((tm, sum(widths)), lambda i: (i, 0)),
        out_shape=jax.ShapeDtypeStruct((T, sum(widths)), BF16),
        compiler_params=_params("parallel"),
    )(*pieces)


def _relbias_grad(dband, onehot, *, name):
    L, H, n = dband.shape
    R = onehot.shape[0]

    def body(d_ref, oh_ref, out_ref):
        tot = d_ref[0]
        for l in range(1, L):
            tot = tot + d_ref[l]
        out_ref[...] = lax.dot_general(oh_ref[...], tot, (((1,), (1,)), ((), ())), preferred_element_type=F32,
                                       precision=lax.Precision.HIGHEST)

    return pl.pallas_call(
        body, name=name, out_shape=jax.ShapeDtypeStruct((R, H), F32),
        compiler_params=pltpu.CompilerParams(vmem_limit_bytes=VMEM_LIMIT_V7X),
    )(dband, onehot)


def _my_id():
    return lax.axis_index("x") * 4 + lax.axis_index("y") * 2 + lax.axis_index("c")


def _peer(k):
    coords = []
    for bit, axis in zip((4, 2, 1), MESH_AXES):
        me = lax.axis_index(axis)
        coords.append(1 - me if k & bit else me)
    return tuple(coords), coords[0] * 4 + coords[1] * 2 + coords[2]


def _window(ref, col, d, size):
    start = pl.multiple_of(d * size, 8)
    return ref.at[:, :, pl.ds(start, size)] if col else ref.at[:, pl.ds(start, size), :]


ANY = pl.BlockSpec(memory_space=pl.ANY)


def _all_gather(shards, cols, *, name):
    n = len(shards)
    sizes = [s.shape[2] if c else s.shape[1] for s, c in zip(shards, cols)]

    def body(*refs):
        ins, outs = refs[:n], refs[n:2 * n]
        send, recv, loc = refs[2 * n:]
        me = _my_id()
        local = []
        for w in range(n):
            cp = pltpu.make_async_copy(ins[w], _window(outs[w], cols[w], me, sizes[w]), loc.at[w])
            cp.start()
            local.append(cp)
        sends = []
        for k in range(1, N_DEV):
            to, _ = _peer(k)
            for w in range(n):
                cp = pltpu.make_async_remote_copy(
                    src_ref=ins[w], dst_ref=_window(outs[w], cols[w], me, sizes[w]),
                    send_sem=send.at[w, k - 1], recv_sem=recv.at[w, k - 1], device_id=to, device_id_type=MESH)
                cp.start()
                sends.append(cp)
        for k in range(1, N_DEV):
            frm, frm_id = _peer(k)
            for w in range(n):
                pltpu.make_async_remote_copy(
                    src_ref=ins[w], dst_ref=_window(outs[w], cols[w], frm_id, sizes[w]),
                    send_sem=send.at[w, k - 1], recv_sem=recv.at[w, k - 1], device_id=frm, device_id_type=MESH).wait_recv()
        for cp in sends:
            cp.wait_send()
        for cp in local:
            cp.wait()

    out_shape = [jax.ShapeDtypeStruct((s.shape[0], s.shape[1], s.shape[2] * N_DEV) if c
                                      else (s.shape[0], s.shape[1] * N_DEV, s.shape[2]), s.dtype)
                 for s, c in zip(shards, cols)]
    return pl.pallas_call(
        body, name=name, in_specs=[ANY] * n, out_specs=[ANY] * n, out_shape=out_shape,
        scratch_shapes=[pltpu.SemaphoreType.DMA((n, N_DEV - 1)), pltpu.SemaphoreType.DMA((n, N_DEV - 1)),
                        pltpu.SemaphoreType.DMA((n,))],
    )(*shards)


def _reduce_scatter(grads, cols, *, name):
    n = len(grads)
    sizes = [g.shape[2] // N_DEV if c else g.shape[1] // N_DEV for g, c in zip(grads, cols)]

    def body(*refs):
        ins, outs = refs[:n], refs[n:2 * n]
        send, recv, loc = refs[2 * n:]
        me = _my_id()
        local = []
        for w in range(n):
            cp = pltpu.make_async_copy(_window(ins[w], cols[w], me, sizes[w]), outs[w].at[me], loc.at[w])
            cp.start()
            local.append(cp)
        sends = []
        for k in range(1, N_DEV):
            to, to_id = _peer(k)
            for w in range(n):
                cp = pltpu.make_async_remote_copy(
                    src_ref=_window(ins[w], cols[w], to_id, sizes[w]), dst_ref=outs[w].at[me],
                    send_sem=send.at[w, k - 1], recv_sem=recv.at[w, k - 1], device_id=to, device_id_type=MESH)
                cp.start()
                sends.append(cp)
        for k in range(1, N_DEV):
            frm, frm_id = _peer(k)
            for w in range(n):
                pltpu.make_async_remote_copy(
                    src_ref=_window(ins[w], cols[w], me, sizes[w]), dst_ref=outs[w].at[frm_id],
                    send_sem=send.at[w, k - 1], recv_sem=recv.at[w, k - 1], device_id=frm, device_id_type=MESH).wait_recv()
        for cp in sends:
            cp.wait_send()
        for cp in local:
            cp.wait()

    out_shape = [jax.ShapeDtypeStruct((N_DEV, g.shape[0], g.shape[1], sz) if c else (N_DEV, g.shape[0], sz, g.shape[2]), g.dtype)
                 for g, c, sz in zip(grads, cols, sizes)]
    return pl.pallas_call(
        body, name=name, in_specs=[ANY] * n, out_specs=[ANY] * n, out_shape=out_shape,
        scratch_shapes=[pltpu.SemaphoreType.DMA((n, N_DEV - 1)), pltpu.SemaphoreType.DMA((n, N_DEV - 1)),
                        pltpu.SemaphoreType.DMA((n,))],
    )(*grads)


def _all_reduce(buf, *, name):
    R, C = buf.shape

    def body(x_ref, o_ref, land, send, recv):
        me = _my_id()
        land[pl.ds(me, 1)] = x_ref[...][None]
        sends = []
        for k in range(1, N_DEV):
            to, _ = _peer(k)
            cp = pltpu.make_async_remote_copy(src_ref=x_ref, dst_ref=land.at[me], send_sem=send.at[k - 1],
                                              recv_sem=recv.at[k - 1], device_id=to, device_id_type=MESH)
            cp.start()
            sends.append(cp)
        for k in range(1, N_DEV):
            frm, frm_id = _peer(k)
            pltpu.make_async_remote_copy(src_ref=x_ref, dst_ref=land.at[frm_id], send_sem=send.at[k - 1],
                                         recv_sem=recv.at[k - 1], device_id=frm, device_id_type=MESH).wait_recv()
        for cp in sends:
            cp.wait_send()
        acc = land[0]
        for d in range(1, N_DEV):
            acc = acc + land[d]
        o_ref[...] = acc

    vmem = pl.BlockSpec(memory_space=pltpu.VMEM)
    return pl.pallas_call(
        body, name=name, in_specs=[vmem], out_specs=vmem, out_shape=jax.ShapeDtypeStruct((R, C), F32),
        scratch_shapes=[pltpu.VMEM((N_DEV, R, C), F32), pltpu.SemaphoreType.DMA((N_DEV - 1,)),
                        pltpu.SemaphoreType.DMA((N_DEV - 1,))],
        compiler_params=pltpu.CompilerParams(vmem_limit_bytes=VMEM_LIMIT_V7X),
    )(buf)


def _row_tile(rows, row_bytes, align):
    best = None
    for t in range(align, rows + 1, align):
        if rows % t == 0 and t * row_bytes <= 2**20:
            best = t
    return best or rows


def _sum8(recv, *, name):
    _, rows, C = recv.shape
    tr = _row_tile(rows, C * 4, 16)

    def body(r_ref, o_ref):
        acc = r_ref[0].astype(F32)
        for d in range(1, N_DEV):
            acc = acc + r_ref[d].astype(F32)
        o_ref[...] = acc

    return pl.pallas_call(
        body, name=name, grid=(rows // tr,),
        in_specs=[pl.BlockSpec((N_DEV, tr, C), lambda i: (0, i, 0))], out_specs=pl.BlockSpec((tr, C), lambda i: (i, 0)),
        out_shape=jax.ShapeDtypeStruct((rows, C), F32), compiler_params=_params("parallel"),
    )(recv)


def _adamw(g, w, m, v, *, name):
    rows, C = g.shape
    tr = _row_tile(rows, C * 4, 8)

    def body(g_ref, w_ref, m_ref, v_ref, d_ref, mo_ref, vo_ref):
        gt = g_ref[...]
        mn = ADAM_B1 * m_ref[...] + (1.0 - ADAM_B1) * gt
        vn = ADAM_B2 * v_ref[...] + (1.0 - ADAM_B2) * jnp.square(gt)
        m_hat = mn / (1.0 - ADAM_B1 ** ADAM_STEP)
        v_hat = vn / (1.0 - ADAM_B2 ** ADAM_STEP)
        d_ref[...] = -ADAM_LR * (m_hat / (jnp.sqrt(v_hat) + ADAM_EPS) + ADAM_WD * w_ref[...])
        mo_ref[...] = mn
        vo_ref[...] = vn

    spec = pl.BlockSpec((tr, C), lambda i: (i, 0))
    return pl.pallas_call(
        body, name=name, grid=(rows // tr,), in_specs=[spec] * 4, out_specs=[spec] * 3,
        out_shape=[jax.ShapeDtypeStruct((rows, C), F32)] * 3, compiler_params=_params("parallel"),
    )(g, w, m, v)


def _in_splits():
    a_f = 3 * A_WIDTH
    b = a_f + A_HEADS
    c = b + 3 * B_WIDTH
    d = c + 2 * C_WIDTH
    gates = d + D_WIDTH + 2 * D_KV_WIDTH
    return a_f, b, c, d, gates


def _permute_in(w):
    a_f, b, c, d, gates = _in_splits()
    pad = jnp.zeros(w.shape[:-1] + (AF_PAD - A_HEADS,), w.dtype)
    return jnp.concatenate([w[..., gates:], w[..., :a_f], w[..., b:c], w[..., c:d], w[..., d:gates], w[..., a_f:b], pad], axis=-1)


def _unpermute_in(g, D):
    off = _layout(D)
    return jnp.concatenate([g[..., off['a']:off['b']], g[..., off['af']:off['af'] + A_HEADS], g[..., off['b']:off['c']],
                            g[..., off['c']:off['d']], g[..., off['d']:off['af']], g[..., :off['a']]], axis=-1)


def _t5_causal_bucket(dist):
    max_exact = REL_BUCKETS // 2
    large = max_exact + (jnp.log(jnp.maximum(dist, 1).astype(F32) / max_exact)
                         / math.log(REL_MAX_DIST / max_exact) * (REL_BUCKETS - max_exact)).astype(jnp.int32)
    large = jnp.minimum(large, REL_BUCKETS - 1)
    return jnp.where(dist < max_exact, dist, large)


def _pack(parts):
    flat = jnp.concatenate([q.reshape(-1).astype(F32) for q in parts])
    return jnp.pad(flat, (0, (-flat.shape[0]) % 1024)).reshape(-1, 128)


def _unpack(buf, shapes):
    flat, out, pos = buf.reshape(-1), [], 0
    for s in shapes:
        n = math.prod(s)
        out.append(flat[pos:pos + n].reshape(s))
        pos += n
    return out


def kernel(x, p, ffn1_norm_pre, ffn1_w_gu, ffn1_w_down, ffn1_norm_post, mix_norm_pre, w_in, b_forget, b_gate, conv_short, conv_dw, conv_dw_bias, conv_ln_gain, conv_ln_bias, attn_sinks, rel_bias, w_br_a, w_br_b, w_br_c, w_br_d, w_o, mix_norm_post, ffn2_norm_pre, ffn2_w_gu, ffn2_w_down, ffn2_norm_post, ple_norm_gate, w_ple_gate, w_ple, ple_norm_post, loss_target, m_ffn1_norm_pre, m_ffn1_w_gu, m_ffn1_w_down, m_ffn1_norm_post, m_mix_norm_pre, m_w_in, m_b_forget, m_b_gate, m_conv_short, m_conv_dw, m_conv_dw_bias, m_conv_ln_gain, m_conv_ln_bias, m_attn_sinks, m_rel_bias, m_w_br_a, m_w_br_b, m_w_br_c, m_w_br_d, m_w_o, m_mix_norm_post, m_ffn2_norm_pre, m_ffn2_w_gu, m_ffn2_w_down, m_ffn2_norm_post, m_ple_norm_gate, m_w_ple_gate, m_w_ple, m_ple_norm_post, v_ffn1_norm_pre, v_ffn1_w_gu, v_ffn1_w_down, v_ffn1_norm_post, v_mix_norm_pre, v_w_in, v_b_forget, v_b_gate, v_conv_short, v_conv_dw, v_conv_dw_bias, v_conv_ln_gain, v_conv_ln_bias, v_attn_sinks, v_rel_bias, v_w_br_a, v_w_br_b, v_w_br_c, v_w_br_d, v_w_o, v_mix_norm_post, v_ffn2_norm_pre, v_ffn2_w_gu, v_ffn2_w_down, v_ffn2_norm_post, v_ple_norm_gate, v_w_ple_gate, v_w_ple, v_ple_norm_post):
    a = dict(locals())
    Bn, S, D = x.shape
    T = Bn * S
    L, E = p.shape[0], p.shape[-1]
    F = ffn1_w_down.shape[1] * N_DEV
    off = _layout(D)
    PW = off['end']
    me = _my_id()

    shard = {n: a[n].astype(BF16) for n in BIG}
    shard['ffn1_w_gu'] = jnp.swapaxes(ffn1_w_gu, 1, 2).astype(BF16)
    shard['ffn2_w_gu'] = jnp.swapaxes(ffn2_w_gu, 1, 2).astype(BF16)
    shard['w_in'] = _permute_in(w_in).astype(BF16)
    cols = [n in COL_SHARDED for n in BIG]
    full = dict(zip(BIG, _all_gather([shard[n] for n in BIG], cols, name="gather_weights")))
    wgu = {1: full['ffn1_w_gu'].reshape(L, 2, F, D), 2: full['ffn2_w_gu'].reshape(L, 2, F, D)}
    wdn = {1: full['ffn1_w_down'], 2: full['ffn2_w_down']}
    win = full['w_in']

    cw = conv_short.shape[2]
    conv_full = [lax.dynamic_update_slice(jnp.zeros(c.shape[:2] + (cw * N_DEV,), F32), c, (0, 0, me * cw))
                 for c in (conv_short, conv_dw)]
    conv_shapes = [c.shape for c in conv_full]
    cs_all, cdw_all = _unpack(_all_reduce(_pack(conv_full), name="gather_conv"), conv_shapes)

    band_dist = jnp.maximum(jnp.arange(Q_BLOCK)[:, None] + Q_BLOCK - jnp.arange(2 * Q_BLOCK)[None, :], 0)
    bucket = _t5_causal_bucket(band_dist)
    band_bias = jnp.transpose(rel_bias[bucket], (2, 0, 1)).astype(F32)
    onehot = (bucket.reshape(1, -1) == jnp.arange(REL_BUCKETS)[:, None]).astype(F32)

    def vec(name, i):
        return a[name][i][None]

    def lay(i):
        return dict(
            bfor=jnp.pad(b_forget[i], (0, AF_LANES - A_HEADS))[None], bgate=b_gate[i].reshape(N_BRANCH, D),
            cs=cs_all[i], cdw=cdw_all[i], cb=conv_dw_bias[i][None], lg=conv_ln_gain[i][None], lb=conv_ln_bias[i][None],
            sinks=jnp.broadcast_to(attn_sinks[i][:, None], (D_Q_HEADS, 128)),
            wbrs=[full[n][i] for n in ('w_br_a', 'w_br_b', 'w_br_c', 'w_br_d')], pe=p[i].reshape(T, E))

    h = x.reshape(T, D)
    saved = []
    for i in range(L):
        s, q = dict(h0=h), lay(i)
        s['gu1'], s['n1'] = _rms_mm(h, vec('ffn1_norm_pre', i), wgu[1][i], nt=True, out_dtype=BF16, save_n=True,
                                    name=f"ffn1_up_{i}")
        s['h1'], s['f1'] = _ffn_down(s['gu1'], wdn[1][i], h, vec('ffn1_norm_post', i), name=f"ffn1_down_{i}")
        proj, s['u'] = _rms_mm(s['h1'], vec('mix_norm_pre', i), win[i][None], nt=False, out_dtype=BF16, save_n=True,
                               name=f"proj_{i}")
        af = _rms_mm(s['h1'], vec('mix_norm_pre', i), win[i][None, :, off['af']:off['af'] + AF_LANES], nt=False,
                     out_dtype=F32, save_n=False, name=f"proj_forget_{i}")
        s['proj'] = proj.reshape(T, PW)
        s['proj3'], s['af3'] = proj.reshape(Bn, S, PW), af.reshape(Bn, S, AF_LANES)
        ya = _fox_fwd(s['proj3'], s['af3'], q['bfor'], off, name=f"fox_{i}")
        yb = _sconv_fwd(s['proj3'], q['cs'], off, name=f"sconv_{i}")
        yc = _cconv_fwd(s['proj3'], q['cdw'], q['cb'], q['lg'], q['lb'], off, name=f"cconv_{i}")
        yd = _swa_fwd(s['proj3'], band_bias, q['sinks'], off, name=f"swa_{i}")
        s['ys'] = [y.reshape(T, y.shape[-1]) for y in (ya, yb, yc, yd)]
        s['h2'], s['o'], s['merged'] = _merge_out(s['ys'], s['proj'], q['bgate'], q['wbrs'], full['w_o'][i], s['h1'],
                                                  vec('mix_norm_post', i), name=f"merge_{i}")
        s['gu2'], s['n2'] = _rms_mm(s['h2'], vec('ffn2_norm_pre', i), wgu[2][i], nt=True, out_dtype=BF16, save_n=True,
                                    name=f"ffn2_up_{i}")
        s['h3'], s['f2'] = _ffn_down(s['gu2'], wdn[2][i], s['h2'], vec('ffn2_norm_post', i), name=f"ffn2_down_{i}")
        h = _ple(s['h3'], q['pe'], vec('ple_norm_gate', i), full['w_ple_gate'][i], full['w_ple'][i],
                 vec('ple_norm_post', i), name=f"ple_{i}")
        saved.append(s)

    lpart, dh = _loss_head(h, loss_target.reshape(T, D), name="loss_head")
    loss = lax.psum(lpart[0, 0], MESH_AXES)

    gbuf = {n: None for n in BIG}
    sg = {n: [None] * L for n in WEIGHTS if n not in BIG and n != 'rel_bias'}
    dbands = [None] * L

    def wgrad(n, i, a_, b_, **kw):
        gbuf[n] = _mm_tn(a_, b_, gbuf[n], i, L, name=f"d_{n}_{i}", **kw)

    def ffn_bwd(k, i, dh_out, s, h_in):
        dgu, df, sg[f'ffn{k}_norm_post'][i] = _ffn_down_bwd(dh_out, s[f'f{k}'], vec(f'ffn{k}_norm_post', i), wdn[k][i],
                                                             s[f'gu{k}'], name=f"ffn{k}_down_bwd_{i}")
        wgrad(f'ffn{k}_w_down', i, s[f'gu{k}'], df, swiglu=True)
        wgrad(f'ffn{k}_w_gu', i, dgu, s[f'n{k}'])
        dh_in, sg[f'ffn{k}_norm_pre'][i] = _mm_rmsbwd(dgu, wgu[k][i], dh_out, h_in, vec(f'ffn{k}_norm_pre', i), nt=False,
                                                      name=f"ffn{k}_up_bwd_{i}")
        return dh_in

    for i in reversed(range(L)):
        s, q = saved[i], lay(i)
        dh, de, dpgl, npg, sg['ple_norm_gate'][i], sg['ple_norm_post'][i] = _ple_bwd(
            dh, s['h3'], q['pe'], vec('ple_norm_gate', i), full['w_ple_gate'][i], full['w_ple'][i], vec('ple_norm_post', i),
            name=f"ple_bwd_{i}")
        wgrad('w_ple', i, q['pe'][None], de)
        wgrad('w_ple_gate', i, npg[None], dpgl)
        dh = ffn_bwd(2, i, dh, s, s['h2'])
        do, dz, *dys, dgates, sg['mix_norm_post'][i], dbg = _merge_out_bwd(
            dh, s['o'], vec('mix_norm_post', i), full['w_o'][i], s['ys'], s['proj'], q['bgate'], q['wbrs'],
            name=f"merge_bwd_{i}")
        sg['b_gate'][i] = dbg.reshape(-1)
        wgrad('w_o', i, s['merged'][None], do)
        for b, n in enumerate(('w_br_a', 'w_br_b', 'w_br_c', 'w_br_d')):
            wgrad(n, i, s['ys'][b][None], dz, b_plane=b)
        dy3 = [d.reshape(Bn, S, d.shape[-1]) for d in dys]
        da, daf, dbf = _fox_bwd(s['proj3'], s['af3'], q['bfor'], dy3[0], off, name=f"fox_bwd_{i}")
        sg['b_forget'][i] = dbf[0, :A_HEADS]
        db, sg['conv_short'][i] = _sconv_bwd(s['proj3'], q['cs'], dy3[1], off, name=f"sconv_bwd_{i}")
        dc, sg['conv_dw'][i], sg['conv_dw_bias'][i], sg['conv_ln_gain'][i], sg['conv_ln_bias'][i] = _cconv_bwd(
            s['proj3'], q['cdw'], q['cb'], q['lg'], q['lb'], dy3[2], off, name=f"cconv_bwd_{i}")
        dd, dbands[i], dsk = _swa_bwd(s['proj3'], band_bias, q['sinks'], dy3[3], off, name=f"swa_bwd_{i}")
        sg['attn_sinks'][i] = dsk[:, 0]
        dproj = _assemble([dgates] + [t.reshape(T, t.shape[-1]) for t in (da, db, dc, dd, daf)], name=f"dproj_{i}")
        wgrad('w_in', i, s['u'][None], dproj)
        dh, sg['mix_norm_pre'][i] = _mm_rmsbwd(dproj[None], win[i][None], dh, s['h1'], vec('mix_norm_pre', i), nt=True,
                                               name=f"proj_bwd_{i}")
        dh = ffn_bwd(1, i, dh, s, s['h0'])
    grad_x = dh.reshape(Bn, S, D)

    d_rel = _relbias_grad(jnp.stack(dbands).reshape(L, D_Q_HEADS, -1), onehot, name="relbias_grad")
    small_g = [d_rel if n == 'rel_bias' else jnp.stack(sg[n]).reshape(a[n].shape) for n in SMALL]
    n_small_rows = _pack(small_g).shape[0]
    conv_g = [jnp.stack(sg[n]) for n in CONV_SHARDED]
    red = _all_reduce(jnp.concatenate([_pack(small_g), _pack(conv_g)]), name="reduce_small")
    g_small_buf = red[:n_small_rows]
    conv_gfull = _unpack(red[n_small_rows:], conv_shapes)
    conv_gloc = [lax.dynamic_slice_in_dim(g, me * cw, cw, axis=2) for g in conv_gfull]

    grads, deltas, new_m, new_v = {}, {}, {}, {}
    small_shapes = [a[n].shape for n in SMALL]
    res = _adamw(g_small_buf, *[_pack([a[pre + n] for n in SMALL]) for pre in ('', 'm_', 'v_')], name="adamw_small")
    for dst, buf in zip((grads, deltas, new_m, new_v), (g_small_buf,) + tuple(res)):
        dst.update(zip(SMALL, _unpack(buf, small_shapes)))
    loc_shapes = [a[n].shape for n in CONV_SHARDED]
    g_conv_buf = _pack(conv_gloc)
    res = _adamw(g_conv_buf, *[_pack([a[pre + n] for n in CONV_SHARDED]) for pre in ('', 'm_', 'v_')], name="adamw_conv")
    for dst, buf in zip((grads, deltas, new_m, new_v), (g_conv_buf,) + tuple(res)):
        dst.update(zip(CONV_SHARDED, _unpack(buf, loc_shapes)))

    recv = _reduce_scatter([gbuf[n] for n in BIG], cols, name="scatter_grads")
    for n, r in zip(BIG, recv):
        g = _sum8(r.reshape(N_DEV, -1, r.shape[-1]), name=f"sum_{n}").reshape(r.shape[1:])
        if n in ('ffn1_w_gu', 'ffn2_w_gu'):
            g = jnp.swapaxes(g, 1, 2)
        elif n == 'w_in':
            g = _unpermute_in(g, D)
        C = g.shape[-1]
        res = _adamw(g.reshape(-1, C), *[a[pre + n].reshape(-1, C) for pre in ('', 'm_', 'v_')], name=f"adamw_{n}")
        grads[n] = g
        deltas[n], new_m[n], new_v[n] = [t.reshape(g.shape) for t in res]

    return (loss, grad_x, *[grads[n] for n in WEIGHTS], *[deltas[n] for n in WEIGHTS],
            *[new_m[n] for n in WEIGHTS], *[new_v[n] for n in WEIGHTS])
```

```python
import functools
import math

import jax
import jax.numpy as jnp
import numpy as np
from jax import lax
from jax.experimental import pallas as pl
from jax.experimental.pallas import tpu as pltpu

F32 = jnp.float32
BF16 = jnp.bfloat16

EPS = 1e-6
NEG_INF = -1e30
HEAD_DIM = 64
A_HEADS = 4
A_WIDTH = A_HEADS * HEAD_DIM
B_WIDTH = 256
C_WIDTH = 256
SHORT_CONV = 3
CONF_CONV = 31
D_Q_HEADS = 8
D_KV_HEADS = 2
D_GROUP = D_Q_HEADS // D_KV_HEADS
D_WIDTH = D_Q_HEADS * HEAD_DIM
D_KV_WIDTH = D_KV_HEADS * HEAD_DIM
WINDOW = 128
Q_BLOCK = 128
N_BRANCH = 4
REL_BUCKETS = 32
REL_MAX_DIST = 128
AF_PAD = 256
AF_LANES = 128

ADAM_LR = 0.001
ADAM_B1 = 0.9
ADAM_B2 = 0.999
ADAM_EPS = 1e-08
ADAM_WD = 0.01
ADAM_STEP = 10

N_DEV = 8
MESH_AXES = ("x", "y", "c")
VMEM_LIMIT_V7X = 56 * 2**20
MESH = pl.DeviceIdType.MESH

WEIGHTS = ['ffn1_norm_pre', 'ffn1_w_gu', 'ffn1_w_down', 'ffn1_norm_post', 'mix_norm_pre', 'w_in', 'b_forget',
           'b_gate', 'conv_short', 'conv_dw', 'conv_dw_bias', 'conv_ln_gain', 'conv_ln_bias', 'attn_sinks',
           'rel_bias', 'w_br_a', 'w_br_b', 'w_br_c', 'w_br_d', 'w_o', 'mix_norm_post', 'ffn2_norm_pre',
           'ffn2_w_gu', 'ffn2_w_down', 'ffn2_norm_post', 'ple_norm_gate', 'w_ple_gate', 'w_ple', 'ple_norm_post']
ARG_NAMES = ['x', 'p'] + WEIGHTS + ['loss_target'] + ['m_' + n for n in WEIGHTS] + ['v_' + n for n in WEIGHTS]
BIG = ['ffn1_w_gu', 'ffn1_w_down', 'w_in', 'w_br_a', 'w_br_b', 'w_br_c', 'w_br_d', 'w_o', 'ffn2_w_gu',
       'ffn2_w_down', 'w_ple_gate', 'w_ple']
COL_SHARDED = ('w_br_a', 'w_br_b', 'w_br_c', 'w_br_d', 'w_ple')
CONV_SHARDED = ('conv_short', 'conv_dw')
SMALL = [n for n in WEIGHTS if n not in BIG and n not in CONV_SHARDED]


def _tile(n, *prefs):
    for t in prefs:
        if n % t == 0:
            return t
    return n


def _params(*sem):
    return pltpu.CompilerParams(dimension_semantics=sem, vmem_limit_bytes=VMEM_LIMIT_V7X)


def _dot(a, b):
    return jnp.dot(a, b, preferred_element_type=F32)


def _dot_nt(a, b):
    return lax.dot_general(a, b, (((1,), (1,)), ((), ())), preferred_element_type=F32)


def _dot_tn(a, b):
    return lax.dot_general(a, b, (((0,), (0,)), ((), ())), preferred_element_type=F32)


def _rstd(x):
    return lax.rsqrt(jnp.mean(x * x, axis=-1, keepdims=True) + EPS)


def _rms_bwd(dy, x, r, g):
    xh = x * r
    dxh = dy * g
    dx = r * (dxh - xh * jnp.mean(dxh * xh, axis=-1, keepdims=True))
    return dx, dy * xh


def _colsum(v):
    return jnp.sum(v, axis=0, keepdims=True)


def _my_id():
    return lax.axis_index("x") * 4 + lax.axis_index("y") * 2 + lax.axis_index("c")


def _peer(k):
    coords = []
    for bit, axis in zip((4, 2, 1), MESH_AXES):
        me = lax.axis_index(axis)
        coords.append(1 - me if k & bit else me)
    return tuple(coords), coords[0] * 4 + coords[1] * 2 + coords[2]


def _window(ref, col, d, size):
    start = pl.multiple_of(d * size, 8)
    return ref.at[:, pl.ds(start, size)] if col else ref.at[pl.ds(start, size), :]


ANY = pl.BlockSpec(memory_space=pl.ANY)


class _Exchange:
    def __init__(self, kind, arrays, cols):
        self.kind, self.arrays, self.cols = kind, list(arrays), list(cols)
        n = len(self.arrays)
        if kind == "gather":
            self.sizes = [a.shape[1] if c else a.shape[0] for a, c in zip(self.arrays, cols)]
            self.out_shape = [jax.ShapeDtypeStruct((a.shape[0], a.shape[1] * N_DEV) if c else (a.shape[0] * N_DEV, a.shape[1]),
                                                   a.dtype) for a, c in zip(self.arrays, cols)]
        else:
            self.sizes = [a.shape[1] // N_DEV if c else a.shape[0] // N_DEV for a, c in zip(self.arrays, cols)]
            self.out_shape = [jax.ShapeDtypeStruct((N_DEV, a.shape[0], s) if c else (N_DEV, s, a.shape[1]), a.dtype)
                              for a, c, s in zip(self.arrays, cols, self.sizes)]
        self.scratch = [pltpu.SemaphoreType.DMA((n, N_DEV - 1)), pltpu.SemaphoreType.DMA((n, N_DEV - 1)),
                        pltpu.SemaphoreType.DMA((n,))]
        self.result = None

    def _src(self, ins, w, d):
        return ins[w] if self.kind == "gather" else _window(ins[w], self.cols[w], d, self.sizes[w])

    def _dst(self, outs, w, d):
        return _window(outs[w], self.cols[w], d, self.sizes[w]) if self.kind == "gather" else outs[w].at[d]

    def _copies(self, ins, outs, send, recv, loc):
        me = _my_id()
        n = len(self.arrays)
        local = [pltpu.make_async_copy(self._src(ins, w, me), self._dst(outs, w, me), loc.at[w]) for w in range(n)]
        sends, arrivals = [], []
        for k in range(1, N_DEV):
            peer, peer_id = _peer(k)
            for w in range(n):
                sems = dict(send_sem=send.at[w, k - 1], recv_sem=recv.at[w, k - 1], device_id=peer, device_id_type=MESH)
                sends.append(pltpu.make_async_remote_copy(src_ref=self._src(ins, w, peer_id), dst_ref=self._dst(outs, w, me), **sems))
                arrivals.append(pltpu.make_async_remote_copy(src_ref=self._src(ins, w, me), dst_ref=self._dst(outs, w, peer_id), **sems))
        return local, sends, arrivals

    def start(self, ins, outs, send, recv, loc):
        local, sends, _ = self._copies(ins, outs, send, recv, loc)
        for cp in local + sends:
            cp.start()

    def wait(self, ins, outs, send, recv, loc):
        local, sends, arrivals = self._copies(ins, outs, send, recv, loc)
        for cp in arrivals:
            cp.wait_recv()
        for cp in sends:
            cp.wait_send()
        for cp in local:
            cp.wait()

    def run_alone(self, name):
        n = len(self.arrays)

        def body(*refs):
            self.start(refs[:n], refs[n:2 * n], *refs[2 * n:])
            self.wait(refs[:n], refs[n:2 * n], *refs[2 * n:])

        self.result = pl.pallas_call(body, name=name, in_specs=[ANY] * n, out_specs=[ANY] * n, out_shape=self.out_shape,
                                     scratch_shapes=self.scratch)(*self.arrays)
        return self.result


def _call(body, args, *, name, grid, in_specs, out_specs, out_shape, scratch_shapes=(), sem, rider=None):
    if rider is None:
        return pl.pallas_call(body, name=name, grid=grid, in_specs=in_specs, out_specs=out_specs, out_shape=out_shape,
                              scratch_shapes=list(scratch_shapes), compiler_params=_params(*sem))(*args)
    n_in, n_out, n_scr, n_r = len(in_specs), len(out_shape), len(scratch_shapes), len(rider.arrays)

    def both(*refs):
        ins, r_in = refs[:n_in], refs[n_in:n_in + n_r]
        outs, r_out = refs[n_in + n_r:n_in + n_r + n_out], refs[n_in + n_r + n_out:n_in + 2 * n_r + n_out]
        scr, sems = refs[n_in + 2 * n_r + n_out:n_in + 2 * n_r + n_out + n_scr], refs[n_in + 2 * n_r + n_out + n_scr:]
        first = functools.reduce(lambda p, q: p & q, [pl.program_id(d) == 0 for d in range(len(grid))])
        last = functools.reduce(lambda p, q: p & q, [pl.program_id(d) == g - 1 for d, g in enumerate(grid)])

        @pl.when(first)
        def _():
            rider.start(r_in, r_out, *sems)

        body(*ins, *outs, *scr)

        @pl.when(last)
        def _():
            rider.wait(r_in, r_out, *sems)

    res = pl.pallas_call(
        both, name=name, grid=grid, in_specs=list(in_specs) + [ANY] * n_r, out_specs=list(out_specs) + [ANY] * n_r,
        out_shape=list(out_shape) + rider.out_shape, scratch_shapes=list(scratch_shapes) + rider.scratch,
        compiler_params=_params(*(("arbitrary",) * len(grid))))(*args, *rider.arrays)
    rider.result = res[n_out:]
    return res[:n_out]


def _rms_mm(h, g, w, *, nt, out_dtype, save_n, name, rider=None):
    T, D = h.shape
    P = w.shape[0]
    N = w.shape[1] if nt else w.shape[2]
    tm = _tile(T, 512, 256, 128)
    tn = _tile(N, 2816, 1792, 2048, 1408, 1024, 512, 256, 128)

    def body(h_ref, g_ref, w_ref, y_ref, *rest):
        n_scr = rest[-1]

        @pl.when((pl.program_id(1) == 0) & (pl.program_id(2) == 0))
        def _():
            x = h_ref[...]
            n = (x * _rstd(x) * g_ref[...]).astype(BF16)
            n_scr[...] = n
            if save_n:
                rest[0][...] = n

        wt = w_ref[...]
        y = _dot_nt(n_scr[...], wt) if nt else _dot(n_scr[...], wt)
        y_ref[...] = y.astype(out_dtype)

    w_spec = (pl.BlockSpec((None, tn, D), lambda i, p, j: (p, j, 0)) if nt
              else pl.BlockSpec((None, D, tn), lambda i, p, j: (p, 0, j)))
    out_shape = [jax.ShapeDtypeStruct((P, T, N), out_dtype)]
    out_specs = [pl.BlockSpec((None, tm, tn), lambda i, p, j: (p, i, j))]
    if save_n:
        out_shape.append(jax.ShapeDtypeStruct((T, D), BF16))
        out_specs.append(pl.BlockSpec((tm, D), lambda i, p, j: (i, 0)))
    res = _call(
        body, (h, g, w), name=name, grid=(T // tm, P, N // tn),
        in_specs=[pl.BlockSpec((tm, D), lambda i, p, j: (i, 0)), pl.BlockSpec((1, D), lambda i, p, j: (0, 0)), w_spec],
        out_specs=out_specs, out_shape=out_shape, scratch_shapes=[pltpu.VMEM((tm, D), BF16)],
        sem=("parallel", "arbitrary", "arbitrary"), rider=rider)
    return res if save_n else res[0]


def _ffn_down(gu, wd, h, gpost, *, name, rider=None):
    _, T, F = gu.shape
    D = wd.shape[1]
    tm = _tile(T, 512, 256, 128)
    tk = _tile(F, 2816, 1408, 1024, 512, 256, 128)
    nk = F // tk

    def body(g_ref, u_ref, wd_ref, h_ref, gp_ref, hn_ref, f_ref, acc):
        k = pl.program_id(1)

        @pl.when(k == 0)
        def _():
            acc[...] = jnp.zeros_like(acc)

        gt = g_ref[...].astype(F32)
        a = (gt * jax.nn.sigmoid(gt) * u_ref[...].astype(F32)).astype(BF16)
        acc[...] += _dot(a, wd_ref[...])

        @pl.when(k == nk - 1)
        def _():
            f = acc[...]
            f_ref[...] = f
            hn_ref[...] = h_ref[...] + 0.5 * (f * _rstd(f) * gp_ref[...])

    return _call(
        body, (gu, gu, wd, h, gpost), name=name, grid=(T // tm, nk),
        in_specs=[pl.BlockSpec((None, tm, tk), lambda i, k: (0, i, k)), pl.BlockSpec((None, tm, tk), lambda i, k: (1, i, k)),
                  pl.BlockSpec((tk, D), lambda i, k: (k, 0)), pl.BlockSpec((tm, D), lambda i, k: (i, 0)),
                  pl.BlockSpec((1, D), lambda i, k: (0, 0))],
        out_specs=[pl.BlockSpec((tm, D), lambda i, k: (i, 0)), pl.BlockSpec((tm, D), lambda i, k: (i, 0))],
        out_shape=[jax.ShapeDtypeStruct((T, D), F32), jax.ShapeDtypeStruct((T, D), F32)],
        scratch_shapes=[pltpu.VMEM((tm, D), F32)], sem=("parallel", "arbitrary"), rider=rider)


def _ffn_down_bwd(dh, f, gpost, wd, gu, *, name, rider=None):
    _, T, F = gu.shape
    D = wd.shape[1]
    tm = _tile(T, 256, 128)
    tn = _tile(F, 2816, 1408, 1024, 512, 256, 128)

    def body(dh_ref, f_ref, gp_ref, wd_ref, g_ref, u_ref, dgu_ref, df_ref, dgp_ref, df_scr):
        i, j = pl.program_id(0), pl.program_id(1)

        @pl.when((i == 0) & (j == 0))
        def _():
            dgp_ref[...] = jnp.zeros_like(dgp_ref)

        @pl.when(j == 0)
        def _():
            x = f_ref[...]
            dx, dgn = _rms_bwd(0.5 * dh_ref[...], x, _rstd(x), gp_ref[...])
            dgp_ref[...] += _colsum(dgn)
            df = dx.astype(BF16)
            df_scr[...] = df
            df_ref[...] = df

        dact = _dot_nt(df_scr[...], wd_ref[...])
        gt = g_ref[...].astype(F32)
        ut = u_ref[...].astype(F32)
        sg = jax.nn.sigmoid(gt)
        dgu_ref[0] = (dact * ut * (sg * (1.0 + gt * (1.0 - sg)))).astype(BF16)
        dgu_ref[1] = (dact * (gt * sg)).astype(BF16)

    return _call(
        body, (dh, f, gpost, wd, gu, gu), name=name, grid=(T // tm, F // tn),
        in_specs=[pl.BlockSpec((tm, D), lambda i, j: (i, 0)), pl.BlockSpec((tm, D), lambda i, j: (i, 0)),
                  pl.BlockSpec((1, D), lambda i, j: (0, 0)), pl.BlockSpec((tn, D), lambda i, j: (j, 0)),
                  pl.BlockSpec((None, tm, tn), lambda i, j: (0, i, j)), pl.BlockSpec((None, tm, tn), lambda i, j: (1, i, j))],
        out_specs=[pl.BlockSpec((2, tm, tn), lambda i, j: (0, i, j)), pl.BlockSpec((tm, D), lambda i, j: (i, 0)),
                   pl.BlockSpec((1, D), lambda i, j: (0, 0))],
        out_shape=[jax.ShapeDtypeStruct((2, T, F), BF16), jax.ShapeDtypeStruct((T, D), BF16),
                   jax.ShapeDtypeStruct((1, D), F32)],
        scratch_shapes=[pltpu.VMEM((tm, D), BF16)], sem=("arbitrary", "arbitrary"), rider=rider)


def _mm_rmsbwd(a, b, dh_in, h, g, *, nt, name, rider=None):
    P, T, K = a.shape
    D = h.shape[1]
    tm = _tile(T, 512, 256, 128)
    tk = _tile(K, 1792, 1408, 2048, 1024, 512, 256, 128)
    nk = K // tk

    def body(a_ref, b_ref, dh_ref, h_ref, g_ref, out_ref, dg_ref, acc):
        i, p, k = pl.program_id(0), pl.program_id(1), pl.program_id(2)

        @pl.when((i == 0) & (p == 0) & (k == 0))
        def _():
            dg_ref[...] = jnp.zeros_like(dg_ref)

        @pl.when((p == 0) & (k == 0))
        def _():
            acc[...] = jnp.zeros_like(acc)

        acc[...] += _dot_nt(a_ref[...], b_ref[...]) if nt else _dot(a_ref[...], b_ref[...])

        @pl.when((p == P - 1) & (k == nk - 1))
        def _():
            x = h_ref[...]
            dx, dgn = _rms_bwd(acc[...], x, _rstd(x), g_ref[...])
            dg_ref[...] += _colsum(dgn)
            out_ref[...] = dh_ref[...] + dx

    b_spec = (pl.BlockSpec((None, D, tk), lambda i, p, k: (p, 0, k)) if nt
              else pl.BlockSpec((None, tk, D), lambda i, p, k: (p, k, 0)))
    return _call(
        body, (a, b, dh_in, h, g), name=name, grid=(T // tm, P, nk),
        in_specs=[pl.BlockSpec((None, tm, tk), lambda i, p, k: (p, i, k)), b_spec,
                  pl.BlockSpec((tm, D), lambda i, p, k: (i, 0)), pl.BlockSpec((tm, D), lambda i, p, k: (i, 0)),
                  pl.BlockSpec((1, D), lambda i, p, k: (0, 0))],
        out_specs=[pl.BlockSpec((tm, D), lambda i, p, k: (i, 0)), pl.BlockSpec((1, D), lambda i, p, k: (0, 0))],
        out_shape=[jax.ShapeDtypeStruct((T, D), F32), jax.ShapeDtypeStruct((1, D), F32)],
        scratch_shapes=[pltpu.VMEM((tm, D), F32)], sem=("arbitrary", "arbitrary", "arbitrary"), rider=rider)


def _mm_tn(a, b, *, swiglu=False, b_plane=0, name):
    T, N = b.shape[-2:]
    K = a.shape[2]
    P = 1 if swiglu else a.shape[0]
    tk = _tile(K, 2816, 1024, 512, 256, 128)
    tn = _tile(N, 1792, 1024, 512, 256, 128)
    tt = _tile(T, 512, 256, 128)
    nt_ = T // tt
    nkb = K // tk

    def body(*refs):
        if swiglu:
            g_ref, u_ref, b_ref = refs[:3]
        else:
            a_ref, b_ref = refs[:2]
        out_ref, acc = refs[-2], refs[-1]
        t = pl.program_id(3)

        @pl.when(t == 0)
        def _():
            acc[...] = jnp.zeros_like(acc)

        if swiglu:
            gt = g_ref[...].astype(F32)
            at = (gt * jax.nn.sigmoid(gt) * u_ref[...].astype(F32)).astype(BF16)
        else:
            at = a_ref[...].astype(BF16)
        acc[...] += _dot_tn(at, b_ref[...].astype(BF16))

        @pl.when(t == nt_ - 1)
        def _():
            out_ref[...] = acc[...].astype(BF16)

    if swiglu:
        a_specs = [pl.BlockSpec((None, tt, tk), lambda p, i, j, t: (0, t, i)),
                   pl.BlockSpec((None, tt, tk), lambda p, i, j, t: (1, t, i))]
        a_args = [a, a]
    else:
        a_specs = [pl.BlockSpec((None, tt, tk), lambda p, i, j, t: (p, t, i))]
        a_args = [a]
    if b.ndim == 3:
        in_specs = a_specs + [pl.BlockSpec((None, tt, tn), lambda p, i, j, t: (b_plane, t, j))]
    else:
        in_specs = a_specs + [pl.BlockSpec((tt, tn), lambda p, i, j, t: (t, j))]
    return pl.pallas_call(
        body, name=name, grid=(P, nkb, N // tn, nt_),
        in_specs=in_specs,
        out_specs=pl.BlockSpec((tk, tn), lambda p, i, j, t: (p * nkb + i, j)),
        out_shape=jax.ShapeDtypeStruct((P * K, N), BF16),
        scratch_shapes=[pltpu.VMEM((tk, tn), F32)],
        compiler_params=_params("parallel", "parallel", "parallel", "arbitrary"),
    )(*a_args, b)


def _row(D):
    return pl.BlockSpec((1, D), lambda i: (0, 0))


def _full(shape):
    return pl.BlockSpec(shape, lambda i: (0,) * len(shape))


def _merge_out(ys, proj, bgate, wbrs, wo, h, gpost, *, name):
    T, D = h.shape
    tm = _tile(T, 512, 256, 128)

    def body(ya, yb, yc, yd, g0, g1, g2, g3, bg_ref, wa, wb, wc, wd_, wo_ref, h_ref, gp_ref, hn_ref, o_ref, mg_ref):
        merged = jnp.zeros((tm, D), F32)
        for b, (y_ref, gt_ref, w_ref) in enumerate(zip((ya, yb, yc, yd), (g0, g1, g2, g3), (wa, wb, wc, wd_))):
            gate = jax.nn.sigmoid(gt_ref[...].astype(F32) + bg_ref[b:b + 1, :])
            merged = merged + gate * _dot(y_ref[...], w_ref[...])
        mb = merged.astype(BF16)
        mg_ref[...] = mb
        o = _dot(mb, wo_ref[...])
        o_ref[...] = o
        hn_ref[...] = h_ref[...] + o * _rstd(o) * gp_ref[...]

    tok = lambda w: pl.BlockSpec((tm, w), lambda i: (i, 0))
    gate_specs = [pl.BlockSpec((tm, D), lambda i, b=b: (i, b)) for b in range(N_BRANCH)]
    return pl.pallas_call(
        body, name=name, grid=(T // tm,),
        in_specs=[tok(A_WIDTH), tok(B_WIDTH), tok(C_WIDTH), tok(D_WIDTH)] + gate_specs
        + [_full((N_BRANCH, D))] + [_full(w.shape) for w in wbrs] + [_full((D, D)), tok(D), _row(D)],
        out_specs=[tok(D), tok(D), tok(D)],
        out_shape=[jax.ShapeDtypeStruct((T, D), F32), jax.ShapeDtypeStruct((T, D), F32), jax.ShapeDtypeStruct((T, D), BF16)],
        compiler_params=_params("parallel"),
    )(*ys, proj, proj, proj, proj, bgate, *wbrs, wo, h, gpost)


def _merge_out_bwd(dh, o, gpost, wo, ys, proj, bgate, wbrs, *, name):
    T, D = o.shape
    tm = _tile(T, 256, 128)
    widths = (A_WIDTH, B_WIDTH, C_WIDTH, D_WIDTH)

    def body(dh_ref, o_ref, gp_ref, wo_ref, ya, yb, yc, yd, g0, g1, g2, g3, bg_ref, wa, wb, wc, wd_,
             do_ref, dz_ref, dya, dyb, dyc, dyd, dgt_ref, dgp_ref, dbg_ref):
        @pl.when(pl.program_id(0) == 0)
        def _():
            dgp_ref[...] = jnp.zeros_like(dgp_ref)
            dbg_ref[...] = jnp.zeros_like(dbg_ref)

        x = o_ref[...]
        do, dgn = _rms_bwd(dh_ref[...], x, _rstd(x), gp_ref[...])
        dgp_ref[...] += _colsum(dgn)
        dob = do.astype(BF16)
        do_ref[...] = dob
        dmerged = _dot_nt(dob, wo_ref[...])
        for b, (y_ref, gt_ref, w_ref, dy_ref) in enumerate(zip((ya, yb, yc, yd), (g0, g1, g2, g3), (wa, wb, wc, wd_),
                                                               (dya, dyb, dyc, dyd))):
            gate = jax.nn.sigmoid(gt_ref[...].astype(F32) + bg_ref[b:b + 1, :])
            z = _dot(y_ref[...], w_ref[...])
            dz = (dmerged * gate).astype(BF16)
            dz_ref[b] = dz
            dy_ref[...] = _dot_nt(dz, w_ref[...]).astype(BF16)
            dgate = dmerged * z * gate * (1.0 - gate)
            dgt_ref[:, b * D:(b + 1) * D] = dgate.astype(BF16)
            dbg_ref[b:b + 1, :] += _colsum(dgate)

    tok = lambda w: pl.BlockSpec((tm, w), lambda i: (i, 0))
    gate_specs = [pl.BlockSpec((tm, D), lambda i, b=b: (i, b)) for b in range(N_BRANCH)]
    return pl.pallas_call(
        body, name=name, grid=(T // tm,),
        in_specs=[tok(D), tok(D), _row(D), _full((D, D))] + [tok(w) for w in widths] + gate_specs
        + [_full((N_BRANCH, D))] + [_full(w.shape) for w in wbrs],
        out_specs=[tok(D), pl.BlockSpec((N_BRANCH, tm, D), lambda i: (0, i, 0))] + [tok(w) for w in widths]
        + [tok(N_BRANCH * D), _row(D), _full((N_BRANCH, D))],
        out_shape=[jax.ShapeDtypeStruct((T, D), BF16), jax.ShapeDtypeStruct((N_BRANCH, T, D), BF16)]
        + [jax.ShapeDtypeStruct((T, w), BF16) for w in widths]
        + [jax.ShapeDtypeStruct((T, N_BRANCH * D), BF16), jax.ShapeDtypeStruct((1, D), F32),
           jax.ShapeDtypeStruct((N_BRANCH, D), F32)],
        compiler_params=_params("arbitrary"),
    )(dh, o, gpost, wo, *ys, proj, proj, proj, proj, bgate, *wbrs)


def _ple(h, pe, ggate, wpg, wple, gpost, *, name):
    T, D = h.shape
    E = pe.shape[1]
    tm = _tile(T, 512, 256, 128)

    def body(h_ref, p_ref, gg_ref, wpg_ref, wple_ref, gp_ref, out_ref):
        x = h_ref[...]
        n = (x * _rstd(x) * gg_ref[...]).astype(BF16)
        pg = jax.nn.sigmoid(_dot(n, wpg_ref[...]))
        e = _dot(p_ref[...].astype(BF16), wple_ref[...])
        out_ref[...] = x + pg * (e * _rstd(e) * gp_ref[...])

    tok = lambda w: pl.BlockSpec((tm, w), lambda i: (i, 0))
    return pl.pallas_call(
        body, name=name, grid=(T // tm,),
        in_specs=[tok(D), tok(E), _row(D), _full((D, D)), _full((E, D)), _row(D)],
        out_specs=tok(D), out_shape=jax.ShapeDtypeStruct((T, D), F32),
        compiler_params=_params("parallel"),
    )(h, pe, ggate, wpg, wple, gpost)


def _ple_bwd(dh, h, pe, ggate, wpg, wple, gpost, *, name):
    T, D = h.shape
    E = pe.shape[1]
    tm = _tile(T, 256, 128)

    def body(dh_ref, h_ref, p_ref, gg_ref, wpg_ref, wple_ref, gp_ref, dhi_ref, de_ref, dpgl_ref, n_ref, dgg_ref, dgp_ref):
        @pl.when(pl.program_id(0) == 0)
        def _():
            dgg_ref[...] = jnp.zeros_like(dgg_ref)
            dgp_ref[...] = jnp.zeros_like(dgp_ref)

        dh = dh_ref[...]
        x = h_ref[...]
        r = _rstd(x)
        n = (x * r * gg_ref[...]).astype(BF16)
        n_ref[...] = n
        pg = jax.nn.sigmoid(_dot(n, wpg_ref[...]))
        e = _dot(p_ref[...].astype(BF16), wple_ref[...])
        re = _rstd(e)
        de, dgn = _rms_bwd(dh * pg, e, re, gp_ref[...])
        dgp_ref[...] += _colsum(dgn)
        de_ref[...] = de.astype(BF16)
        dpgl = (dh * (e * re * gp_ref[...]) * pg * (1.0 - pg)).astype(BF16)
        dpgl_ref[...] = dpgl
        dn = _dot_nt(dpgl, wpg_ref[...])
        dx, dgn2 = _rms_bwd(dn, x, r, gg_ref[...])
        dgg_ref[...] += _colsum(dgn2)
        dhi_ref[...] = dh + dx

    tok = lambda w: pl.BlockSpec((tm, w), lambda i: (i, 0))
    return pl.pallas_call(
        body, name=name, grid=(T // tm,),
        in_specs=[tok(D), tok(D), tok(E), _row(D), _full((D, D)), _full((E, D)), _row(D)],
        out_specs=[tok(D), tok(D), tok(D), tok(D), _row(D), _row(D)],
        out_shape=[jax.ShapeDtypeStruct((T, D), F32), jax.ShapeDtypeStruct((T, D), BF16), jax.ShapeDtypeStruct((T, D), BF16),
                   jax.ShapeDtypeStruct((T, D), BF16), jax.ShapeDtypeStruct((1, D), F32), jax.ShapeDtypeStruct((1, D), F32)],
        compiler_params=_params("arbitrary"),
    )(dh, h, pe, ggate, wpg, wple, gpost)


def _loss_head(y, target, *, name):
    T, D = y.shape
    tm = _tile(T, 512, 256, 128)

    def body(y_ref, t_ref, l_ref, dy_ref):
        @pl.when(pl.program_id(0) == 0)
        def _():
            l_ref[...] = jnp.zeros_like(l_ref)

        err = y_ref[...] - t_ref[...]
        dy_ref[...] = err / D
        l_ref[...] += 0.5 * jnp.sum(jnp.mean(err * err, axis=-1, keepdims=True), axis=0, keepdims=True)

    tok = pl.BlockSpec((tm, D), lambda i: (i, 0))
    return pl.pallas_call(
        body, name=name, grid=(T // tm,),
        in_specs=[tok, tok], out_specs=[_full((8, 128)), tok],
        out_shape=[jax.ShapeDtypeStruct((8, 128), F32), jax.ShapeDtypeStruct((T, D), F32)],
        compiler_params=_params("arbitrary"),
    )(y, target)


def _layout(D):
    off = {'gates': 0}
    off['a'] = N_BRANCH * D
    off['b'] = off['a'] + 3 * A_WIDTH
    off['c'] = off['b'] + 3 * B_WIDTH
    off['d'] = off['c'] + 2 * C_WIDTH
    off['af'] = off['d'] + D_WIDTH + 2 * D_KV_WIDTH
    off['end'] = off['af'] + AF_PAD
    return off


def _seq_spec(S, width, col):
    assert col % width == 0
    return pl.BlockSpec((None, S, width), lambda b: (b, 0, col // width))


def _split3(x):
    hi = x.astype(BF16)
    r1 = x - hi.astype(F32)
    mid = r1.astype(BF16)
    lo = (r1 - mid.astype(F32)).astype(BF16)
    return hi, mid, lo


def _fox_cumsum(af_ref, bf_ref, c_scr, ct_scr):
    S = af_ref.shape[0]
    cb = _tile(S, 256, 128)
    tril = (lax.broadcasted_iota(jnp.int32, (cb, cb), 0) >= lax.broadcasted_iota(jnp.int32, (cb, cb), 1)).astype(BF16)
    carry = jnp.zeros((1, AF_LANES), F32)
    for j in range(S // cb):
        rows = slice(j * cb, (j + 1) * cb)
        hi, mid, lo = _split3(jax.nn.log_sigmoid(af_ref[rows, :] + bf_ref[...]))
        cblk = _dot(tril, hi) + _dot(tril, mid) + _dot(tril, lo) + carry
        c_scr[rows, :] = cblk
        carry = cblk[cb - 1:cb, :]
    ct_scr[...] = c_scr[...].T


def _fox_probs(q_ref, k_ref, c_scr, ct_scr, h, i, bq):
    end = (i + 1) * bq
    qs, hs = slice(i * bq, end), slice(HEAD_DIM * h, HEAD_DIM * (h + 1))
    s = _dot_nt(q_ref[qs, hs], k_ref[0:end, hs]) * HEAD_DIM ** -0.5
    s = s + (c_scr[qs, h:h + 1] - ct_scr[h:h + 1, 0:end])
    row = i * bq + lax.broadcasted_iota(jnp.int32, (bq, end), 0)
    col = lax.broadcasted_iota(jnp.int32, (bq, end), 1)
    s = jnp.where(row >= col, s, NEG_INF)
    e = jnp.exp(s - jnp.max(s, axis=-1, keepdims=True))
    return e / jnp.sum(e, axis=-1, keepdims=True)


def _fox_fwd(proj3, af3, bfor, off, *, name, rider=None):
    Bn, S, _ = proj3.shape
    bq = _tile(S, 256, 128)

    def body(q_ref, k_ref, v_ref, af_ref, bf_ref, o_ref, c_scr, ct_scr):
        _fox_cumsum(af_ref, bf_ref, c_scr, ct_scr)
        for h in range(A_HEADS):
            hs = slice(HEAD_DIM * h, HEAD_DIM * (h + 1))
            for i in range(S // bq):
                p = _fox_probs(q_ref, k_ref, c_scr, ct_scr, h, i, bq)
                o_ref[i * bq:(i + 1) * bq, hs] = _dot(p.astype(BF16), v_ref[0:(i + 1) * bq, hs]).astype(BF16)

    return _call(
        body, (proj3, proj3, proj3, af3, bfor), name=name, grid=(Bn,),
        in_specs=[_seq_spec(S, A_WIDTH, off['a']), _seq_spec(S, A_WIDTH, off['a'] + A_WIDTH),
                  _seq_spec(S, A_WIDTH, off['a'] + 2 * A_WIDTH), _seq_spec(S, AF_LANES, 0), _full((1, AF_LANES))],
        out_specs=[_seq_spec(S, A_WIDTH, 0)], out_shape=[jax.ShapeDtypeStruct((Bn, S, A_WIDTH), BF16)],
        scratch_shapes=[pltpu.VMEM((S, AF_LANES), F32), pltpu.VMEM((AF_LANES, S), F32)],
        sem=("parallel",), rider=rider)[0]


def _fox_bwd(proj3, af3, bfor, dya3, off, *, name, rider=None):
    Bn, S, _ = proj3.shape
    bq = _tile(S, 256, 128)
    cb = _tile(S, 256, 128)
    scale = HEAD_DIM ** -0.5

    def body(q_ref, k_ref, v_ref, af_ref, bf_ref, do_ref, dqkv_ref, da_ref, dbf_ref,
             c_scr, ct_scr, dk_scr, dv_scr, dc_scr, dct_scr):
        @pl.when(pl.program_id(0) == 0)
        def _():
            dbf_ref[...] = jnp.zeros_like(dbf_ref)

        _fox_cumsum(af_ref, bf_ref, c_scr, ct_scr)
        dk_scr[...] = jnp.zeros_like(dk_scr)
        dv_scr[...] = jnp.zeros_like(dv_scr)
        dc_scr[...] = jnp.zeros_like(dc_scr)
        dct_scr[...] = jnp.zeros_like(dct_scr)
        for h in range(A_HEADS):
            hs = slice(HEAD_DIM * h, HEAD_DIM * (h + 1))
            for i in range(S // bq):
                end = (i + 1) * bq
                qs = slice(i * bq, end)
                p = _fox_probs(q_ref, k_ref, c_scr, ct_scr, h, i, bq)
                doh = do_ref[qs, hs]
                dp = _dot_nt(doh, v_ref[0:end, hs])
                ds = p * (dp - jnp.sum(p * dp, axis=-1, keepdims=True))
                dsb = ds.astype(BF16)
                dqkv_ref[qs, hs] = (_dot(dsb, k_ref[0:end, hs]) * scale).astype(BF16)
                dk_scr[0:end, hs] += _dot_tn(dsb, q_ref[qs, hs]) * scale
                dv_scr[0:end, hs] += _dot_tn(p.astype(BF16), doh)
                dc_scr[qs, h:h + 1] += jnp.sum(ds, axis=-1, keepdims=True)
                dct_scr[h:h + 1, 0:end] += -jnp.sum(ds, axis=0, keepdims=True)
        dqkv_ref[:, A_WIDTH:2 * A_WIDTH] = dk_scr[...].astype(BF16)
        dqkv_ref[:, 2 * A_WIDTH:3 * A_WIDTH] = dv_scr[...].astype(BF16)
        dc_scr[...] += dct_scr[...].T
        triu = (lax.broadcasted_iota(jnp.int32, (cb, cb), 0) <= lax.broadcasted_iota(jnp.int32, (cb, cb), 1)).astype(BF16)
        carry = jnp.zeros((1, AF_LANES), F32)
        dbf = jnp.zeros((1, AF_LANES), F32)
        for j in reversed(range(S // cb)):
            rows = slice(j * cb, (j + 1) * cb)
            hi, mid, lo = _split3(dc_scr[rows, :])
            dlf = _dot(triu, hi) + _dot(triu, mid) + _dot(triu, lo) + carry
            carry = dlf[0:1, :]
            da = dlf * jax.nn.sigmoid(-(af_ref[rows, :] + bf_ref[...]))
            dbf = dbf + _colsum(da)
            da_ref[rows, 0:AF_LANES] = da.astype(BF16)
        da_ref[:, AF_LANES:AF_PAD] = jnp.zeros((S, AF_PAD - AF_LANES), BF16)
        dbf_ref[...] += dbf

    return _call(
        body, (proj3, proj3, proj3, af3, bfor, dya3), name=name, grid=(Bn,),
        in_specs=[_seq_spec(S, A_WIDTH, off['a']), _seq_spec(S, A_WIDTH, off['a'] + A_WIDTH),
                  _seq_spec(S, A_WIDTH, off['a'] + 2 * A_WIDTH), _seq_spec(S, AF_LANES, 0), _full((1, AF_LANES)),
                  _seq_spec(S, A_WIDTH, 0)],
        out_specs=[_seq_spec(S, 3 * A_WIDTH, 0), _seq_spec(S, AF_PAD, 0), _full((1, AF_LANES))],
        out_shape=[jax.ShapeDtypeStruct((Bn, S, 3 * A_WIDTH), BF16), jax.ShapeDtypeStruct((Bn, S, AF_PAD), BF16),
                   jax.ShapeDtypeStruct((1, AF_LANES), F32)],
        scratch_shapes=[pltpu.VMEM((S, AF_LANES), F32), pltpu.VMEM((AF_LANES, S), F32), pltpu.VMEM((S, A_WIDTH), F32),
                        pltpu.VMEM((S, A_WIDTH), F32), pltpu.VMEM((S, AF_LANES), F32), pltpu.VMEM((AF_LANES, S), F32)],
        sem=("arbitrary",), rider=rider)


def _shift_down(z, s):
    if s == 0:
        return z
    row = lax.broadcasted_iota(jnp.int32, z.shape, 0)
    return jnp.where(row >= s, pltpu.roll(z, s, 0), 0.0)


def _shift_up(z, s):
    if s == 0:
        return z
    n = z.shape[0]
    row = lax.broadcasted_iota(jnp.int32, z.shape, 0)
    return jnp.where(row < n - s, pltpu.roll(z, n - s, 0), 0.0)


def _conv_fwd(z, w_ref, K):
    acc = jnp.zeros_like(z)
    for k in range(K):
        acc = acc + w_ref[k:k + 1, :] * _shift_down(z, K - 1 - k)
    return acc


def _conv_bwd(dy, z, w_ref, dw_ref, K):
    dz = jnp.zeros_like(z)
    for k in range(K):
        dz = dz + w_ref[k:k + 1, :] * _shift_up(dy, K - 1 - k)
        dw_ref[k:k + 1, :] += _colsum(dy * _shift_down(z, K - 1 - k))
    return dz


def _sconv_fwd(proj3, w, off, *, name):
    Bn, S, _ = proj3.shape

    def body(bg_ref, cg_ref, xb_ref, w_ref, o_ref):
        z = cg_ref[...].astype(F32) * xb_ref[...].astype(F32)
        o_ref[...] = (bg_ref[...].astype(F32) * _conv_fwd(z, w_ref, SHORT_CONV)).astype(BF16)

    return pl.pallas_call(
        body, name=name, grid=(Bn,),
        in_specs=[_seq_spec(S, B_WIDTH, off['b'] + j * B_WIDTH) for j in range(3)] + [_full((SHORT_CONV, B_WIDTH))],
        out_specs=_seq_spec(S, B_WIDTH, 0), out_shape=jax.ShapeDtypeStruct((Bn, S, B_WIDTH), BF16),
        compiler_params=_params("parallel"),
    )(proj3, proj3, proj3, w)


def _sconv_bwd(proj3, w, dyb3, off, *, name):
    Bn, S, _ = proj3.shape

    def body(bg_ref, cg_ref, xb_ref, w_ref, do_ref, din_ref, dw_ref):
        @pl.when(pl.program_id(0) == 0)
        def _():
            dw_ref[...] = jnp.zeros_like(dw_ref)

        cg, xb = cg_ref[...].astype(F32), xb_ref[...].astype(F32)
        z = cg * xb
        do = do_ref[...].astype(F32)
        din_ref[:, 0:B_WIDTH] = (do * _conv_fwd(z, w_ref, SHORT_CONV)).astype(BF16)
        dz = _conv_bwd(do * bg_ref[...].astype(F32), z, w_ref, dw_ref, SHORT_CONV)
        din_ref[:, B_WIDTH:2 * B_WIDTH] = (dz * xb).astype(BF16)
        din_ref[:, 2 * B_WIDTH:3 * B_WIDTH] = (dz * cg).astype(BF16)

    return pl.pallas_call(
        body, name=name, grid=(Bn,),
        in_specs=[_seq_spec(S, B_WIDTH, off['b'] + j * B_WIDTH) for j in range(3)]
        + [_full((SHORT_CONV, B_WIDTH)), _seq_spec(S, B_WIDTH, 0)],
        out_specs=[_seq_spec(S, 3 * B_WIDTH, 0), _full((SHORT_CONV, B_WIDTH))],
        out_shape=[jax.ShapeDtypeStruct((Bn, S, 3 * B_WIDTH), BF16), jax.ShapeDtypeStruct((SHORT_CONV, B_WIDTH), F32)],
        compiler_params=_params("arbitrary"),
    )(proj3, proj3, proj3, w, dyb3)


def _cconv_pre(cin_ref, w_ref, cb_ref):
    x = cin_ref[...].astype(F32)
    a, gt = x[:, 0:C_WIDTH], x[:, C_WIDTH:2 * C_WIDTH]
    sg = jax.nn.sigmoid(gt)
    glu = a * sg
    y0 = _conv_fwd(glu, w_ref, CONF_CONV) + cb_ref[...]
    mu = jnp.mean(y0, axis=-1, keepdims=True)
    xc = y0 - mu
    rs = lax.rsqrt(jnp.mean(xc * xc, axis=-1, keepdims=True) + EPS)
    return a, sg, glu, xc * rs, rs


def _cconv_fwd(proj3, w, cbias, lg, lb, off, *, name):
    Bn, S, _ = proj3.shape

    def body(cin_ref, w_ref, cb_ref, lg_ref, lb_ref, o_ref):
        _, _, _, xh, _ = _cconv_pre(cin_ref, w_ref, cb_ref)
        ln = xh * lg_ref[...] + lb_ref[...]
        o_ref[...] = (ln * jax.nn.sigmoid(ln)).astype(BF16)

    return pl.pallas_call(
        body, name=name, grid=(Bn,),
        in_specs=[_seq_spec(S, 2 * C_WIDTH, off['c']), _full((CONF_CONV, C_WIDTH)), _full((1, C_WIDTH)),
                  _full((1, C_WIDTH)), _full((1, C_WIDTH))],
        out_specs=_seq_spec(S, C_WIDTH, 0), out_shape=jax.ShapeDtypeStruct((Bn, S, C_WIDTH), BF16),
        compiler_params=_params("parallel"),
    )(proj3, w, cbias, lg, lb)


def _cconv_bwd(proj3, w, cbias, lg, lb, dyc3, off, *, name):
    Bn, S, _ = proj3.shape

    def body(cin_ref, w_ref, cb_ref, lg_ref, lb_ref, do_ref, din_ref, dw_ref, dcb_ref, dlg_ref, dlb_ref):
        @pl.when(pl.program_id(0) == 0)
        def _():
            for r in (dw_ref, dcb_ref, dlg_ref, dlb_ref):
                r[...] = jnp.zeros_like(r)

        a, sg, glu, xh, rs = _cconv_pre(cin_ref, w_ref, cb_ref)
        ln = xh * lg_ref[...] + lb_ref[...]
        sl = jax.nn.sigmoid(ln)
        dln = do_ref[...].astype(F32) * (sl * (1.0 + ln * (1.0 - sl)))
        dlg_ref[...] += _colsum(dln * xh)
        dlb_ref[...] += _colsum(dln)
        dxh = dln * lg_ref[...]
        dy0 = rs * (dxh - jnp.mean(dxh, axis=-1, keepdims=True) - xh * jnp.mean(dxh * xh, axis=-1, keepdims=True))
        dcb_ref[...] += _colsum(dy0)
        dglu = _conv_bwd(dy0, glu, w_ref, dw_ref, CONF_CONV)
        din_ref[:, 0:C_WIDTH] = (dglu * sg).astype(BF16)
        din_ref[:, C_WIDTH:2 * C_WIDTH] = (dglu * a * sg * (1.0 - sg)).astype(BF16)

    vec = _full((1, C_WIDTH))
    return pl.pallas_call(
        body, name=name, grid=(Bn,),
        in_specs=[_seq_spec(S, 2 * C_WIDTH, off['c']), _full((CONF_CONV, C_WIDTH)), vec, vec, vec, _seq_spec(S, C_WIDTH, 0)],
        out_specs=[_seq_spec(S, 2 * C_WIDTH, 0), _full((CONF_CONV, C_WIDTH)), vec, vec, vec],
        out_shape=[jax.ShapeDtypeStruct((Bn, S, 2 * C_WIDTH), BF16), jax.ShapeDtypeStruct((CONF_CONV, C_WIDTH), F32)]
        + [jax.ShapeDtypeStruct((1, C_WIDTH), F32)] * 3,
        compiler_params=_params("arbitrary"),
    )(proj3, w, cbias, lg, lb, dyc3)


def _swa_band(x_ref, g, nb):
    xb = x_ref[:, HEAD_DIM * g:HEAD_DIM * (g + 1)].reshape(nb, Q_BLOCK, HEAD_DIM)
    prev = jnp.concatenate([jnp.zeros((1, Q_BLOCK, HEAD_DIM), xb.dtype), xb[:-1]], axis=0)
    return jnp.concatenate([prev, xb], axis=1)


def _swa_probs(q_ref, kband, bias_ref, sk_ref, h, nb):
    qh = q_ref[:, HEAD_DIM * h:HEAD_DIM * (h + 1)].reshape(nb, Q_BLOCK, HEAD_DIM)
    s = jnp.einsum('nqd,nsd->nqs', qh, kband, preferred_element_type=F32) * HEAD_DIM ** -0.5 + bias_ref[h][None]
    shape = (nb, Q_BLOCK, 2 * Q_BLOCK)
    n = lax.broadcasted_iota(jnp.int32, shape, 0)
    dist = lax.broadcasted_iota(jnp.int32, shape, 1) + Q_BLOCK - lax.broadcasted_iota(jnp.int32, shape, 2)
    col = lax.broadcasted_iota(jnp.int32, shape, 2)
    valid = (dist >= 0) & (dist < WINDOW) & ((n > 0) | (col >= Q_BLOCK))
    s = jnp.where(valid, s, NEG_INF)
    sink = sk_ref[h:h + 1, 0:1].reshape(1, 1, 1)
    m = jnp.maximum(jnp.max(s, axis=-1, keepdims=True), sink)
    e = jnp.exp(s - m)
    es = jnp.exp(sink - m)
    den = jnp.sum(e, axis=-1, keepdims=True) + es
    return qh, e / den, es / den


def _swa_fwd(proj3, band_bias, sinks, off, *, name):
    Bn, S, _ = proj3.shape
    nb = S // Q_BLOCK

    def body(q_ref, k_ref, v_ref, bias_ref, sk_ref, o_ref):
        for g in range(D_KV_HEADS):
            kband, vband = _swa_band(k_ref, g, nb), _swa_band(v_ref, g, nb)
            for h in range(g * D_GROUP, (g + 1) * D_GROUP):
                _, p, _ = _swa_probs(q_ref, kband, bias_ref, sk_ref, h, nb)
                out = jnp.einsum('nqs,nsd->nqd', p.astype(BF16), vband, preferred_element_type=F32)
                o_ref[:, HEAD_DIM * h:HEAD_DIM * (h + 1)] = out.reshape(S, HEAD_DIM).astype(BF16)

    kcol = off['d'] + D_WIDTH
    return pl.pallas_call(
        body, name=name, grid=(Bn,),
        in_specs=[_seq_spec(S, D_WIDTH, off['d']), _seq_spec(S, D_KV_WIDTH, kcol), _seq_spec(S, D_KV_WIDTH, kcol + D_KV_WIDTH),
                  _full((D_Q_HEADS, Q_BLOCK, 2 * Q_BLOCK)), _full((D_Q_HEADS, 128))],
        out_specs=_seq_spec(S, D_WIDTH, 0), out_shape=jax.ShapeDtypeStruct((Bn, S, D_WIDTH), BF16),
        compiler_params=_params("parallel"),
    )(proj3, proj3, proj3, band_bias, sinks)


def _swa_bwd(proj3, band_bias, sinks, dyd3, off, *, name):
    Bn, S, _ = proj3.shape
    nb = S // Q_BLOCK
    scale = HEAD_DIM ** -0.5

    def body(q_ref, k_ref, v_ref, bias_ref, sk_ref, do_ref, dqkv_ref, dband_ref, dsk_ref):
        @pl.when(pl.program_id(0) == 0)
        def _():
            dband_ref[...] = jnp.zeros_like(dband_ref)
            dsk_ref[...] = jnp.zeros_like(dsk_ref)

        def unband(acc):
            prev, cur = acc[:, 0:Q_BLOCK, :], acc[:, Q_BLOCK:2 * Q_BLOCK, :]
            nxt = jnp.concatenate([prev[1:], jnp.zeros((1, Q_BLOCK, HEAD_DIM), F32)], axis=0)
            return (cur + nxt).reshape(S, HEAD_DIM).astype(BF16)

        for g in range(D_KV_HEADS):
            kband, vband = _swa_band(k_ref, g, nb), _swa_band(v_ref, g, nb)
            dkb = jnp.zeros((nb, 2 * Q_BLOCK, HEAD_DIM), F32)
            dvb = jnp.zeros((nb, 2 * Q_BLOCK, HEAD_DIM), F32)
            for h in range(g * D_GROUP, (g + 1) * D_GROUP):
                hs = slice(HEAD_DIM * h, HEAD_DIM * (h + 1))
                qh, p, ps = _swa_probs(q_ref, kband, bias_ref, sk_ref, h, nb)
                doh = do_ref[:, hs].reshape(nb, Q_BLOCK, HEAD_DIM)
                dp = jnp.einsum('nqd,nsd->nqs', doh, vband, preferred_element_type=F32)
                delta = jnp.sum(p * dp, axis=-1, keepdims=True)
                ds = p * (dp - delta)
                dsink = jnp.sum(jnp.sum(-ps * delta, axis=0), axis=0, keepdims=True)
                dsk_ref[h:h + 1, :] += jnp.broadcast_to(dsink, (1, 128))
                dband_ref[h] += jnp.sum(ds, axis=0)
                dsb = ds.astype(BF16)
                dq = jnp.einsum('nqs,nsd->nqd', dsb, kband, preferred_element_type=F32) * scale
                dqkv_ref[:, hs] = dq.reshape(S, HEAD_DIM).astype(BF16)
                dkb = dkb + jnp.einsum('nqs,nqd->nsd', dsb, qh, preferred_element_type=F32) * scale
                dvb = dvb + jnp.einsum('nqs,nqd->nsd', p.astype(BF16), doh, preferred_element_type=F32)
            dqkv_ref[:, D_WIDTH + HEAD_DIM * g:D_WIDTH + HEAD_DIM * (g + 1)] = unband(dkb)
            dqkv_ref[:, D_WIDTH + D_KV_WIDTH + HEAD_DIM * g:D_WIDTH + D_KV_WIDTH + HEAD_DIM * (g + 1)] = unband(dvb)

    kcol = off['d'] + D_WIDTH
    wq = D_WIDTH + 2 * D_KV_WIDTH
    return pl.pallas_call(
        body, name=name, grid=(Bn,),
        in_specs=[_seq_spec(S, D_WIDTH, off['d']), _seq_spec(S, D_KV_WIDTH, kcol), _seq_spec(S, D_KV_WIDTH, kcol + D_KV_WIDTH),
                  _full((D_Q_HEADS, Q_BLOCK, 2 * Q_BLOCK)), _full((D_Q_HEADS, 128)), _seq_spec(S, D_WIDTH, 0)],
        out_specs=[_seq_spec(S, wq, 0), _full((D_Q_HEADS, Q_BLOCK, 2 * Q_BLOCK)), _full((D_Q_HEADS, 128))],
        out_shape=[jax.ShapeDtypeStruct((Bn, S, wq), BF16), jax.ShapeDtypeStruct((D_Q_HEADS, Q_BLOCK, 2 * Q_BLOCK), F32),
                   jax.ShapeDtypeStruct((D_Q_HEADS, 128), F32)],
        compiler_params=_params("arbitrary"),
    )(proj3, proj3, proj3, band_bias, sinks, dyd3)


def _assemble(pieces, *, name):
    T = pieces[0].shape[0]
    widths = [p.shape[1] for p in pieces]
    tm = _tile(T, 512, 256, 128)

    def body(*refs):
        out_ref = refs[-1]
        col = 0
        for r, w in zip(refs[:-1], widths):
            out_ref[:, col:col + w] = r[...]
            col += w

    return pl.pallas_call(
        body, name=name, grid=(T // tm,),
        in_specs=[pl.BlockSpec((tm, w), lambda i: (i, 0)) for w in widths],
        out_specs=pl.BlockSpec((tm, sum(widths)), lambda i: (i, 0)),
        out_shape=jax.ShapeDtypeStruct((T, sum(widths)), BF16),
        compiler_params=_params("parallel"),
    )(*pieces)


def _relbias_grad(dband, onehot, *, name):
    L, H, n = dband.shape
    R = onehot.shape[0]

    def body(d_ref, oh_ref, out_ref):
        tot = d_ref[0]
        for l in range(1, L):
            tot = tot + d_ref[l]
        out_ref[...] = lax.dot_general(oh_ref[...], tot, (((1,), (1,)), ((), ())), preferred_element_type=F32,
                                       precision=lax.Precision.HIGHEST)

    return pl.pallas_call(
        body, name=name, out_shape=jax.ShapeDtypeStruct((R, H), F32),
        compiler_params=pltpu.CompilerParams(vmem_limit_bytes=VMEM_LIMIT_V7X),
    )(dband, onehot)


def _all_reduce(buf, *, name):
    R, C = buf.shape

    def body(x_ref, o_ref, land, send, recv):
        me = _my_id()
        land[pl.ds(me, 1)] = x_ref[...][None]
        sends = []
        for k in range(1, N_DEV):
            to, _ = _peer(k)
            cp = pltpu.make_async_remote_copy(src_ref=x_ref, dst_ref=land.at[me], send_sem=send.at[k - 1],
                                              recv_sem=recv.at[k - 1], device_id=to, device_id_type=MESH)
            cp.start()
            sends.append(cp)
        for k in range(1, N_DEV):
            frm, frm_id = _peer(k)
            pltpu.make_async_remote_copy(src_ref=x_ref, dst_ref=land.at[frm_id], send_sem=send.at[k - 1],
                                         recv_sem=recv.at[k - 1], device_id=frm, device_id_type=MESH).wait_recv()
        for cp in sends:
            cp.wait_send()
        acc = land[0]
        for d in range(1, N_DEV):
            acc = acc + land[d]
        o_ref[...] = acc

    vmem = pl.BlockSpec(memory_space=pltpu.VMEM)
    return pl.pallas_call(
        body, name=name, in_specs=[vmem], out_specs=vmem, out_shape=jax.ShapeDtypeStruct((R, C), F32),
        scratch_shapes=[pltpu.VMEM((N_DEV, R, C), F32), pltpu.SemaphoreType.DMA((N_DEV - 1,)),
                        pltpu.SemaphoreType.DMA((N_DEV - 1,))],
        compiler_params=pltpu.CompilerParams(vmem_limit_bytes=VMEM_LIMIT_V7X),
    )(buf)


def _row_tile(rows, row_bytes, align):
    fits = [t for t in range(align, rows + 1, align) if rows % t == 0]
    small = [t for t in fits if t * row_bytes <= 2**20]
    return max(small) if small else (min(fits) if fits else rows)


def _sum8(recvs, *, name):
    L = len(recvs)
    _, rows, C = recvs[0].shape
    tr = _row_tile(rows, C * 4 * L, 16)

    def body(*refs):
        o_ref = refs[-1]
        for l, r_ref in enumerate(refs[:-1]):
            acc = r_ref[0].astype(F32)
            for d in range(1, N_DEV):
                acc = acc + r_ref[d].astype(F32)
            o_ref[l] = acc

    return pl.pallas_call(
        body, name=name, grid=(rows // tr,),
        in_specs=[pl.BlockSpec((N_DEV, tr, C), lambda i: (0, i, 0))] * L,
        out_specs=pl.BlockSpec((L, tr, C), lambda i: (0, i, 0)),
        out_shape=jax.ShapeDtypeStruct((L, rows, C), F32), compiler_params=_params("parallel"),
    )(*recvs)


def _band_bias(rel_bias, onehot, *, name):
    R, H = rel_bias.shape

    def body(rb_ref, oh_ref, out_ref):
        out_ref[...] = lax.dot_general(rb_ref[...], oh_ref[...], (((0,), (0,)), ((), ())), preferred_element_type=F32,
                                       precision=lax.Precision.HIGHEST)

    return pl.pallas_call(
        body, name=name, out_shape=jax.ShapeDtypeStruct((H, onehot.shape[1]), F32),
        compiler_params=pltpu.CompilerParams(vmem_limit_bytes=VMEM_LIMIT_V7X),
    )(rel_bias, onehot)


def _adamw(g, w, m, v, *, name):
    rows, C = g.shape
    tr = _row_tile(rows, C * 4, 8)

    def body(g_ref, w_ref, m_ref, v_ref, d_ref, mo_ref, vo_ref):
        gt = g_ref[...]
        mn = ADAM_B1 * m_ref[...] + (1.0 - ADAM_B1) * gt
        vn = ADAM_B2 * v_ref[...] + (1.0 - ADAM_B2) * jnp.square(gt)
        m_hat = mn / (1.0 - ADAM_B1 ** ADAM_STEP)
        v_hat = vn / (1.0 - ADAM_B2 ** ADAM_STEP)
        d_ref[...] = -ADAM_LR * (m_hat / (jnp.sqrt(v_hat) + ADAM_EPS) + ADAM_WD * w_ref[...])
        mo_ref[...] = mn
        vo_ref[...] = vn

    spec = pl.BlockSpec((tr, C), lambda i: (i, 0))
    return pl.pallas_call(
        body, name=name, grid=(rows // tr,), in_specs=[spec] * 4, out_specs=[spec] * 3,
        out_shape=[jax.ShapeDtypeStruct((rows, C), F32)] * 3, compiler_params=_params("parallel"),
    )(g, w, m, v)


def _in_splits():
    a_f = 3 * A_WIDTH
    b = a_f + A_HEADS
    c = b + 3 * B_WIDTH
    d = c + 2 * C_WIDTH
    gates = d + D_WIDTH + 2 * D_KV_WIDTH
    return a_f, b, c, d, gates


def _permute_in(w):
    a_f, b, c, d, gates = _in_splits()
    pad = jnp.zeros(w.shape[:-1] + (AF_PAD - A_HEADS,), w.dtype)
    return jnp.concatenate([w[..., gates:], w[..., :a_f], w[..., b:c], w[..., c:d], w[..., d:gates], w[..., a_f:b], pad], axis=-1)


def _unpermute_in(g, D):
    off = _layout(D)
    return jnp.concatenate([g[..., off['a']:off['b']], g[..., off['af']:off['af'] + A_HEADS], g[..., off['b']:off['c']],
                            g[..., off['c']:off['d']], g[..., off['d']:off['af']], g[..., :off['a']]], axis=-1)


def _bucket_onehot():
    dist = np.maximum(np.arange(Q_BLOCK)[:, None] + Q_BLOCK - np.arange(2 * Q_BLOCK)[None, :], 0)
    max_exact = REL_BUCKETS // 2
    large = max_exact + (np.log(np.maximum(dist, 1).astype(np.float32) / np.float32(max_exact))
                         / np.float32(math.log(REL_MAX_DIST / max_exact)) * np.float32(REL_BUCKETS - max_exact)).astype(np.int32)
    bucket = np.where(dist < max_exact, dist, np.minimum(large, REL_BUCKETS - 1))
    return (bucket.reshape(1, -1) == np.arange(REL_BUCKETS)[:, None]).astype(np.float32)


GATHER_SEQ = [('ffn1_up', ['ffn1_w_gu']), ('ffn1_down', ['ffn1_w_down']), ('proj', ['w_in']),
              ('fox', ['w_br_a', 'w_br_b', 'w_br_c', 'w_br_d', 'w_o', 'w_ple_gate', 'w_ple']),
              ('ffn2_up', ['ffn2_w_gu']), ('ffn2_down', ['ffn2_w_down'])]
GATHER_AHEAD = 2
SCATTER_HOSTS = {'ffn2_down_bwd': ['w_ple', 'w_ple_gate'], 'ffn2_up_bwd': ['ffn2_w_gu'],
                 'fox_bwd': ['ffn2_w_down', 'w_o', 'w_br_a', 'w_br_b', 'w_br_c', 'w_br_d'], 'proj_bwd': ['w_in'],
                 'ffn1_down_bwd': [], 'ffn1_up_bwd': ['ffn1_w_down', 'ffn1_w_gu']}


def _pack(parts):
    flat = jnp.concatenate([q.reshape(-1).astype(F32) for q in parts])
    return jnp.pad(flat, (0, (-flat.shape[0]) % 1024)).reshape(-1, 128)


def _unpack(buf, shapes):
    flat, out, pos = buf.reshape(-1), [], 0
    for s in shapes:
        n = math.prod(s)
        out.append(flat[pos:pos + n].reshape(s))
        pos += n
    return out


def kernel(x, p, ffn1_norm_pre, ffn1_w_gu, ffn1_w_down, ffn1_norm_post, mix_norm_pre, w_in, b_forget, b_gate, conv_short, conv_dw, conv_dw_bias, conv_ln_gain, conv_ln_bias, attn_sinks, rel_bias, w_br_a, w_br_b, w_br_c, w_br_d, w_o, mix_norm_post, ffn2_norm_pre, ffn2_w_gu, ffn2_w_down, ffn2_norm_post, ple_norm_gate, w_ple_gate, w_ple, ple_norm_post, loss_target, m_ffn1_norm_pre, m_ffn1_w_gu, m_ffn1_w_down, m_ffn1_norm_post, m_mix_norm_pre, m_w_in, m_b_forget, m_b_gate, m_conv_short, m_conv_dw, m_conv_dw_bias, m_conv_ln_gain, m_conv_ln_bias, m_attn_sinks, m_rel_bias, m_w_br_a, m_w_br_b, m_w_br_c, m_w_br_d, m_w_o, m_mix_norm_post, m_ffn2_norm_pre, m_ffn2_w_gu, m_ffn2_w_down, m_ffn2_norm_post, m_ple_norm_gate, m_w_ple_gate, m_w_ple, m_ple_norm_post, v_ffn1_norm_pre, v_ffn1_w_gu, v_ffn1_w_down, v_ffn1_norm_post, v_mix_norm_pre, v_w_in, v_b_forget, v_b_gate, v_conv_short, v_conv_dw, v_conv_dw_bias, v_conv_ln_gain, v_conv_ln_bias, v_attn_sinks, v_rel_bias, v_w_br_a, v_w_br_b, v_w_br_c, v_w_br_d, v_w_o, v_mix_norm_post, v_ffn2_norm_pre, v_ffn2_w_gu, v_ffn2_w_down, v_ffn2_norm_post, v_ple_norm_gate, v_w_ple_gate, v_w_ple, v_ple_norm_post):
    a = dict(locals())
    Bn, S, D = x.shape
    T = Bn * S
    L, E = p.shape[0], p.shape[-1]
    F = ffn1_w_down.shape[1] * N_DEV
    off = _layout(D)
    PW = off['end']
    me = _my_id()

    shard = {n: a[n].astype(BF16) for n in BIG}
    shard['ffn1_w_gu'] = jnp.swapaxes(ffn1_w_gu, 1, 2).astype(BF16)
    shard['ffn2_w_gu'] = jnp.swapaxes(ffn2_w_gu, 1, 2).astype(BF16)
    shard['w_in'] = _permute_in(w_in).astype(BF16)
    is_col = lambda names: [n in COL_SHARDED for n in names]
    head = [n for _, names in GATHER_SEQ[:GATHER_AHEAD] for n in names]
    first = _Exchange("gather", [shard[n][0] for n in head], is_col(head))
    full = [dict(zip(head, first.run_alone("gather_head")))] + [{} for _ in range(1, L)]
    hosts = [h_ for h_, _ in GATHER_SEQ]

    def with_gather(fn, host, i, *args, **kw):
        li, lj = divmod(i * len(hosts) + hosts.index(host) + GATHER_AHEAD, len(hosts))
        names = GATHER_SEQ[lj][1]
        rider = _Exchange("gather", [shard[n][li] for n in names], is_col(names)) if li < L else None
        out = fn(*args, rider=rider, **kw)
        if rider is not None:
            full[li].update(zip(names, rider.result))
        return out

    cw = conv_short.shape[2]
    conv_full = [lax.dynamic_update_slice(jnp.zeros(c.shape[:2] + (cw * N_DEV,), F32), c, (0, 0, me * cw))
                 for c in (conv_short, conv_dw)]
    conv_shapes = [c.shape for c in conv_full]
    cs_all, cdw_all = _unpack(_all_reduce(_pack(conv_full), name="gather_conv"), conv_shapes)

    onehot = jnp.asarray(_bucket_onehot())
    band_bias = _band_bias(rel_bias, onehot, name="band_bias").reshape(D_Q_HEADS, Q_BLOCK, 2 * Q_BLOCK)

    def vec(name, i):
        return a[name][i][None]

    def lay(i):
        return dict(
            bfor=jnp.pad(b_forget[i], (0, AF_LANES - A_HEADS))[None], bgate=b_gate[i].reshape(N_BRANCH, D),
            cs=cs_all[i], cdw=cdw_all[i], cb=conv_dw_bias[i][None], lg=conv_ln_gain[i][None], lb=conv_ln_bias[i][None],
            sinks=jnp.broadcast_to(attn_sinks[i][:, None], (D_Q_HEADS, 128)),
            wbrs=[full[i][n] for n in ('w_br_a', 'w_br_b', 'w_br_c', 'w_br_d')], pe=p[i].reshape(T, E))

    def wgu(k, i):
        return full[i][f'ffn{k}_w_gu'].reshape(2, F, D)

    h = x.reshape(T, D)
    saved = []
    for i in range(L):
        s = dict(h0=h)
        s['gu1'], s['n1'] = with_gather(_rms_mm, 'ffn1_up', i, h, vec('ffn1_norm_pre', i), wgu(1, i), nt=True,
                                        out_dtype=BF16, save_n=True, name=f"ffn1_up_{i}")
        s['h1'], s['f1'] = with_gather(_ffn_down, 'ffn1_down', i, s['gu1'], full[i]['ffn1_w_down'], h,
                                       vec('ffn1_norm_post', i), name=f"ffn1_down_{i}")
        win = full[i]['w_in']
        proj, s['u'] = with_gather(_rms_mm, 'proj', i, s['h1'], vec('mix_norm_pre', i), win[None], nt=False,
                                   out_dtype=BF16, save_n=True, name=f"proj_{i}")
        af = _rms_mm(s['h1'], vec('mix_norm_pre', i), win[None, :, off['af']:off['af'] + AF_LANES], nt=False,
                     out_dtype=F32, save_n=False, name=f"proj_forget_{i}")
        s['proj'] = proj.reshape(T, PW)
        s['proj3'], s['af3'] = proj.reshape(Bn, S, PW), af.reshape(Bn, S, AF_LANES)
        q = lay(i)
        ya = with_gather(_fox_fwd, 'fox', i, s['proj3'], s['af3'], q['bfor'], off, name=f"fox_{i}")
        yb = _sconv_fwd(s['proj3'], q['cs'], off, name=f"sconv_{i}")
        yc = _cconv_fwd(s['proj3'], q['cdw'], q['cb'], q['lg'], q['lb'], off, name=f"cconv_{i}")
        yd = _swa_fwd(s['proj3'], band_bias, q['sinks'], off, name=f"swa_{i}")
        s['ys'] = [y.reshape(T, y.shape[-1]) for y in (ya, yb, yc, yd)]
        s['h2'], s['o'], s['merged'] = _merge_out(s['ys'], s['proj'], q['bgate'], q['wbrs'], full[i]['w_o'], s['h1'],
                                                  vec('mix_norm_post', i), name=f"merge_{i}")
        s['gu2'], s['n2'] = with_gather(_rms_mm, 'ffn2_up', i, s['h2'], vec('ffn2_norm_pre', i), wgu(2, i), nt=True,
                                        out_dtype=BF16, save_n=True, name=f"ffn2_up_{i}")
        s['h3'], s['f2'] = with_gather(_ffn_down, 'ffn2_down', i, s['gu2'], full[i]['ffn2_w_down'], s['h2'],
                                       vec('ffn2_norm_post', i), name=f"ffn2_down_{i}")
        h = _ple(s['h3'], q['pe'], vec('ple_norm_gate', i), full[i]['w_ple_gate'], full[i]['w_ple'],
                 vec('ple_norm_post', i), name=f"ple_{i}")
        saved.append(s)

    lpart, dh = _loss_head(h, loss_target.reshape(T, D), name="loss_head")
    loss = lax.psum(lpart[0, 0], MESH_AXES)

    gbuf = [{} for _ in range(L)]
    recv = [{} for _ in range(L)]
    sg = {n: [None] * L for n in WEIGHTS if n not in BIG and n != 'rel_bias'}
    dbands = [None] * L

    def wgrad(n, i, a_, b_, **kw):
        gbuf[i][n] = _mm_tn(a_, b_, name=f"d_{n}_{i}", **kw)

    def with_scatter(fn, host, i, *args, **kw):
        names = SCATTER_HOSTS[host]
        rider = _Exchange("scatter", [gbuf[i][n] for n in names], is_col(names)) if names else None
        out = fn(*args, rider=rider, **kw)
        if rider is not None:
            recv[i].update(zip(names, rider.result))
        return out

    def ffn_bwd(k, i, dh_out, s, h_in):
        dgu, df, sg[f'ffn{k}_norm_post'][i] = with_scatter(
            _ffn_down_bwd, f'ffn{k}_down_bwd', i, dh_out, s[f'f{k}'], vec(f'ffn{k}_norm_post', i), full[i][f'ffn{k}_w_down'],
            s[f'gu{k}'], name=f"ffn{k}_down_bwd_{i}")
        wgrad(f'ffn{k}_w_down', i, s[f'gu{k}'], df, swiglu=True)
        wgrad(f'ffn{k}_w_gu', i, dgu, s[f'n{k}'])
        dh_in, sg[f'ffn{k}_norm_pre'][i] = with_scatter(
            _mm_rmsbwd, f'ffn{k}_up_bwd', i, dgu, wgu(k, i), dh_out, h_in, vec(f'ffn{k}_norm_pre', i), nt=False,
            name=f"ffn{k}_up_bwd_{i}")
        return dh_in

    for i in reversed(range(L)):
        s, q = saved[i], lay(i)
        dh, de, dpgl, npg, sg['ple_norm_gate'][i], sg['ple_norm_post'][i] = _ple_bwd(
            dh, s['h3'], q['pe'], vec('ple_norm_gate', i), full[i]['w_ple_gate'], full[i]['w_ple'], vec('ple_norm_post', i),
            name=f"ple_bwd_{i}")
        wgrad('w_ple', i, q['pe'][None], de)
        wgrad('w_ple_gate', i, npg[None], dpgl)
        dh = ffn_bwd(2, i, dh, s, s['h2'])
        do, dz, *dys, dgates, sg['mix_norm_post'][i], dbg = _merge_out_bwd(
            dh, s['o'], vec('mix_norm_post', i), full[i]['w_o'], s['ys'], s['proj'], q['bgate'], q['wbrs'],
            name=f"merge_bwd_{i}")
        sg['b_gate'][i] = dbg.reshape(-1)
        wgrad('w_o', i, s['merged'][None], do)
        for b, n in enumerate(('w_br_a', 'w_br_b', 'w_br_c', 'w_br_d')):
            wgrad(n, i, s['ys'][b][None], dz, b_plane=b)
        dy3 = [d.reshape(Bn, S, d.shape[-1]) for d in dys]
        da, daf, dbf = with_scatter(_fox_bwd, 'fox_bwd', i, s['proj3'], s['af3'], q['bfor'], dy3[0], off,
                                    name=f"fox_bwd_{i}")
        sg['b_forget'][i] = dbf[0, :A_HEADS]
        db, sg['conv_short'][i] = _sconv_bwd(s['proj3'], q['cs'], dy3[1], off, name=f"sconv_bwd_{i}")
        dc, sg['conv_dw'][i], sg['conv_dw_bias'][i], sg['conv_ln_gain'][i], sg['conv_ln_bias'][i] = _cconv_bwd(
            s['proj3'], q['cdw'], q['cb'], q['lg'], q['lb'], dy3[2], off, name=f"cconv_bwd_{i}")
        dd, dbands[i], dsk = _swa_bwd(s['proj3'], band_bias, q['sinks'], dy3[3], off, name=f"swa_bwd_{i}")
        sg['attn_sinks'][i] = dsk[:, 0]
        dproj = _assemble([dgates] + [t.reshape(T, t.shape[-1]) for t in (da, db, dc, dd, daf)], name=f"dproj_{i}")
        wgrad('w_in', i, s['u'][None], dproj)
        dh, sg['mix_norm_pre'][i] = with_scatter(
            _mm_rmsbwd, 'proj_bwd', i, dproj[None], full[i]['w_in'][None], dh, s['h1'], vec('mix_norm_pre', i), nt=True,
            name=f"proj_bwd_{i}")
        dh = ffn_bwd(1, i, dh, s, s['h0'])
    grad_x = dh.reshape(Bn, S, D)

    d_rel = _relbias_grad(jnp.stack(dbands).reshape(L, D_Q_HEADS, -1), onehot, name="relbias_grad")
    small_g = [d_rel if n == 'rel_bias' else jnp.stack(sg[n]).reshape(a[n].shape) for n in SMALL]
    n_small_rows = _pack(small_g).shape[0]
    conv_g = [jnp.stack(sg[n]) for n in CONV_SHARDED]
    red = _all_reduce(jnp.concatenate([_pack(small_g), _pack(conv_g)]), name="reduce_small")
    g_small_buf = red[:n_small_rows]
    conv_gfull = _unpack(red[n_small_rows:], conv_shapes)
    conv_gloc = [lax.dynamic_slice_in_dim(g, me * cw, cw, axis=2) for g in conv_gfull]

    grads, deltas, new_m, new_v = {}, {}, {}, {}
    small_shapes = [a[n].shape for n in SMALL]
    res = _adamw(g_small_buf, *[_pack([a[pre + n] for n in SMALL]) for pre in ('', 'm_', 'v_')], name="adamw_small")
    for dst, buf in zip((grads, deltas, new_m, new_v), (g_small_buf,) + tuple(res)):
        dst.update(zip(SMALL, _unpack(buf, small_shapes)))
    loc_shapes = [a[n].shape for n in CONV_SHARDED]
    g_conv_buf = _pack(conv_gloc)
    res = _adamw(g_conv_buf, *[_pack([a[pre + n] for n in CONV_SHARDED]) for pre in ('', 'm_', 'v_')], name="adamw_conv")
    for dst, buf in zip((grads, deltas, new_m, new_v), (g_conv_buf,) + tuple(res)):
        dst.update(zip(CONV_SHARDED, _unpack(buf, loc_shapes)))

    for n in BIG:
        g = _sum8([recv[l][n] for l in range(L)], name=f"sum_{n}")
        if n in ('ffn1_w_gu', 'ffn2_w_gu'):
            g = jnp.swapaxes(g, 1, 2)
        elif n == 'w_in':
            g = _unpermute_in(g, D)
        C = g.shape[-1]
        res = _adamw(g.reshape(-1, C), *[a[pre + n].reshape(-1, C) for pre in ('', 'm_', 'v_')], name=f"adamw_{n}")
        grads[n] = g
        deltas[n], new_m[n], new_v[n] = [t.reshape(g.shape) for t in res]

    return (loss, grad_x, *[grads[n] for n in WEIGHTS], *[deltas[n] for n in WEIGHTS],
            *[new_m[n] for n in WEIGHTS], *[new_v[n] for n in WEIGHTS])
```

```python
import functools
import math

import jax
import jax.numpy as jnp
import numpy as np
from jax import lax
from jax.experimental import pallas as pl
from jax.experimental.pallas import tpu as pltpu

F32 = jnp.float32
BF16 = jnp.bfloat16

EPS = 1e-6
NEG_INF = -1e30
HEAD_DIM = 64
A_HEADS = 4
A_WIDTH = A_HEADS * HEAD_DIM
B_WIDTH = 256
C_WIDTH = 256
SHORT_CONV = 3
CONF_CONV = 31
D_Q_HEADS = 8
D_KV_HEADS = 2
D_GROUP = D_Q_HEADS // D_KV_HEADS
D_WIDTH = D_Q_HEADS * HEAD_DIM
D_KV_WIDTH = D_KV_HEADS * HEAD_DIM
WINDOW = 128
Q_BLOCK = 128
N_BRANCH = 4
REL_BUCKETS = 32
REL_MAX_DIST = 128
AF_PAD = 256
AF_LANES = 128

ADAM_LR = 0.001
ADAM_B1 = 0.9
ADAM_B2 = 0.999
ADAM_EPS = 1e-08
ADAM_WD = 0.01
ADAM_STEP = 10

N_DEV = 8
MESH_AXES = ("x", "y", "c")
VMEM_LIMIT_V7X = 56 * 2**20
MESH = pl.DeviceIdType.MESH

WEIGHTS = ['ffn1_norm_pre', 'ffn1_w_gu', 'ffn1_w_down', 'ffn1_norm_post', 'mix_norm_pre', 'w_in', 'b_forget',
           'b_gate', 'conv_short', 'conv_dw', 'conv_dw_bias', 'conv_ln_gain', 'conv_ln_bias', 'attn_sinks',
           'rel_bias', 'w_br_a', 'w_br_b', 'w_br_c', 'w_br_d', 'w_o', 'mix_norm_post', 'ffn2_norm_pre',
           'ffn2_w_gu', 'ffn2_w_down', 'ffn2_norm_post', 'ple_norm_gate', 'w_ple_gate', 'w_ple', 'ple_norm_post']
ARG_NAMES = ['x', 'p'] + WEIGHTS + ['loss_target'] + ['m_' + n for n in WEIGHTS] + ['v_' + n for n in WEIGHTS]
BIG = ['ffn1_w_gu', 'ffn1_w_down', 'w_in', 'w_br_a', 'w_br_b', 'w_br_c', 'w_br_d', 'w_o', 'ffn2_w_gu',
       'ffn2_w_down', 'w_ple_gate', 'w_ple']
COL_SHARDED = ('w_br_a', 'w_br_b', 'w_br_c', 'w_br_d', 'w_ple')
CONV_SHARDED = ('conv_short', 'conv_dw')
SMALL = [n for n in WEIGHTS if n not in BIG and n not in CONV_SHARDED]


def _tile(n, *prefs):
    for t in prefs:
        if n % t == 0:
            return t
    return n


def _params(*sem):
    return pltpu.CompilerParams(dimension_semantics=sem, vmem_limit_bytes=VMEM_LIMIT_V7X)


def _dot(a, b):
    return jnp.dot(a, b, preferred_element_type=F32)


def _dot_nt(a, b):
    return lax.dot_general(a, b, (((1,), (1,)), ((), ())), preferred_element_type=F32)


def _dot_tn(a, b):
    return lax.dot_general(a, b, (((0,), (0,)), ((), ())), preferred_element_type=F32)


def _rstd(x):
    return lax.rsqrt(jnp.mean(x * x, axis=-1, keepdims=True) + EPS)


def _rms_bwd(dy, x, r, g):
    xh = x * r
    dxh = dy * g
    dx = r * (dxh - xh * jnp.mean(dxh * xh, axis=-1, keepdims=True))
    return dx, dy * xh


def _colsum(v):
    return jnp.sum(v, axis=0, keepdims=True)


def _my_id():
    return lax.axis_index("x") * 4 + lax.axis_index("y") * 2 + lax.axis_index("c")


def _peer(k):
    coords = []
    for bit, axis in zip((4, 2, 1), MESH_AXES):
        me = lax.axis_index(axis)
        coords.append(1 - me if k & bit else me)
    return tuple(coords), coords[0] * 4 + coords[1] * 2 + coords[2]


def _window(ref, col, d, size):
    start = pl.multiple_of(d * size, 8)
    return ref.at[:, pl.ds(start, size)] if col else ref.at[pl.ds(start, size), :]


ANY = pl.BlockSpec(memory_space=pl.ANY)


class _Exchange:
    def __init__(self, kind, arrays, cols):
        self.kind, self.arrays, self.cols = kind, list(arrays), list(cols)
        n = len(self.arrays)
        if kind == "gather":
            self.sizes = [a.shape[1] if c else a.shape[0] for a, c in zip(self.arrays, cols)]
            self.out_shape = [jax.ShapeDtypeStruct((a.shape[0], a.shape[1] * N_DEV) if c else (a.shape[0] * N_DEV, a.shape[1]),
                                                   a.dtype) for a, c in zip(self.arrays, cols)]
        else:
            self.sizes = [a.shape[1] // N_DEV if c else a.shape[0] // N_DEV for a, c in zip(self.arrays, cols)]
            self.out_shape = [jax.ShapeDtypeStruct((N_DEV, a.shape[0], s) if c else (N_DEV, s, a.shape[1]), a.dtype)
                              for a, c, s in zip(self.arrays, cols, self.sizes)]
        self.scratch = [pltpu.SemaphoreType.DMA((n, N_DEV - 1)), pltpu.SemaphoreType.DMA((n, N_DEV - 1)),
                        pltpu.SemaphoreType.DMA((n,))]
        self.result = None

    def _src(self, ins, w, d):
        return ins[w] if self.kind == "gather" else _window(ins[w], self.cols[w], d, self.sizes[w])

    def _dst(self, outs, w, d):
        return _window(outs[w], self.cols[w], d, self.sizes[w]) if self.kind == "gather" else outs[w].at[d]

    def _copies(self, ins, outs, send, recv, loc):
        me = _my_id()
        n = len(self.arrays)
        local = [pltpu.make_async_copy(self._src(ins, w, me), self._dst(outs, w, me), loc.at[w]) for w in range(n)]
        sends, arrivals = [], []
        for k in range(1, N_DEV):
            peer, peer_id = _peer(k)
            for w in range(n):
                sems = dict(send_sem=send.at[w, k - 1], recv_sem=recv.at[w, k - 1], device_id=peer, device_id_type=MESH)
                sends.append(pltpu.make_async_remote_copy(src_ref=self._src(ins, w, peer_id), dst_ref=self._dst(outs, w, me), **sems))
                arrivals.append(pltpu.make_async_remote_copy(src_ref=self._src(ins, w, me), dst_ref=self._dst(outs, w, peer_id), **sems))
        return local, sends, arrivals

    def start(self, ins, outs, send, recv, loc):
        local, sends, _ = self._copies(ins, outs, send, recv, loc)
        for cp in local + sends:
            cp.start()

    def wait(self, ins, outs, send, recv, loc):
        local, sends, arrivals = self._copies(ins, outs, send, recv, loc)
        for cp in arrivals:
            cp.wait_recv()
        for cp in sends:
            cp.wait_send()
        for cp in local:
            cp.wait()

    def run_alone(self, name):
        n = len(self.arrays)

        def body(*refs):
            self.start(refs[:n], refs[n:2 * n], *refs[2 * n:])
            self.wait(refs[:n], refs[n:2 * n], *refs[2 * n:])

        self.result = pl.pallas_call(body, name=name, in_specs=[ANY] * n, out_specs=[ANY] * n, out_shape=self.out_shape,
                                     scratch_shapes=self.scratch)(*self.arrays)
        return self.result


def _call(body, args, *, name, grid, in_specs, out_specs, out_shape, scratch_shapes=(), sem, rider=None):
    if rider is None:
        return pl.pallas_call(body, name=name, grid=grid, in_specs=in_specs, out_specs=out_specs, out_shape=out_shape,
                              scratch_shapes=list(scratch_shapes), compiler_params=_params(*sem))(*args)
    n_in, n_out, n_scr, n_r = len(in_specs), len(out_shape), len(scratch_shapes), len(rider.arrays)

    def both(*refs):
        ins, r_in = refs[:n_in], refs[n_in:n_in + n_r]
        outs, r_out = refs[n_in + n_r:n_in + n_r + n_out], refs[n_in + n_r + n_out:n_in + 2 * n_r + n_out]
        scr, sems = refs[n_in + 2 * n_r + n_out:n_in + 2 * n_r + n_out + n_scr], refs[n_in + 2 * n_r + n_out + n_scr:]
        first = functools.reduce(lambda p, q: p & q, [pl.program_id(d) == 0 for d in range(len(grid))])
        last = functools.reduce(lambda p, q: p & q, [pl.program_id(d) == g - 1 for d, g in enumerate(grid)])

        @pl.when(first)
        def _():
            rider.start(r_in, r_out, *sems)

        body(*ins, *outs, *scr)

        @pl.when(last)
        def _():
            rider.wait(r_in, r_out, *sems)

    res = pl.pallas_call(
        both, name=name, grid=grid, in_specs=list(in_specs) + [ANY] * n_r, out_specs=list(out_specs) + [ANY] * n_r,
        out_shape=list(out_shape) + rider.out_shape, scratch_shapes=list(scratch_shapes) + rider.scratch,
        compiler_params=_params(*(("arbitrary",) * len(grid))))(*args, *rider.arrays)
    rider.result = res[n_out:]
    return res[:n_out]


def _rms_mm(h, g, w, *, nt, out_dtype, save_n, name, rider=None):
    T, D = h.shape
    P = w.shape[0]
    N = w.shape[1] if nt else w.shape[2]
    tm = _tile(T, 1024, 512, 256, 128)
    tn = _tile(N, 2816, 1792, 2048, 1408, 1024, 512, 256, 128)

    def body(h_ref, g_ref, w_ref, y_ref, *rest):
        n_scr = rest[-1]

        @pl.when((pl.program_id(1) == 0) & (pl.program_id(2) == 0))
        def _():
            x = h_ref[...]
            n = (x * _rstd(x) * g_ref[...]).astype(BF16)
            n_scr[...] = n
            if save_n:
                rest[0][...] = n

        wt = w_ref[...]
        y = _dot_nt(n_scr[...], wt) if nt else _dot(n_scr[...], wt)
        y_ref[...] = y.astype(out_dtype)

    w_spec = (pl.BlockSpec((None, tn, D), lambda i, p, j: (p, j, 0)) if nt
              else pl.BlockSpec((None, D, tn), lambda i, p, j: (p, 0, j)))
    out_shape = [jax.ShapeDtypeStruct((P, T, N), out_dtype)]
    out_specs = [pl.BlockSpec((None, tm, tn), lambda i, p, j: (p, i, j))]
    if save_n:
        out_shape.append(jax.ShapeDtypeStruct((T, D), BF16))
        out_specs.append(pl.BlockSpec((tm, D), lambda i, p, j: (i, 0)))
    res = _call(
        body, (h, g, w), name=name, grid=(T // tm, P, N // tn),
        in_specs=[pl.BlockSpec((tm, D), lambda i, p, j: (i, 0)), pl.BlockSpec((1, D), lambda i, p, j: (0, 0)), w_spec],
        out_specs=out_specs, out_shape=out_shape, scratch_shapes=[pltpu.VMEM((tm, D), BF16)],
        sem=("parallel", "arbitrary", "arbitrary"), rider=rider)
    return res if save_n else res[0]


def _ffn_down(gu, wd, h, gpost, *, name, rider=None):
    _, T, F = gu.shape
    D = wd.shape[1]
    tm = _tile(T, 512, 256, 128)
    tk = _tile(F, 2816, 1408, 1024, 512, 256, 128)
    nk = F // tk

    def body(g_ref, u_ref, wd_ref, h_ref, gp_ref, hn_ref, f_ref, acc):
        k = pl.program_id(1)

        @pl.when(k == 0)
        def _():
            acc[...] = jnp.zeros_like(acc)

        gt = g_ref[...].astype(F32)
        a = (gt * jax.nn.sigmoid(gt) * u_ref[...].astype(F32)).astype(BF16)
        acc[...] += _dot(a, wd_ref[...])

        @pl.when(k == nk - 1)
        def _():
            f = acc[...]
            f_ref[...] = f
            hn_ref[...] = h_ref[...] + 0.5 * (f * _rstd(f) * gp_ref[...])

    return _call(
        body, (gu, gu, wd, h, gpost), name=name, grid=(T // tm, nk),
        in_specs=[pl.BlockSpec((None, tm, tk), lambda i, k: (0, i, k)), pl.BlockSpec((None, tm, tk), lambda i, k: (1, i, k)),
                  pl.BlockSpec((tk, D), lambda i, k: (k, 0)), pl.BlockSpec((tm, D), lambda i, k: (i, 0)),
                  pl.BlockSpec((1, D), lambda i, k: (0, 0))],
        out_specs=[pl.BlockSpec((tm, D), lambda i, k: (i, 0)), pl.BlockSpec((tm, D), lambda i, k: (i, 0))],
        out_shape=[jax.ShapeDtypeStruct((T, D), F32), jax.ShapeDtypeStruct((T, D), F32)],
        scratch_shapes=[pltpu.VMEM((tm, D), F32)], sem=("parallel", "arbitrary"), rider=rider)


def _ffn_down_bwd(dh, f, gpost, wd, gu, *, name, rider=None):
    _, T, F = gu.shape
    D = wd.shape[1]
    tm = _tile(T, 512, 256, 128)
    tn = _tile(F, 1408, 1024, 512, 256, 128)

    def body(dh_ref, f_ref, gp_ref, wd_ref, g_ref, u_ref, dgu_ref, df_ref, dgp_ref, df_scr):
        i, j = pl.program_id(0), pl.program_id(1)

        @pl.when((i == 0) & (j == 0))
        def _():
            dgp_ref[...] = jnp.zeros_like(dgp_ref)

        @pl.when(j == 0)
        def _():
            x = f_ref[...]
            dx, dgn = _rms_bwd(0.5 * dh_ref[...], x, _rstd(x), gp_ref[...])
            dgp_ref[...] += _colsum(dgn)
            df = dx.astype(BF16)
            df_scr[...] = df
            df_ref[...] = df

        dact = _dot_nt(df_scr[...], wd_ref[...])
        gt = g_ref[...].astype(F32)
        ut = u_ref[...].astype(F32)
        sg = jax.nn.sigmoid(gt)
        dgu_ref[0] = (dact * ut * (sg * (1.0 + gt * (1.0 - sg)))).astype(BF16)
        dgu_ref[1] = (dact * (gt * sg)).astype(BF16)

    return _call(
        body, (dh, f, gpost, wd, gu, gu), name=name, grid=(T // tm, F // tn),
        in_specs=[pl.BlockSpec((tm, D), lambda i, j: (i, 0)), pl.BlockSpec((tm, D), lambda i, j: (i, 0)),
                  pl.BlockSpec((1, D), lambda i, j: (0, 0)), pl.BlockSpec((tn, D), lambda i, j: (j, 0)),
                  pl.BlockSpec((None, tm, tn), lambda i, j: (0, i, j)), pl.BlockSpec((None, tm, tn), lambda i, j: (1, i, j))],
        out_specs=[pl.BlockSpec((2, tm, tn), lambda i, j: (0, i, j)), pl.BlockSpec((tm, D), lambda i, j: (i, 0)),
                   pl.BlockSpec((1, D), lambda i, j: (0, 0))],
        out_shape=[jax.ShapeDtypeStruct((2, T, F), BF16), jax.ShapeDtypeStruct((T, D), BF16),
                   jax.ShapeDtypeStruct((1, D), F32)],
        scratch_shapes=[pltpu.VMEM((tm, D), BF16)], sem=("arbitrary", "arbitrary"), rider=rider)


def _mm_rmsbwd(a, b, dh_in, h, g, *, nt, name, rider=None):
    P, T, K = a.shape
    D = h.shape[1]
    tm = _tile(T, 1024, 512, 256, 128)
    tk = _tile(K, 1792, 1408, 2048, 1024, 512, 256, 128)
    nk = K // tk

    def body(a_ref, b_ref, dh_ref, h_ref, g_ref, out_ref, dg_ref, acc):
        i, p, k = pl.program_id(0), pl.program_id(1), pl.program_id(2)

        @pl.when((i == 0) & (p == 0) & (k == 0))
        def _():
            dg_ref[...] = jnp.zeros_like(dg_ref)

        @pl.when((p == 0) & (k == 0))
        def _():
            acc[...] = jnp.zeros_like(acc)

        acc[...] += _dot_nt(a_ref[...], b_ref[...]) if nt else _dot(a_ref[...], b_ref[...])

        @pl.when((p == P - 1) & (k == nk - 1))
        def _():
            x = h_ref[...]
            dx, dgn = _rms_bwd(acc[...], x, _rstd(x), g_ref[...])
            dg_ref[...] += _colsum(dgn)
            out_ref[...] = dh_ref[...] + dx

    b_spec = (pl.BlockSpec((None, D, tk), lambda i, p, k: (p, 0, k)) if nt
              else pl.BlockSpec((None, tk, D), lambda i, p, k: (p, k, 0)))
    return _call(
        body, (a, b, dh_in, h, g), name=name, grid=(T // tm, P, nk),
        in_specs=[pl.BlockSpec((None, tm, tk), lambda i, p, k: (p, i, k)), b_spec,
                  pl.BlockSpec((tm, D), lambda i, p, k: (i, 0)), pl.BlockSpec((tm, D), lambda i, p, k: (i, 0)),
                  pl.BlockSpec((1, D), lambda i, p, k: (0, 0))],
        out_specs=[pl.BlockSpec((tm, D), lambda i, p, k: (i, 0)), pl.BlockSpec((1, D), lambda i, p, k: (0, 0))],
        out_shape=[jax.ShapeDtypeStruct((T, D), F32), jax.ShapeDtypeStruct((1, D), F32)],
        scratch_shapes=[pltpu.VMEM((tm, D), F32)], sem=("arbitrary", "arbitrary", "arbitrary"), rider=rider)


def _mm_tn(a, b, *, swiglu=False, b_plane=0, name):
    T, N = b.shape[-2:]
    K = a.shape[2]
    P = 1 if swiglu else a.shape[0]
    tk = _tile(K, 2816, 1024, 512, 256, 128)
    tn = _tile(N, 1792, 1024, 512, 256, 128)
    tt = _tile(T, 512, 256, 128)
    nt_ = T // tt
    nkb = K // tk

    def body(*refs):
        if swiglu:
            g_ref, u_ref, b_ref = refs[:3]
        else:
            a_ref, b_ref = refs[:2]
        out_ref, acc = refs[-2], refs[-1]
        t = pl.program_id(3)

        @pl.when(t == 0)
        def _():
            acc[...] = jnp.zeros_like(acc)

        if swiglu:
            gt = g_ref[...].astype(F32)
            at = (gt * jax.nn.sigmoid(gt) * u_ref[...].astype(F32)).astype(BF16)
        else:
            at = a_ref[...].astype(BF16)
        acc[...] += _dot_tn(at, b_ref[...].astype(BF16))

        @pl.when(t == nt_ - 1)
        def _():
            out_ref[...] = acc[...].astype(BF16)

    if swiglu:
        a_specs = [pl.BlockSpec((None, tt, tk), lambda p, i, j, t: (0, t, i)),
                   pl.BlockSpec((None, tt, tk), lambda p, i, j, t: (1, t, i))]
        a_args = [a, a]
    else:
        a_specs = [pl.BlockSpec((None, tt, tk), lambda p, i, j, t: (p, t, i))]
        a_args = [a]
    if b.ndim == 3:
        in_specs = a_specs + [pl.BlockSpec((None, tt, tn), lambda p, i, j, t: (b_plane, t, j))]
    else:
        in_specs = a_specs + [pl.BlockSpec((tt, tn), lambda p, i, j, t: (t, j))]
    return pl.pallas_call(
        body, name=name, grid=(P, nkb, N // tn, nt_),
        in_specs=in_specs,
        out_specs=pl.BlockSpec((tk, tn), lambda p, i, j, t: (p * nkb + i, j)),
        out_shape=jax.ShapeDtypeStruct((P * K, N), BF16),
        scratch_shapes=[pltpu.VMEM((tk, tn), F32)],
        compiler_params=_params("parallel", "parallel", "parallel", "arbitrary"),
    )(*a_args, b)


def _row(D):
    return pl.BlockSpec((1, D), lambda i: (0, 0))


def _full(shape):
    return pl.BlockSpec(shape, lambda i: (0,) * len(shape))


def _merge_out(ys, proj, bgate, wbrs, wo, h, gpost, *, name):
    T, D = h.shape
    tm = _tile(T, 512, 256, 128)

    def body(ya, yb, yc, yd, g0, g1, g2, g3, bg_ref, wa, wb, wc, wd_, wo_ref, h_ref, gp_ref, hn_ref, o_ref, mg_ref):
        merged = jnp.zeros((tm, D), F32)
        for b, (y_ref, gt_ref, w_ref) in enumerate(zip((ya, yb, yc, yd), (g0, g1, g2, g3), (wa, wb, wc, wd_))):
            gate = jax.nn.sigmoid(gt_ref[...].astype(F32) + bg_ref[b:b + 1, :])
            merged = merged + gate * _dot(y_ref[...], w_ref[...])
        mb = merged.astype(BF16)
        mg_ref[...] = mb
        o = _dot(mb, wo_ref[...])
        o_ref[...] = o
        hn_ref[...] = h_ref[...] + o * _rstd(o) * gp_ref[...]

    tok = lambda w: pl.BlockSpec((tm, w), lambda i: (i, 0))
    gate_specs = [pl.BlockSpec((tm, D), lambda i, b=b: (i, b)) for b in range(N_BRANCH)]
    return pl.pallas_call(
        body, name=name, grid=(T // tm,),
        in_specs=[tok(A_WIDTH), tok(B_WIDTH), tok(C_WIDTH), tok(D_WIDTH)] + gate_specs
        + [_full((N_BRANCH, D))] + [_full(w.shape) for w in wbrs] + [_full((D, D)), tok(D), _row(D)],
        out_specs=[tok(D), tok(D), tok(D)],
        out_shape=[jax.ShapeDtypeStruct((T, D), F32), jax.ShapeDtypeStruct((T, D), F32), jax.ShapeDtypeStruct((T, D), BF16)],
        compiler_params=_params("parallel"),
    )(*ys, proj, proj, proj, proj, bgate, *wbrs, wo, h, gpost)


def _merge_out_bwd(dh, o, gpost, wo, ys, proj, bgate, wbrs, *, name):
    T, D = o.shape
    tm = _tile(T, 256, 128)
    widths = (A_WIDTH, B_WIDTH, C_WIDTH, D_WIDTH)

    def body(dh_ref, o_ref, gp_ref, wo_ref, ya, yb, yc, yd, g0, g1, g2, g3, bg_ref, wa, wb, wc, wd_,
             do_ref, dz_ref, dya, dyb, dyc, dyd, dgt_ref, dgp_ref, dbg_ref):
        @pl.when(pl.program_id(0) == 0)
        def _():
            dgp_ref[...] = jnp.zeros_like(dgp_ref)
            dbg_ref[...] = jnp.zeros_like(dbg_ref)

        x = o_ref[...]
        do, dgn = _rms_bwd(dh_ref[...], x, _rstd(x), gp_ref[...])
        dgp_ref[...] += _colsum(dgn)
        dob = do.astype(BF16)
        do_ref[...] = dob
        dmerged = _dot_nt(dob, wo_ref[...])
        for b, (y_ref, gt_ref, w_ref, dy_ref) in enumerate(zip((ya, yb, yc, yd), (g0, g1, g2, g3), (wa, wb, wc, wd_),
                                                               (dya, dyb, dyc, dyd))):
            gate = jax.nn.sigmoid(gt_ref[...].astype(F32) + bg_ref[b:b + 1, :])
            z = _dot(y_ref[...], w_ref[...])
            dz = (dmerged * gate).astype(BF16)
            dz_ref[b] = dz
            dy_ref[...] = _dot_nt(dz, w_ref[...]).astype(BF16)
            dgate = dmerged * z * gate * (1.0 - gate)
            dgt_ref[:, b * D:(b + 1) * D] = dgate.astype(BF16)
            dbg_ref[b:b + 1, :] += _colsum(dgate)

    tok = lambda w: pl.BlockSpec((tm, w), lambda i: (i, 0))
    gate_specs = [pl.BlockSpec((tm, D), lambda i, b=b: (i, b)) for b in range(N_BRANCH)]
    return pl.pallas_call(
        body, name=name, grid=(T // tm,),
        in_specs=[tok(D), tok(D), _row(D), _full((D, D))] + [tok(w) for w in widths] + gate_specs
        + [_full((N_BRANCH, D))] + [_full(w.shape) for w in wbrs],
        out_specs=[tok(D), pl.BlockSpec((N_BRANCH, tm, D), lambda i: (0, i, 0))] + [tok(w) for w in widths]
        + [tok(N_BRANCH * D), _row(D), _full((N_BRANCH, D))],
        out_shape=[jax.ShapeDtypeStruct((T, D), BF16), jax.ShapeDtypeStruct((N_BRANCH, T, D), BF16)]
        + [jax.ShapeDtypeStruct((T, w), BF16) for w in widths]
        + [jax.ShapeDtypeStruct((T, N_BRANCH * D), BF16), jax.ShapeDtypeStruct((1, D), F32),
           jax.ShapeDtypeStruct((N_BRANCH, D), F32)],
        compiler_params=_params("arbitrary"),
    )(dh, o, gpost, wo, *ys, proj, proj, proj, proj, bgate, *wbrs)


def _ple(h, pe, ggate, wpg, wple, gpost, *, name):
    T, D = h.shape
    E = pe.shape[1]
    tm = _tile(T, 512, 256, 128)

    def body(h_ref, p_ref, gg_ref, wpg_ref, wple_ref, gp_ref, out_ref):
        x = h_ref[...]
        n = (x * _rstd(x) * gg_ref[...]).astype(BF16)
        pg = jax.nn.sigmoid(_dot(n, wpg_ref[...]))
        e = _dot(p_ref[...].astype(BF16), wple_ref[...])
        out_ref[...] = x + pg * (e * _rstd(e) * gp_ref[...])

    tok = lambda w: pl.BlockSpec((tm, w), lambda i: (i, 0))
    return pl.pallas_call(
        body, name=name, grid=(T // tm,),
        in_specs=[tok(D), tok(E), _row(D), _full((D, D)), _full((E, D)), _row(D)],
        out_specs=tok(D), out_shape=jax.ShapeDtypeStruct((T, D), F32),
        compiler_params=_params("parallel"),
    )(h, pe, ggate, wpg, wple, gpost)


def _ple_bwd(dh, h, pe, ggate, wpg, wple, gpost, *, name):
    T, D = h.shape
    E = pe.shape[1]
    tm = _tile(T, 256, 128)

    def body(dh_ref, h_ref, p_ref, gg_ref, wpg_ref, wple_ref, gp_ref, dhi_ref, de_ref, dpgl_ref, n_ref, dgg_ref, dgp_ref):
        @pl.when(pl.program_id(0) == 0)
        def _():
            dgg_ref[...] = jnp.zeros_like(dgg_ref)
            dgp_ref[...] = jnp.zeros_like(dgp_ref)

        dh = dh_ref[...]
        x = h_ref[...]
        r = _rstd(x)
        n = (x * r * gg_ref[...]).astype(BF16)
        n_ref[...] = n
        pg = jax.nn.sigmoid(_dot(n, wpg_ref[...]))
        e = _dot(p_ref[...].astype(BF16), wple_ref[...])
        re = _rstd(e)
        de, dgn = _rms_bwd(dh * pg, e, re, gp_ref[...])
        dgp_ref[...] += _colsum(dgn)
        de_ref[...] = de.astype(BF16)
        dpgl = (dh * (e * re * gp_ref[...]) * pg * (1.0 - pg)).astype(BF16)
        dpgl_ref[...] = dpgl
        dn = _dot_nt(dpgl, wpg_ref[...])
        dx, dgn2 = _rms_bwd(dn, x, r, gg_ref[...])
        dgg_ref[...] += _colsum(dgn2)
        dhi_ref[...] = dh + dx

    tok = lambda w: pl.BlockSpec((tm, w), lambda i: (i, 0))
    return pl.pallas_call(
        body, name=name, grid=(T // tm,),
        in_specs=[tok(D), tok(D), tok(E), _row(D), _full((D, D)), _full((E, D)), _row(D)],
        out_specs=[tok(D), tok(D), tok(D), tok(D), _row(D), _row(D)],
        out_shape=[jax.ShapeDtypeStruct((T, D), F32), jax.ShapeDtypeStruct((T, D), BF16), jax.ShapeDtypeStruct((T, D), BF16),
                   jax.ShapeDtypeStruct((T, D), BF16), jax.ShapeDtypeStruct((1, D), F32), jax.ShapeDtypeStruct((1, D), F32)],
        compiler_params=_params("arbitrary"),
    )(dh, h, pe, ggate, wpg, wple, gpost)


def _loss_head(y, target, *, name):
    T, D = y.shape
    tm = _tile(T, 512, 256, 128)

    def body(y_ref, t_ref, l_ref, dy_ref):
        @pl.when(pl.program_id(0) == 0)
        def _():
            l_ref[...] = jnp.zeros_like(l_ref)

        err = y_ref[...] - t_ref[...]
        dy_ref[...] = err / D
        l_ref[...] += 0.5 * jnp.sum(jnp.mean(err * err, axis=-1, keepdims=True), axis=0, keepdims=True)

    tok = pl.BlockSpec((tm, D), lambda i: (i, 0))
    return pl.pallas_call(
        body, name=name, grid=(T // tm,),
        in_specs=[tok, tok], out_specs=[_full((8, 128)), tok],
        out_shape=[jax.ShapeDtypeStruct((8, 128), F32), jax.ShapeDtypeStruct((T, D), F32)],
        compiler_params=_params("arbitrary"),
    )(y, target)


def _layout(D):
    off = {'gates': 0}
    off['a'] = N_BRANCH * D
    off['b'] = off['a'] + 3 * A_WIDTH
    off['c'] = off['b'] + 3 * B_WIDTH
    off['d'] = off['c'] + 2 * C_WIDTH
    off['af'] = off['d'] + D_WIDTH + 2 * D_KV_WIDTH
    off['end'] = off['af'] + AF_PAD
    return off


def _seq_spec(S, width, col):
    assert col % width == 0
    return pl.BlockSpec((None, S, width), lambda b: (b, 0, col // width))


def _split3(x):
    hi = x.astype(BF16)
    r1 = x - hi.astype(F32)
    mid = r1.astype(BF16)
    lo = (r1 - mid.astype(F32)).astype(BF16)
    return hi, mid, lo


def _fox_cumsum(af_ref, bf_ref, c_scr, ct_scr):
    S = af_ref.shape[0]
    cb = _tile(S, 256, 128)
    tril = (lax.broadcasted_iota(jnp.int32, (cb, cb), 0) >= lax.broadcasted_iota(jnp.int32, (cb, cb), 1)).astype(BF16)
    carry = jnp.zeros((1, AF_LANES), F32)
    for j in range(S // cb):
        rows = slice(j * cb, (j + 1) * cb)
        hi, mid, lo = _split3(jax.nn.log_sigmoid(af_ref[rows, :] + bf_ref[...]))
        cblk = _dot(tril, hi) + _dot(tril, mid) + _dot(tril, lo) + carry
        c_scr[rows, :] = cblk
        carry = cblk[cb - 1:cb, :]
    ct_scr[...] = c_scr[...].T


def _fox_probs(q_ref, k_ref, c_scr, ct_scr, h, i, bq):
    end = (i + 1) * bq
    qs, hs = slice(i * bq, end), slice(HEAD_DIM * h, HEAD_DIM * (h + 1))
    s = _dot_nt(q_ref[qs, hs], k_ref[0:end, hs]) * HEAD_DIM ** -0.5
    s = s + (c_scr[qs, h:h + 1] - ct_scr[h:h + 1, 0:end])
    row = i * bq + lax.broadcasted_iota(jnp.int32, (bq, end), 0)
    col = lax.broadcasted_iota(jnp.int32, (bq, end), 1)
    s = jnp.where(row >= col, s, NEG_INF)
    e = jnp.exp(s - jnp.max(s, axis=-1, keepdims=True))
    return e / jnp.sum(e, axis=-1, keepdims=True)


def _fox_fwd(proj3, af3, bfor, off, *, name, rider=None):
    Bn, S, _ = proj3.shape
    bq = _tile(S, 256, 128)

    def body(q_ref, k_ref, v_ref, af_ref, bf_ref, o_ref, c_scr, ct_scr):
        _fox_cumsum(af_ref, bf_ref, c_scr, ct_scr)
        for h in range(A_HEADS):
            hs = slice(HEAD_DIM * h, HEAD_DIM * (h + 1))
            for i in range(S // bq):
                p = _fox_probs(q_ref, k_ref, c_scr, ct_scr, h, i, bq)
                o_ref[i * bq:(i + 1) * bq, hs] = _dot(p.astype(BF16), v_ref[0:(i + 1) * bq, hs]).astype(BF16)

    return _call(
        body, (proj3, proj3, proj3, af3, bfor), name=name, grid=(Bn,),
        in_specs=[_seq_spec(S, A_WIDTH, off['a']), _seq_spec(S, A_WIDTH, off['a'] + A_WIDTH),
                  _seq_spec(S, A_WIDTH, off['a'] + 2 * A_WIDTH), _seq_spec(S, AF_LANES, 0), _full((1, AF_LANES))],
        out_specs=[_seq_spec(S, A_WIDTH, 0)], out_shape=[jax.ShapeDtypeStruct((Bn, S, A_WIDTH), BF16)],
        scratch_shapes=[pltpu.VMEM((S, AF_LANES), F32), pltpu.VMEM((AF_LANES, S), F32)],
        sem=("parallel",), rider=rider)[0]


def _fox_bwd(proj3, af3, bfor, dya3, off, *, name, rider=None):
    Bn, S, _ = proj3.shape
    bq = _tile(S, 256, 128)
    cb = _tile(S, 256, 128)
    scale = HEAD_DIM ** -0.5

    def body(q_ref, k_ref, v_ref, af_ref, bf_ref, do_ref, dqkv_ref, da_ref, dbf_ref,
             c_scr, ct_scr, dk_scr, dv_scr, dc_scr, dct_scr):
        @pl.when(pl.program_id(0) == 0)
        def _():
            dbf_ref[...] = jnp.zeros_like(dbf_ref)

        _fox_cumsum(af_ref, bf_ref, c_scr, ct_scr)
        dk_scr[...] = jnp.zeros_like(dk_scr)
        dv_scr[...] = jnp.zeros_like(dv_scr)
        dc_scr[...] = jnp.zeros_like(dc_scr)
        dct_scr[...] = jnp.zeros_like(dct_scr)
        for h in range(A_HEADS):
            hs = slice(HEAD_DIM * h, HEAD_DIM * (h + 1))
            for i in range(S // bq):
                end = (i + 1) * bq
                qs = slice(i * bq, end)
                p = _fox_probs(q_ref, k_ref, c_scr, ct_scr, h, i, bq)
                doh = do_ref[qs, hs]
                dp = _dot_nt(doh, v_ref[0:end, hs])
                ds = p * (dp - jnp.sum(p * dp, axis=-1, keepdims=True))
                dsb = ds.astype(BF16)
                dqkv_ref[qs, hs] = (_dot(dsb, k_ref[0:end, hs]) * scale).astype(BF16)
                dk_scr[0:end, hs] += _dot_tn(dsb, q_ref[qs, hs]) * scale
                dv_scr[0:end, hs] += _dot_tn(p.astype(BF16), doh)
                dc_scr[qs, h:h + 1] += jnp.sum(ds, axis=-1, keepdims=True)
                dct_scr[h:h + 1, 0:end] += -jnp.sum(ds, axis=0, keepdims=True)
        dqkv_ref[:, A_WIDTH:2 * A_WIDTH] = dk_scr[...].astype(BF16)
        dqkv_ref[:, 2 * A_WIDTH:3 * A_WIDTH] = dv_scr[...].astype(BF16)
        dc_scr[...] += dct_scr[...].T
        triu = (lax.broadcasted_iota(jnp.int32, (cb, cb), 0) <= lax.broadcasted_iota(jnp.int32, (cb, cb), 1)).astype(BF16)
        carry = jnp.zeros((1, AF_LANES), F32)
        dbf = jnp.zeros((1, AF_LANES), F32)
        for j in reversed(range(S // cb)):
            rows = slice(j * cb, (j + 1) * cb)
            hi, mid, lo = _split3(dc_scr[rows, :])
            dlf = _dot(triu, hi) + _dot(triu, mid) + _dot(triu, lo) + carry
            carry = dlf[0:1, :]
            da = dlf * jax.nn.sigmoid(-(af_ref[rows, :] + bf_ref[...]))
            dbf = dbf + _colsum(da)
            da_ref[rows, 0:AF_LANES] = da.astype(BF16)
        da_ref[:, AF_LANES:AF_PAD] = jnp.zeros((S, AF_PAD - AF_LANES), BF16)
        dbf_ref[...] += dbf

    return _call(
        body, (proj3, proj3, proj3, af3, bfor, dya3), name=name, grid=(Bn,),
        in_specs=[_seq_spec(S, A_WIDTH, off['a']), _seq_spec(S, A_WIDTH, off['a'] + A_WIDTH),
                  _seq_spec(S, A_WIDTH, off['a'] + 2 * A_WIDTH), _seq_spec(S, AF_LANES, 0), _full((1, AF_LANES)),
                  _seq_spec(S, A_WIDTH, 0)],
        out_specs=[_seq_spec(S, 3 * A_WIDTH, 0), _seq_spec(S, AF_PAD, 0), _full((1, AF_LANES))],
        out_shape=[jax.ShapeDtypeStruct((Bn, S, 3 * A_WIDTH), BF16), jax.ShapeDtypeStruct((Bn, S, AF_PAD), BF16),
                   jax.ShapeDtypeStruct((1, AF_LANES), F32)],
        scratch_shapes=[pltpu.VMEM((S, AF_LANES), F32), pltpu.VMEM((AF_LANES, S), F32), pltpu.VMEM((S, A_WIDTH), F32),
                        pltpu.VMEM((S, A_WIDTH), F32), pltpu.VMEM((S, AF_LANES), F32), pltpu.VMEM((AF_LANES, S), F32)],
        sem=("arbitrary",), rider=rider)


def _shift_down(z, s):
    if s == 0:
        return z
    row = lax.broadcasted_iota(jnp.int32, z.shape, 0)
    return jnp.where(row >= s, pltpu.roll(z, s, 0), 0.0)


def _shift_up(z, s):
    if s == 0:
        return z
    n = z.shape[0]
    row = lax.broadcasted_iota(jnp.int32, z.shape, 0)
    return jnp.where(row < n - s, pltpu.roll(z, n - s, 0), 0.0)


def _conv_fwd(z, w_ref, K):
    acc = jnp.zeros_like(z)
    for k in range(K):
        acc = acc + w_ref[k:k + 1, :] * _shift_down(z, K - 1 - k)
    return acc


def _conv_bwd(dy, z, w_ref, dw_ref, K):
    dz = jnp.zeros_like(z)
    for k in range(K):
        dz = dz + w_ref[k:k + 1, :] * _shift_up(dy, K - 1 - k)
        dw_ref[k:k + 1, :] += _colsum(dy * _shift_down(z, K - 1 - k))
    return dz


def _sconv_fwd(proj3, w, off, *, name):
    Bn, S, _ = proj3.shape

    def body(bg_ref, cg_ref, xb_ref, w_ref, o_ref):
        z = cg_ref[...].astype(F32) * xb_ref[...].astype(F32)
        o_ref[...] = (bg_ref[...].astype(F32) * _conv_fwd(z, w_ref, SHORT_CONV)).astype(BF16)

    return pl.pallas_call(
        body, name=name, grid=(Bn,),
        in_specs=[_seq_spec(S, B_WIDTH, off['b'] + j * B_WIDTH) for j in range(3)] + [_full((SHORT_CONV, B_WIDTH))],
        out_specs=_seq_spec(S, B_WIDTH, 0), out_shape=jax.ShapeDtypeStruct((Bn, S, B_WIDTH), BF16),
        compiler_params=_params("parallel"),
    )(proj3, proj3, proj3, w)


def _sconv_bwd(proj3, w, dyb3, off, *, name):
    Bn, S, _ = proj3.shape

    def body(bg_ref, cg_ref, xb_ref, w_ref, do_ref, din_ref, dw_ref):
        @pl.when(pl.program_id(0) == 0)
        def _():
            dw_ref[...] = jnp.zeros_like(dw_ref)

        cg, xb = cg_ref[...].astype(F32), xb_ref[...].astype(F32)
        z = cg * xb
        do = do_ref[...].astype(F32)
        din_ref[:, 0:B_WIDTH] = (do * _conv_fwd(z, w_ref, SHORT_CONV)).astype(BF16)
        dz = _conv_bwd(do * bg_ref[...].astype(F32), z, w_ref, dw_ref, SHORT_CONV)
        din_ref[:, B_WIDTH:2 * B_WIDTH] = (dz * xb).astype(BF16)
        din_ref[:, 2 * B_WIDTH:3 * B_WIDTH] = (dz * cg).astype(BF16)

    return pl.pallas_call(
        body, name=name, grid=(Bn,),
        in_specs=[_seq_spec(S, B_WIDTH, off['b'] + j * B_WIDTH) for j in range(3)]
        + [_full((SHORT_CONV, B_WIDTH)), _seq_spec(S, B_WIDTH, 0)],
        out_specs=[_seq_spec(S, 3 * B_WIDTH, 0), _full((SHORT_CONV, B_WIDTH))],
        out_shape=[jax.ShapeDtypeStruct((Bn, S, 3 * B_WIDTH), BF16), jax.ShapeDtypeStruct((SHORT_CONV, B_WIDTH), F32)],
        compiler_params=_params("arbitrary"),
    )(proj3, proj3, proj3, w, dyb3)


def _cconv_pre(cin_ref, w_ref, cb_ref):
    x = cin_ref[...].astype(F32)
    a, gt = x[:, 0:C_WIDTH], x[:, C_WIDTH:2 * C_WIDTH]
    sg = jax.nn.sigmoid(gt)
    glu = a * sg
    y0 = _conv_fwd(glu, w_ref, CONF_CONV) + cb_ref[...]
    mu = jnp.mean(y0, axis=-1, keepdims=True)
    xc = y0 - mu
    rs = lax.rsqrt(jnp.mean(xc * xc, axis=-1, keepdims=True) + EPS)
    return a, sg, glu, xc * rs, rs


def _cconv_fwd(proj3, w, cbias, lg, lb, off, *, name):
    Bn, S, _ = proj3.shape

    def body(cin_ref, w_ref, cb_ref, lg_ref, lb_ref, o_ref):
        _, _, _, xh, _ = _cconv_pre(cin_ref, w_ref, cb_ref)
        ln = xh * lg_ref[...] + lb_ref[...]
        o_ref[...] = (ln * jax.nn.sigmoid(ln)).astype(BF16)

    return pl.pallas_call(
        body, name=name, grid=(Bn,),
        in_specs=[_seq_spec(S, 2 * C_WIDTH, off['c']), _full((CONF_CONV, C_WIDTH)), _full((1, C_WIDTH)),
                  _full((1, C_WIDTH)), _full((1, C_WIDTH))],
        out_specs=_seq_spec(S, C_WIDTH, 0), out_shape=jax.ShapeDtypeStruct((Bn, S, C_WIDTH), BF16),
        compiler_params=_params("parallel"),
    )(proj3, w, cbias, lg, lb)


def _cconv_bwd(proj3, w, cbias, lg, lb, dyc3, off, *, name):
    Bn, S, _ = proj3.shape

    def body(cin_ref, w_ref, cb_ref, lg_ref, lb_ref, do_ref, din_ref, dw_ref, dcb_ref, dlg_ref, dlb_ref):
        @pl.when(pl.program_id(0) == 0)
        def _():
            for r in (dw_ref, dcb_ref, dlg_ref, dlb_ref):
                r[...] = jnp.zeros_like(r)

        a, sg, glu, xh, rs = _cconv_pre(cin_ref, w_ref, cb_ref)
        ln = xh * lg_ref[...] + lb_ref[...]
        sl = jax.nn.sigmoid(ln)
        dln = do_ref[...].astype(F32) * (sl * (1.0 + ln * (1.0 - sl)))
        dlg_ref[...] += _colsum(dln * xh)
        dlb_ref[...] += _colsum(dln)
        dxh = dln * lg_ref[...]
        dy0 = rs * (dxh - jnp.mean(dxh, axis=-1, keepdims=True) - xh * jnp.mean(dxh * xh, axis=-1, keepdims=True))
        dcb_ref[...] += _colsum(dy0)
        dglu = _conv_bwd(dy0, glu, w_ref, dw_ref, CONF_CONV)
        din_ref[:, 0:C_WIDTH] = (dglu * sg).astype(BF16)
        din_ref[:, C_WIDTH:2 * C_WIDTH] = (dglu * a * sg * (1.0 - sg)).astype(BF16)

    vec = _full((1, C_WIDTH))
    return pl.pallas_call(
        body, name=name, grid=(Bn,),
        in_specs=[_seq_spec(S, 2 * C_WIDTH, off['c']), _full((CONF_CONV, C_WIDTH)), vec, vec, vec, _seq_spec(S, C_WIDTH, 0)],
        out_specs=[_seq_spec(S, 2 * C_WIDTH, 0), _full((CONF_CONV, C_WIDTH)), vec, vec, vec],
        out_shape=[jax.ShapeDtypeStruct((Bn, S, 2 * C_WIDTH), BF16), jax.ShapeDtypeStruct((CONF_CONV, C_WIDTH), F32)]
        + [jax.ShapeDtypeStruct((1, C_WIDTH), F32)] * 3,
        compiler_params=_params("arbitrary"),
    )(proj3, w, cbias, lg, lb, dyc3)


def _swa_band(x_ref, g, nb):
    xb = x_ref[:, HEAD_DIM * g:HEAD_DIM * (g + 1)].reshape(nb, Q_BLOCK, HEAD_DIM)
    prev = jnp.concatenate([jnp.zeros((1, Q_BLOCK, HEAD_DIM), xb.dtype), xb[:-1]], axis=0)
    return jnp.concatenate([prev, xb], axis=1)


def _swa_probs(q_ref, kband, bias_ref, sk_ref, h, nb):
    qh = q_ref[:, HEAD_DIM * h:HEAD_DIM * (h + 1)].reshape(nb, Q_BLOCK, HEAD_DIM)
    s = jnp.einsum('nqd,nsd->nqs', qh, kband, preferred_element_type=F32) * HEAD_DIM ** -0.5 + bias_ref[h][None]
    shape = (nb, Q_BLOCK, 2 * Q_BLOCK)
    n = lax.broadcasted_iota(jnp.int32, shape, 0)
    dist = lax.broadcasted_iota(jnp.int32, shape, 1) + Q_BLOCK - lax.broadcasted_iota(jnp.int32, shape, 2)
    col = lax.broadcasted_iota(jnp.int32, shape, 2)
    valid = (dist >= 0) & (dist < WINDOW) & ((n > 0) | (col >= Q_BLOCK))
    s = jnp.where(valid, s, NEG_INF)
    sink = sk_ref[h:h + 1, 0:1].reshape(1, 1, 1)
    m = jnp.maximum(jnp.max(s, axis=-1, keepdims=True), sink)
    e = jnp.exp(s - m)
    es = jnp.exp(sink - m)
    den = jnp.sum(e, axis=-1, keepdims=True) + es
    return qh, e / den, es / den


def _swa_fwd(proj3, band_bias, sinks, off, *, name):
    Bn, S, _ = proj3.shape
    nb = S // Q_BLOCK

    def body(q_ref, k_ref, v_ref, bias_ref, sk_ref, o_ref):
        for g in range(D_KV_HEADS):
            kband, vband = _swa_band(k_ref, g, nb), _swa_band(v_ref, g, nb)
            for h in range(g * D_GROUP, (g + 1) * D_GROUP):
                _, p, _ = _swa_probs(q_ref, kband, bias_ref, sk_ref, h, nb)
                out = jnp.einsum('nqs,nsd->nqd', p.astype(BF16), vband, preferred_element_type=F32)
                o_ref[:, HEAD_DIM * h:HEAD_DIM * (h + 1)] = out.reshape(S, HEAD_DIM).astype(BF16)

    kcol = off['d'] + D_WIDTH
    return pl.pallas_call(
        body, name=name, grid=(Bn,),
        in_specs=[_seq_spec(S, D_WIDTH, off['d']), _seq_spec(S, D_KV_WIDTH, kcol), _seq_spec(S, D_KV_WIDTH, kcol + D_KV_WIDTH),
                  _full((D_Q_HEADS, Q_BLOCK, 2 * Q_BLOCK)), _full((D_Q_HEADS, 128))],
        out_specs=_seq_spec(S, D_WIDTH, 0), out_shape=jax.ShapeDtypeStruct((Bn, S, D_WIDTH), BF16),
        compiler_params=_params("parallel"),
    )(proj3, proj3, proj3, band_bias, sinks)


def _swa_bwd(proj3, band_bias, sinks, dyd3, off, *, name):
    Bn, S, _ = proj3.shape
    nb = S // Q_BLOCK
    scale = HEAD_DIM ** -0.5

    def body(q_ref, k_ref, v_ref, bias_ref, sk_ref, do_ref, dqkv_ref, dband_ref, dsk_ref):
        @pl.when(pl.program_id(0) == 0)
        def _():
            dband_ref[...] = jnp.zeros_like(dband_ref)
            dsk_ref[...] = jnp.zeros_like(dsk_ref)

        def unband(acc):
            prev, cur = acc[:, 0:Q_BLOCK, :], acc[:, Q_BLOCK:2 * Q_BLOCK, :]
            nxt = jnp.concatenate([prev[1:], jnp.zeros((1, Q_BLOCK, HEAD_DIM), F32)], axis=0)
            return (cur + nxt).reshape(S, HEAD_DIM).astype(BF16)

        for g in range(D_KV_HEADS):
            kband, vband = _swa_band(k_ref, g, nb), _swa_band(v_ref, g, nb)
            dkb = jnp.zeros((nb, 2 * Q_BLOCK, HEAD_DIM), F32)
            dvb = jnp.zeros((nb, 2 * Q_BLOCK, HEAD_DIM), F32)
            for h in range(g * D_GROUP, (g + 1) * D_GROUP):
                hs = slice(HEAD_DIM * h, HEAD_DIM * (h + 1))
                qh, p, ps = _swa_probs(q_ref, kband, bias_ref, sk_ref, h, nb)
                doh = do_ref[:, hs].reshape(nb, Q_BLOCK, HEAD_DIM)
                dp = jnp.einsum('nqd,nsd->nqs', doh, vband, preferred_element_type=F32)
                delta = jnp.sum(p * dp, axis=-1, keepdims=True)
                ds = p * (dp - delta)
                dsink = jnp.sum(jnp.sum(-ps * delta, axis=0), axis=0, keepdims=True)
                dsk_ref[h:h + 1, :] += jnp.broadcast_to(dsink, (1, 128))
                dband_ref[h] += jnp.sum(ds, axis=0)
                dsb = ds.astype(BF16)
                dq = jnp.einsum('nqs,nsd->nqd', dsb, kband, preferred_element_type=F32) * scale
                dqkv_ref[:, hs] = dq.reshape(S, HEAD_DIM).astype(BF16)
                dkb = dkb + jnp.einsum('nqs,nqd->nsd', dsb, qh, preferred_element_type=F32) * scale
                dvb = dvb + jnp.einsum('nqs,nqd->nsd', p.astype(BF16), doh, preferred_element_type=F32)
            dqkv_ref[:, D_WIDTH + HEAD_DIM * g:D_WIDTH + HEAD_DIM * (g + 1)] = unband(dkb)
            dqkv_ref[:, D_WIDTH + D_KV_WIDTH + HEAD_DIM * g:D_WIDTH + D_KV_WIDTH + HEAD_DIM * (g + 1)] = unband(dvb)

    kcol = off['d'] + D_WIDTH
    wq = D_WIDTH + 2 * D_KV_WIDTH
    return pl.pallas_call(
        body, name=name, grid=(Bn,),
        in_specs=[_seq_spec(S, D_WIDTH, off['d']), _seq_spec(S, D_KV_WIDTH, kcol), _seq_spec(S, D_KV_WIDTH, kcol + D_KV_WIDTH),
                  _full((D_Q_HEADS, Q_BLOCK, 2 * Q_BLOCK)), _full((D_Q_HEADS, 128)), _seq_spec(S, D_WIDTH, 0)],
        out_specs=[_seq_spec(S, wq, 0), _full((D_Q_HEADS, Q_BLOCK, 2 * Q_BLOCK)), _full((D_Q_HEADS, 128))],
        out_shape=[jax.ShapeDtypeStruct((Bn, S, wq), BF16), jax.ShapeDtypeStruct((D_Q_HEADS, Q_BLOCK, 2 * Q_BLOCK), F32),
                   jax.ShapeDtypeStruct((D_Q_HEADS, 128), F32)],
        compiler_params=_params("arbitrary"),
    )(proj3, proj3, proj3, band_bias, sinks, dyd3)


def _assemble(pieces, *, name):
    T = pieces[0].shape[0]
    widths = [p.shape[1] for p in pieces]
    tm = _tile(T, 512, 256, 128)

    def body(*refs):
        out_ref = refs[-1]
        col = 0
        for r, w in zip(refs[:-1], widths):
            out_ref[:, col:col + w] = r[...]
            col += w

    return pl.pallas_call(
        body, name=name, grid=(T // tm,),
        in_specs=[pl.BlockSpec((tm, w), lambda i: (i, 0)) for w in widths],
        out_specs=pl.BlockSpec((tm, sum(widths)), lambda i: (i, 0)),
        out_shape=jax.ShapeDtypeStruct((T, sum(widths)), BF16),
        compiler_params=_params("parallel"),
    )(*pieces)


def _relbias_grad(dband, onehot, *, name):
    L, H, n = dband.shape
    R = onehot.shape[0]

    def body(d_ref, oh_ref, out_ref):
        tot = d_ref[0]
        for l in range(1, L):
            tot = tot + d_ref[l]
        out_ref[...] = lax.dot_general(oh_ref[...], tot, (((1,), (1,)), ((), ())), preferred_element_type=F32,
                                       precision=lax.Precision.HIGHEST)

    return pl.pallas_call(
        body, name=name, out_shape=jax.ShapeDtypeStruct((R, H), F32),
        compiler_params=pltpu.CompilerParams(vmem_limit_bytes=VMEM_LIMIT_V7X),
    )(dband, onehot)


def _all_reduce(buf, *, name):
    R, C = buf.shape

    def body(x_ref, o_ref, land, send, recv):
        me = _my_id()
        land[pl.ds(me, 1)] = x_ref[...][None]
        sends = []
        for k in range(1, N_DEV):
            to, _ = _peer(k)
            cp = pltpu.make_async_remote_copy(src_ref=x_ref, dst_ref=land.at[me], send_sem=send.at[k - 1],
                                              recv_sem=recv.at[k - 1], device_id=to, device_id_type=MESH)
            cp.start()
            sends.append(cp)
        for k in range(1, N_DEV):
            frm, frm_id = _peer(k)
            pltpu.make_async_remote_copy(src_ref=x_ref, dst_ref=land.at[frm_id], send_sem=send.at[k - 1],
                                         recv_sem=recv.at[k - 1], device_id=frm, device_id_type=MESH).wait_recv()
        for cp in sends:
            cp.wait_send()
        acc = land[0]
        for d in range(1, N_DEV):
            acc = acc + land[d]
        o_ref[...] = acc

    vmem = pl.BlockSpec(memory_space=pltpu.VMEM)
    return pl.pallas_call(
        body, name=name, in_specs=[vmem], out_specs=vmem, out_shape=jax.ShapeDtypeStruct((R, C), F32),
        scratch_shapes=[pltpu.VMEM((N_DEV, R, C), F32), pltpu.SemaphoreType.DMA((N_DEV - 1,)),
                        pltpu.SemaphoreType.DMA((N_DEV - 1,))],
        compiler_params=pltpu.CompilerParams(vmem_limit_bytes=VMEM_LIMIT_V7X),
    )(buf)


def _row_tile(rows, row_bytes, align):
    fits = [t for t in range(align, rows + 1, align) if rows % t == 0]
    small = [t for t in fits if t * row_bytes <= 2**20]
    return max(small) if small else (min(fits) if fits else rows)


def _sum8(recvs, *, name):
    L = len(recvs)
    _, rows, C = recvs[0].shape
    tr = _row_tile(rows, C * 4 * L, 16)

    def body(*refs):
        o_ref = refs[-1]
        for l, r_ref in enumerate(refs[:-1]):
            acc = r_ref[0].astype(F32)
            for d in range(1, N_DEV):
                acc = acc + r_ref[d].astype(F32)
            o_ref[l] = acc

    return pl.pallas_call(
        body, name=name, grid=(rows // tr,),
        in_specs=[pl.BlockSpec((N_DEV, tr, C), lambda i: (0, i, 0))] * L,
        out_specs=pl.BlockSpec((L, tr, C), lambda i: (0, i, 0)),
        out_shape=jax.ShapeDtypeStruct((L, rows, C), F32), compiler_params=_params("parallel"),
    )(*recvs)


def _band_bias(rel_bias, onehot, *, name):
    R, H = rel_bias.shape

    def body(rb_ref, oh_ref, out_ref):
        out_ref[...] = lax.dot_general(rb_ref[...], oh_ref[...], (((0,), (0,)), ((), ())), preferred_element_type=F32,
                                       precision=lax.Precision.HIGHEST)

    return pl.pallas_call(
        body, name=name, out_shape=jax.ShapeDtypeStruct((H, onehot.shape[1]), F32),
        compiler_params=pltpu.CompilerParams(vmem_limit_bytes=VMEM_LIMIT_V7X),
    )(rel_bias, onehot)


def _adamw(g, w, m, v, *, name):
    rows, C = g.shape
    tr = _row_tile(rows, C * 4, 8)

    def body(g_ref, w_ref, m_ref, v_ref, d_ref, mo_ref, vo_ref):
        gt = g_ref[...]
        mn = ADAM_B1 * m_ref[...] + (1.0 - ADAM_B1) * gt
        vn = ADAM_B2 * v_ref[...] + (1.0 - ADAM_B2) * jnp.square(gt)
        m_hat = mn / (1.0 - ADAM_B1 ** ADAM_STEP)
        v_hat = vn / (1.0 - ADAM_B2 ** ADAM_STEP)
        d_ref[...] = -ADAM_LR * (m_hat / (jnp.sqrt(v_hat) + ADAM_EPS) + ADAM_WD * w_ref[...])
        mo_ref[...] = mn
        vo_ref[...] = vn

    spec = pl.BlockSpec((tr, C), lambda i: (i, 0))
    return pl.pallas_call(
        body, name=name, grid=(rows // tr,), in_specs=[spec] * 4, out_specs=[spec] * 3,
        out_shape=[jax.ShapeDtypeStruct((rows, C), F32)] * 3, compiler_params=_params("parallel"),
    )(g, w, m, v)


def _in_splits():
    a_f = 3 * A_WIDTH
    b = a_f + A_HEADS
    c = b + 3 * B_WIDTH
    d = c + 2 * C_WIDTH
    gates = d + D_WIDTH + 2 * D_KV_WIDTH
    return a_f, b, c, d, gates


def _permute_in(w):
    a_f, b, c, d, gates = _in_splits()
    pad = jnp.zeros(w.shape[:-1] + (AF_PAD - A_HEADS,), w.dtype)
    return jnp.concatenate([w[..., gates:], w[..., :a_f], w[..., b:c], w[..., c:d], w[..., d:gates], w[..., a_f:b], pad], axis=-1)


def _unpermute_in(g, D):
    off = _layout(D)
    return jnp.concatenate([g[..., off['a']:off['b']], g[..., off['af']:off['af'] + A_HEADS], g[..., off['b']:off['c']],
                            g[..., off['c']:off['d']], g[..., off['d']:off['af']], g[..., :off['a']]], axis=-1)


def _bucket_onehot():
    dist = np.maximum(np.arange(Q_BLOCK)[:, None] + Q_BLOCK - np.arange(2 * Q_BLOCK)[None, :], 0)
    max_exact = REL_BUCKETS // 2
    large = max_exact + (np.log(np.maximum(dist, 1).astype(np.float32) / np.float32(max_exact))
                         / np.float32(math.log(REL_MAX_DIST / max_exact)) * np.float32(REL_BUCKETS - max_exact)).astype(np.int32)
    bucket = np.where(dist < max_exact, dist, np.minimum(large, REL_BUCKETS - 1))
    return (bucket.reshape(1, -1) == np.arange(REL_BUCKETS)[:, None]).astype(np.float32)


GATHER_SEQ = [('ffn1_up', ['ffn1_w_gu']), ('ffn1_down', ['ffn1_w_down']), ('proj', ['w_in']),
              ('fox', ['w_br_a', 'w_br_b', 'w_br_c', 'w_br_d', 'w_o', 'w_ple_gate', 'w_ple']),
              ('ffn2_up', ['ffn2_w_gu']), ('ffn2_down', ['ffn2_w_down'])]
GATHER_AHEAD = 2
SCATTER_HOSTS = {'ffn2_down_bwd': ['w_ple', 'w_ple_gate'], 'ffn2_up_bwd': ['ffn2_w_gu'],
                 'fox_bwd': ['ffn2_w_down', 'w_o', 'w_br_a', 'w_br_b', 'w_br_c', 'w_br_d'], 'proj_bwd': ['w_in'],
                 'ffn1_down_bwd': [], 'ffn1_up_bwd': ['ffn1_w_down', 'ffn1_w_gu']}


def _pack(parts):
    flat = jnp.concatenate([q.reshape(-1).astype(F32) for q in parts])
    return jnp.pad(flat, (0, (-flat.shape[0]) % 1024)).reshape(-1, 128)


def _unpack(buf, shapes):
    flat, out, pos = buf.reshape(-1), [], 0
    for s in shapes:
        n = math.prod(s)
        out.append(flat[pos:pos + n].reshape(s))
        pos += n
    return out


def kernel(x, p, ffn1_norm_pre, ffn1_w_gu, ffn1_w_down, ffn1_norm_post, mix_norm_pre, w_in, b_forget, b_gate, conv_short, conv_dw, conv_dw_bias, conv_ln_gain, conv_ln_bias, attn_sinks, rel_bias, w_br_a, w_br_b, w_br_c, w_br_d, w_o, mix_norm_post, ffn2_norm_pre, ffn2_w_gu, ffn2_w_down, ffn2_norm_post, ple_norm_gate, w_ple_gate, w_ple, ple_norm_post, loss_target, m_ffn1_norm_pre, m_ffn1_w_gu, m_ffn1_w_down, m_ffn1_norm_post, m_mix_norm_pre, m_w_in, m_b_forget, m_b_gate, m_conv_short, m_conv_dw, m_conv_dw_bias, m_conv_ln_gain, m_conv_ln_bias, m_attn_sinks, m_rel_bias, m_w_br_a, m_w_br_b, m_w_br_c, m_w_br_d, m_w_o, m_mix_norm_post, m_ffn2_norm_pre, m_ffn2_w_gu, m_ffn2_w_down, m_ffn2_norm_post, m_ple_norm_gate, m_w_ple_gate, m_w_ple, m_ple_norm_post, v_ffn1_norm_pre, v_ffn1_w_gu, v_ffn1_w_down, v_ffn1_norm_post, v_mix_norm_pre, v_w_in, v_b_forget, v_b_gate, v_conv_short, v_conv_dw, v_conv_dw_bias, v_conv_ln_gain, v_conv_ln_bias, v_attn_sinks, v_rel_bias, v_w_br_a, v_w_br_b, v_w_br_c, v_w_br_d, v_w_o, v_mix_norm_post, v_ffn2_norm_pre, v_ffn2_w_gu, v_ffn2_w_down, v_ffn2_norm_post, v_ple_norm_gate, v_w_ple_gate, v_w_ple, v_ple_norm_post):
    a = dict(locals())
    Bn, S, D = x.shape
    T = Bn * S
    L, E = p.shape[0], p.shape[-1]
    F = ffn1_w_down.shape[1] * N_DEV
    off = _layout(D)
    PW = off['end']
    me = _my_id()

    shard = {n: a[n].astype(BF16) for n in BIG}
    shard['ffn1_w_gu'] = jnp.swapaxes(ffn1_w_gu, 1, 2).astype(BF16)
    shard['ffn2_w_gu'] = jnp.swapaxes(ffn2_w_gu, 1, 2).astype(BF16)
    shard['w_in'] = _permute_in(w_in).astype(BF16)
    is_col = lambda names: [n in COL_SHARDED for n in names]
    head = [n for _, names in GATHER_SEQ[:GATHER_AHEAD] for n in names]
    first = _Exchange("gather", [shard[n][0] for n in head], is_col(head))
    full = [dict(zip(head, first.run_alone("gather_head")))] + [{} for _ in range(1, L)]
    hosts = [h_ for h_, _ in GATHER_SEQ]

    def with_gather(fn, host, i, *args, **kw):
        li, lj = divmod(i * len(hosts) + hosts.index(host) + GATHER_AHEAD, len(hosts))
        names = GATHER_SEQ[lj][1]
        rider = _Exchange("gather", [shard[n][li] for n in names], is_col(names)) if li < L else None
        out = fn(*args, rider=rider, **kw)
        if rider is not None:
            full[li].update(zip(names, rider.result))
        return out

    cw = conv_short.shape[2]
    conv_full = [lax.dynamic_update_slice(jnp.zeros(c.shape[:2] + (cw * N_DEV,), F32), c, (0, 0, me * cw))
                 for c in (conv_short, conv_dw)]
    conv_shapes = [c.shape for c in conv_full]
    cs_all, cdw_all = _unpack(_all_reduce(_pack(conv_full), name="gather_conv"), conv_shapes)

    onehot = jnp.asarray(_bucket_onehot())
    band_bias = _band_bias(rel_bias, onehot, name="band_bias").reshape(D_Q_HEADS, Q_BLOCK, 2 * Q_BLOCK)

    def vec(name, i):
        return a[name][i][None]

    def lay(i):
        return dict(
            bfor=jnp.pad(b_forget[i], (0, AF_LANES - A_HEADS))[None], bgate=b_gate[i].reshape(N_BRANCH, D),
            cs=cs_all[i], cdw=cdw_all[i], cb=conv_dw_bias[i][None], lg=conv_ln_gain[i][None], lb=conv_ln_bias[i][None],
            sinks=jnp.broadcast_to(attn_sinks[i][:, None], (D_Q_HEADS, 128)),
            wbrs=[full[i][n] for n in ('w_br_a', 'w_br_b', 'w_br_c', 'w_br_d')], pe=p[i].reshape(T, E))

    def wgu(k, i):
        return full[i][f'ffn{k}_w_gu'].reshape(2, F, D)

    h = x.reshape(T, D)
    saved = []
    for i in range(L):
        s = dict(h0=h)
        s['gu1'], s['n1'] = with_gather(_rms_mm, 'ffn1_up', i, h, vec('ffn1_norm_pre', i), wgu(1, i), nt=True,
                                        out_dtype=BF16, save_n=True, name=f"ffn1_up_{i}")
        s['h1'], s['f1'] = with_gather(_ffn_down, 'ffn1_down', i, s['gu1'], full[i]['ffn1_w_down'], h,
                                       vec('ffn1_norm_post', i), name=f"ffn1_down_{i}")
        win = full[i]['w_in']
        proj, s['u'] = with_gather(_rms_mm, 'proj', i, s['h1'], vec('mix_norm_pre', i), win[None], nt=False,
                                   out_dtype=BF16, save_n=True, name=f"proj_{i}")
        af = _rms_mm(s['h1'], vec('mix_norm_pre', i), win[None, :, off['af']:off['af'] + AF_LANES], nt=False,
                     out_dtype=F32, save_n=False, name=f"proj_forget_{i}")
        s['proj'] = proj.reshape(T, PW)
        s['proj3'], s['af3'] = proj.reshape(Bn, S, PW), af.reshape(Bn, S, AF_LANES)
        q = lay(i)
        ya = with_gather(_fox_fwd, 'fox', i, s['proj3'], s['af3'], q['bfor'], off, name=f"fox_{i}")
        yb = _sconv_fwd(s['proj3'], q['cs'], off, name=f"sconv_{i}")
        yc = _cconv_fwd(s['proj3'], q['cdw'], q['cb'], q['lg'], q['lb'], off, name=f"cconv_{i}")
        yd = _swa_fwd(s['proj3'], band_bias, q['sinks'], off, name=f"swa_{i}")
        s['ys'] = [y.reshape(T, y.shape[-1]) for y in (ya, yb, yc, yd)]
        s['h2'], s['o'], s['merged'] = _merge_out(s['ys'], s['proj'], q['bgate'], q['wbrs'], full[i]['w_o'], s['h1'],
                                                  vec('mix_norm_post', i), name=f"merge_{i}")
        s['gu2'], s['n2'] = with_gather(_rms_mm, 'ffn2_up', i, s['h2'], vec('ffn2_norm_pre', i), wgu(2, i), nt=True,
                                        out_dtype=BF16, save_n=True, name=f"ffn2_up_{i}")
        s['h3'], s['f2'] = with_gather(_ffn_down, 'ffn2_down', i, s['gu2'], full[i]['ffn2_w_down'], s['h2'],
                                       vec('ffn2_norm_post', i), name=f"ffn2_down_{i}")
        h = _ple(s['h3'], q['pe'], vec('ple_norm_gate', i), full[i]['w_ple_gate'], full[i]['w_ple'],
                 vec('ple_norm_post', i), name=f"ple_{i}")
        saved.append(s)

    lpart, dh = _loss_head(h, loss_target.reshape(T, D), name="loss_head")
    loss = lax.psum(lpart[0, 0], MESH_AXES)

    gbuf = [{} for _ in range(L)]
    recv = [{} for _ in range(L)]
    sg = {n: [None] * L for n in WEIGHTS if n not in BIG and n != 'rel_bias'}
    dbands = [None] * L

    def wgrad(n, i, a_, b_, **kw):
        gbuf[i][n] = _mm_tn(a_, b_, name=f"d_{n}_{i}", **kw)

    def with_scatter(fn, host, i, *args, **kw):
        names = SCATTER_HOSTS[host]
        rider = _Exchange("scatter", [gbuf[i][n] for n in names], is_col(names)) if names else None
        out = fn(*args, rider=rider, **kw)
        if rider is not None:
            recv[i].update(zip(names, rider.result))
        return out

    def ffn_bwd(k, i, dh_out, s, h_in):
        dgu, df, sg[f'ffn{k}_norm_post'][i] = with_scatter(
            _ffn_down_bwd, f'ffn{k}_down_bwd', i, dh_out, s[f'f{k}'], vec(f'ffn{k}_norm_post', i), full[i][f'ffn{k}_w_down'],
            s[f'gu{k}'], name=f"ffn{k}_down_bwd_{i}")
        wgrad(f'ffn{k}_w_down', i, s[f'gu{k}'], df, swiglu=True)
        wgrad(f'ffn{k}_w_gu', i, dgu, s[f'n{k}'])
        dh_in, sg[f'ffn{k}_norm_pre'][i] = with_scatter(
            _mm_rmsbwd, f'ffn{k}_up_bwd', i, dgu, wgu(k, i), dh_out, h_in, vec(f'ffn{k}_norm_pre', i), nt=False,
            name=f"ffn{k}_up_bwd_{i}")
        return dh_in

    for i in reversed(range(L)):
        s, q = saved[i], lay(i)
        dh, de, dpgl, npg, sg['ple_norm_gate'][i], sg['ple_norm_post'][i] = _ple_bwd(
            dh, s['h3'], q['pe'], vec('ple_norm_gate', i), full[i]['w_ple_gate'], full[i]['w_ple'], vec('ple_norm_post', i),
            name=f"ple_bwd_{i}")
        wgrad('w_ple', i, q['pe'][None], de)
        wgrad('w_ple_gate', i, npg[None], dpgl)
        dh = ffn_bwd(2, i, dh, s, s['h2'])
        do, dz, *dys, dgates, sg['mix_norm_post'][i], dbg = _merge_out_bwd(
            dh, s['o'], vec('mix_norm_post', i), full[i]['w_o'], s['ys'], s['proj'], q['bgate'], q['wbrs'],
            name=f"merge_bwd_{i}")
        sg['b_gate'][i] = dbg.reshape(-1)
        wgrad('w_o', i, s['merged'][None], do)
        for b, n in enumerate(('w_br_a', 'w_br_b', 'w_br_c', 'w_br_d')):
            wgrad(n, i, s['ys'][b][None], dz, b_plane=b)
        dy3 = [d.reshape(Bn, S, d.shape[-1]) for d in dys]
        da, daf, dbf = with_scatter(_fox_bwd, 'fox_bwd', i, s['proj3'], s['af3'], q['bfor'], dy3[0], off,
                                    name=f"fox_bwd_{i}")
        sg['b_forget'][i] = dbf[0, :A_HEADS]
        db, sg['conv_short'][i] = _sconv_bwd(s['proj3'], q['cs'], dy3[1], off, name=f"sconv_bwd_{i}")
        dc, sg['conv_dw'][i], sg['conv_dw_bias'][i], sg['conv_ln_gain'][i], sg['conv_ln_bias'][i] = _cconv_bwd(
            s['proj3'], q['cdw'], q['cb'], q['lg'], q['lb'], dy3[2], off, name=f"cconv_bwd_{i}")
        dd, dbands[i], dsk = _swa_bwd(s['proj3'], band_bias, q['sinks'], dy3[3], off, name=f"swa_bwd_{i}")
        sg['attn_sinks'][i] = dsk[:, 0]
        dproj = _assemble([dgates] + [t.reshape(T, t.shape[-1]) for t in (da, db, dc, dd, daf)], name=f"dproj_{i}")
        wgrad('w_in', i, s['u'][None], dproj)
        dh, sg['mix_norm_pre'][i] = with_scatter(
            _mm_rmsbwd, 'proj_bwd', i, dproj[None], full[i]['w_in'][None], dh, s['h1'], vec('mix_norm_pre', i), nt=True,
            name=f"proj_bwd_{i}")
        dh = ffn_bwd(1, i, dh, s, s['h0'])
    grad_x = dh.reshape(Bn, S, D)

    d_rel = _relbias_grad(jnp.stack(dbands).reshape(L, D_Q_HEADS, -1), onehot, name="relbias_grad")
    small_g = [d_rel if n == 'rel_bias' else jnp.stack(sg[n]).reshape(a[n].shape) for n in SMALL]
    n_small_rows = _pack(small_g).shape[0]
    conv_g = [jnp.stack(sg[n]) for n in CONV_SHARDED]
    red = _all_reduce(jnp.concatenate([_pack(small_g), _pack(conv_g)]), name="reduce_small")
    g_small_buf = red[:n_small_rows]
    conv_gfull = _unpack(red[n_small_rows:], conv_shapes)
    conv_gloc = [lax.dynamic_slice_in_dim(g, me * cw, cw, axis=2) for g in conv_gfull]

    grads, deltas, new_m, new_v = {}, {}, {}, {}
    small_shapes = [a[n].shape for n in SMALL]
    res = _adamw(g_small_buf, *[_pack([a[pre + n] for n in SMALL]) for pre in ('', 'm_', 'v_')], name="adamw_small")
    for dst, buf in zip((grads, deltas, new_m, new_v), (g_small_buf,) + tuple(res)):
        dst.update(zip(SMALL, _unpack(buf, small_shapes)))
    loc_shapes = [a[n].shape for n in CONV_SHARDED]
    g_conv_buf = _pack(conv_gloc)
    res = _adamw(g_conv_buf, *[_pack([a[pre + n] for n in CONV_SHARDED]) for pre in ('', 'm_', 'v_')], name="adamw_conv")
    for dst, buf in zip((grads, deltas, new_m, new_v), (g_conv_buf,) + tuple(res)):
        dst.update(zip(CONV_SHARDED, _unpack(buf, loc_shapes)))

    for n in BIG:
        g = _sum8([recv[l][n] for l in range(L)], name=f"sum_{n}")
        if n in ('ffn1_w_gu', 'ffn2_w_gu'):
            g = jnp.swapaxes(g, 1, 2)
        elif n == 'w_in':
            g = _unpermute_in(g, D)
        C = g.shape[-1]
        res = _adamw(g.reshape(-1, C), *[a[pre + n].reshape(-1, C) for pre in ('', 'm_', 'v_')], name=f"adamw_{n}")
        grads[n] = g
        deltas[n], new_m[n], new_v[n] = [t.reshape(g.shape) for t in res]

    return (loss, grad_x, *[grads[n] for n in WEIGHTS], *[deltas[n] for n in WEIGHTS],
            *[new_m[n] for n in WEIGHTS], *[new_v[n] for n in WEIGHTS])
```

```python
import functools
import math

import jax
import jax.numpy as jnp
import numpy as np
from jax import lax
from jax.experimental import pallas as pl
from jax.experimental.pallas import tpu as pltpu

F32 = jnp.float32
BF16 = jnp.bfloat16

EPS = 1e-6
NEG_INF = -1e30
HEAD_DIM = 64
A_HEADS = 4
A_WIDTH = A_HEADS * HEAD_DIM
B_WIDTH = 256
C_WIDTH = 256
SHORT_CONV = 3
CONF_CONV = 31
D_Q_HEADS = 8
D_KV_HEADS = 2
D_GROUP = D_Q_HEADS // D_KV_HEADS
D_WIDTH = D_Q_HEADS * HEAD_DIM
D_KV_WIDTH = D_KV_HEADS * HEAD_DIM
WINDOW = 128
Q_BLOCK = 128
N_BRANCH = 4
REL_BUCKETS = 32
REL_MAX_DIST = 128
AF_PAD = 256
AF_LANES = 128

ADAM_LR = 0.001
ADAM_B1 = 0.9
ADAM_B2 = 0.999
ADAM_EPS = 1e-08
ADAM_WD = 0.01
ADAM_STEP = 10

N_DEV = 8
MESH_AXES = ("x", "y", "c")
VMEM_LIMIT_V7X = 56 * 2**20
MESH = pl.DeviceIdType.MESH

WEIGHTS = ['ffn1_norm_pre', 'ffn1_w_gu', 'ffn1_w_down', 'ffn1_norm_post', 'mix_norm_pre', 'w_in', 'b_forget',
           'b_gate', 'conv_short', 'conv_dw', 'conv_dw_bias', 'conv_ln_gain', 'conv_ln_bias', 'attn_sinks',
           'rel_bias', 'w_br_a', 'w_br_b', 'w_br_c', 'w_br_d', 'w_o', 'mix_norm_post', 'ffn2_norm_pre',
           'ffn2_w_gu', 'ffn2_w_down', 'ffn2_norm_post', 'ple_norm_gate', 'w_ple_gate', 'w_ple', 'ple_norm_post']
ARG_NAMES = ['x', 'p'] + WEIGHTS + ['loss_target'] + ['m_' + n for n in WEIGHTS] + ['v_' + n for n in WEIGHTS]
BIG = ['ffn1_w_gu', 'ffn1_w_down', 'w_in', 'w_br_a', 'w_br_b', 'w_br_c', 'w_br_d', 'w_o', 'ffn2_w_gu',
       'ffn2_w_down', 'w_ple_gate', 'w_ple']
COL_SHARDED = ('w_br_a', 'w_br_b', 'w_br_c', 'w_br_d', 'w_ple')
CONV_SHARDED = ('conv_short', 'conv_dw')
SMALL = [n for n in WEIGHTS if n not in BIG and n not in CONV_SHARDED]


def _tile(n, *prefs):
    for t in prefs:
        if n % t == 0:
            return t
    return n


def _params(*sem):
    return pltpu.CompilerParams(dimension_semantics=sem, vmem_limit_bytes=VMEM_LIMIT_V7X)


def _dot(a, b):
    return jnp.dot(a, b, preferred_element_type=F32)


def _dot_nt(a, b):
    return lax.dot_general(a, b, (((1,), (1,)), ((), ())), preferred_element_type=F32)


def _dot_tn(a, b):
    return lax.dot_general(a, b, (((0,), (0,)), ((), ())), preferred_element_type=F32)


def _rstd(x):
    return lax.rsqrt(jnp.mean(x * x, axis=-1, keepdims=True) + EPS)


def _rms_bwd(dy, x, r, g):
    xh = x * r
    dxh = dy * g
    dx = r * (dxh - xh * jnp.mean(dxh * xh, axis=-1, keepdims=True))
    return dx, dy * xh


def _colsum(v):
    return jnp.sum(v, axis=0, keepdims=True)


def _my_id():
    return lax.axis_index("x") * 4 + lax.axis_index("y") * 2 + lax.axis_index("c")


def _peer(k):
    coords = []
    for bit, axis in zip((4, 2, 1), MESH_AXES):
        me = lax.axis_index(axis)
        coords.append(1 - me if k & bit else me)
    return tuple(coords), coords[0] * 4 + coords[1] * 2 + coords[2]


def _window(ref, col, d, size):
    start = pl.multiple_of(d * size, 8)
    return ref.at[:, pl.ds(start, size)] if col else ref.at[pl.ds(start, size), :]


ANY = pl.BlockSpec(memory_space=pl.ANY)


class _Exchange:
    def __init__(self, kind, arrays, cols):
        self.kind, self.arrays, self.cols = kind, list(arrays), list(cols)
        n = len(self.arrays)
        if kind == "gather":
            self.sizes = [a.shape[1] if c else a.shape[0] for a, c in zip(self.arrays, cols)]
            self.out_shape = [jax.ShapeDtypeStruct((a.shape[0], a.shape[1] * N_DEV) if c else (a.shape[0] * N_DEV, a.shape[1]),
                                                   a.dtype) for a, c in zip(self.arrays, cols)]
        else:
            self.sizes = [a.shape[1] // N_DEV if c else a.shape[0] // N_DEV for a, c in zip(self.arrays, cols)]
            self.out_shape = [jax.ShapeDtypeStruct((N_DEV, a.shape[0], s) if c else (N_DEV, s, a.shape[1]), a.dtype)
                              for a, c, s in zip(self.arrays, cols, self.sizes)]
        self.scratch = [pltpu.SemaphoreType.DMA((n, N_DEV - 1)), pltpu.SemaphoreType.DMA((n, N_DEV - 1)),
                        pltpu.SemaphoreType.DMA((n,))]
        self.result = None

    def _src(self, ins, w, d):
        return ins[w] if self.kind == "gather" else _window(ins[w], self.cols[w], d, self.sizes[w])

    def _dst(self, outs, w, d):
        return _window(outs[w], self.cols[w], d, self.sizes[w]) if self.kind == "gather" else outs[w].at[d]

    def _copies(self, ins, outs, send, recv, loc):
        me = _my_id()
        n = len(self.arrays)
        two_level = self.kind == "gather"
        local = [pltpu.make_async_copy(self._src(ins, w, me), self._dst(outs, w, me), loc.at[w]) for w in range(n)]
        sends, arrivals, forwards, handed = [], [], [], []
        sibling, _ = _peer(1)
        for k in range(1, N_DEV):
            peer, peer_id = _peer(k)
            for w in range(n):
                sems = dict(send_sem=send.at[w, k - 1], recv_sem=recv.at[w, k - 1])
                if two_level and k > 1 and k % 2 == 1:
                    _, via_id = _peer(k - 1)
                    forwards.append(pltpu.make_async_remote_copy(
                        src_ref=self._dst(outs, w, via_id), dst_ref=self._dst(outs, w, via_id), device_id=sibling,
                        device_id_type=MESH, **sems))
                    handed.append(pltpu.make_async_remote_copy(
                        src_ref=self._dst(outs, w, peer_id), dst_ref=self._dst(outs, w, peer_id), device_id=sibling,
                        device_id_type=MESH, **sems))
                    continue
                at = dict(device_id=peer, device_id_type=MESH, **sems)
                sends.append(pltpu.make_async_remote_copy(src_ref=self._src(ins, w, peer_id), dst_ref=self._dst(outs, w, me), **at))
                arrivals.append(pltpu.make_async_remote_copy(src_ref=self._src(ins, w, me), dst_ref=self._dst(outs, w, peer_id), **at))
        return local, sends, arrivals, forwards, handed

    def start(self, ins, outs, send, recv, loc):
        local, sends, _, _, _ = self._copies(ins, outs, send, recv, loc)
        for cp in local + sends:
            cp.start()

    def wait(self, ins, outs, send, recv, loc):
        local, sends, arrivals, forwards, handed = self._copies(ins, outs, send, recv, loc)
        for cp in arrivals:
            cp.wait_recv()
        for cp in forwards:
            cp.start()
        for cp in handed:
            cp.wait_recv()
        for cp in sends + forwards:
            cp.wait_send()
        for cp in local:
            cp.wait()

    def run_alone(self, name):
        n = len(self.arrays)

        def body(*refs):
            self.start(refs[:n], refs[n:2 * n], *refs[2 * n:])
            self.wait(refs[:n], refs[n:2 * n], *refs[2 * n:])

        self.result = pl.pallas_call(body, name=name, in_specs=[ANY] * n, out_specs=[ANY] * n, out_shape=self.out_shape,
                                     scratch_shapes=self.scratch)(*self.arrays)
        return self.result


def _call(body, args, *, name, grid, in_specs, out_specs, out_shape, scratch_shapes=(), sem, rider=None):
    if rider is None:
        return pl.pallas_call(body, name=name, grid=grid, in_specs=in_specs, out_specs=out_specs, out_shape=out_shape,
                              scratch_shapes=list(scratch_shapes), compiler_params=_params(*sem))(*args)
    n_in, n_out, n_scr, n_r = len(in_specs), len(out_shape), len(scratch_shapes), len(rider.arrays)

    def both(*refs):
        ins, r_in = refs[:n_in], refs[n_in:n_in + n_r]
        outs, r_out = refs[n_in + n_r:n_in + n_r + n_out], refs[n_in + n_r + n_out:n_in + 2 * n_r + n_out]
        scr, sems = refs[n_in + 2 * n_r + n_out:n_in + 2 * n_r + n_out + n_scr], refs[n_in + 2 * n_r + n_out + n_scr:]
        first = functools.reduce(lambda p, q: p & q, [pl.program_id(d) == 0 for d in range(len(grid))])
        last = functools.reduce(lambda p, q: p & q, [pl.program_id(d) == g - 1 for d, g in enumerate(grid)])

        @pl.when(first)
        def _():
            rider.start(r_in, r_out, *sems)

        body(*ins, *outs, *scr)

        @pl.when(last)
        def _():
            rider.wait(r_in, r_out, *sems)

    res = pl.pallas_call(
        both, name=name, grid=grid, in_specs=list(in_specs) + [ANY] * n_r, out_specs=list(out_specs) + [ANY] * n_r,
        out_shape=list(out_shape) + rider.out_shape, scratch_shapes=list(scratch_shapes) + rider.scratch,
        compiler_params=_params(*(("arbitrary",) * len(grid))))(*args, *rider.arrays)
    rider.result = res[n_out:]
    return res[:n_out]


def _rms_mm(h, g, w, *, nt, out_dtype, save_n, name, rider=None):
    T, D = h.shape
    P = w.shape[0]
    N = w.shape[1] if nt else w.shape[2]
    tm = _tile(T, 1024, 512, 256, 128)
    tn = _tile(N, 2816, 1792, 2048, 1408, 1024, 512, 256, 128)

    def body(h_ref, g_ref, w_ref, y_ref, *rest):
        n_scr = rest[-1]

        @pl.when((pl.program_id(1) == 0) & (pl.program_id(2) == 0))
        def _():
            x = h_ref[...]
            n = (x * _rstd(x) * g_ref[...]).astype(BF16)
            n_scr[...] = n
            if save_n:
                rest[0][...] = n

        wt = w_ref[...]
        y = _dot_nt(n_scr[...], wt) if nt else _dot(n_scr[...], wt)
        y_ref[...] = y.astype(out_dtype)

    w_spec = (pl.BlockSpec((None, tn, D), lambda i, p, j: (p, j, 0)) if nt
              else pl.BlockSpec((None, D, tn), lambda i, p, j: (p, 0, j)))
    out_shape = [jax.ShapeDtypeStruct((P, T, N), out_dtype)]
    out_specs = [pl.BlockSpec((None, tm, tn), lambda i, p, j: (p, i, j))]
    if save_n:
        out_shape.append(jax.ShapeDtypeStruct((T, D), BF16))
        out_specs.append(pl.BlockSpec((tm, D), lambda i, p, j: (i, 0)))
    res = _call(
        body, (h, g, w), name=name, grid=(T // tm, P, N // tn),
        in_specs=[pl.BlockSpec((tm, D), lambda i, p, j: (i, 0)), pl.BlockSpec((1, D), lambda i, p, j: (0, 0)), w_spec],
        out_specs=out_specs, out_shape=out_shape, scratch_shapes=[pltpu.VMEM((tm, D), BF16)],
        sem=("parallel", "arbitrary", "arbitrary"), rider=rider)
    return res if save_n else res[0]


def _ffn_down(gu, wd, h, gpost, *, name, rider=None):
    _, T, F = gu.shape
    D = wd.shape[1]
    tm = _tile(T, 512, 256, 128)
    tk = _tile(F, 2816, 1408, 1024, 512, 256, 128)
    nk = F // tk

    def body(g_ref, u_ref, wd_ref, h_ref, gp_ref, hn_ref, f_ref, acc):
        k = pl.program_id(1)

        @pl.when(k == 0)
        def _():
            acc[...] = jnp.zeros_like(acc)

        gt = g_ref[...].astype(F32)
        a = (gt * jax.nn.sigmoid(gt) * u_ref[...].astype(F32)).astype(BF16)
        acc[...] += _dot(a, wd_ref[...])

        @pl.when(k == nk - 1)
        def _():
            f = acc[...]
            f_ref[...] = f
            hn_ref[...] = h_ref[...] + 0.5 * (f * _rstd(f) * gp_ref[...])

    return _call(
        body, (gu, gu, wd, h, gpost), name=name, grid=(T // tm, nk),
        in_specs=[pl.BlockSpec((None, tm, tk), lambda i, k: (0, i, k)), pl.BlockSpec((None, tm, tk), lambda i, k: (1, i, k)),
                  pl.BlockSpec((tk, D), lambda i, k: (k, 0)), pl.BlockSpec((tm, D), lambda i, k: (i, 0)),
                  pl.BlockSpec((1, D), lambda i, k: (0, 0))],
        out_specs=[pl.BlockSpec((tm, D), lambda i, k: (i, 0)), pl.BlockSpec((tm, D), lambda i, k: (i, 0))],
        out_shape=[jax.ShapeDtypeStruct((T, D), F32), jax.ShapeDtypeStruct((T, D), F32)],
        scratch_shapes=[pltpu.VMEM((tm, D), F32)], sem=("parallel", "arbitrary"), rider=rider)


def _ffn_down_bwd(dh, f, gpost, wd, gu, *, name, rider=None):
    _, T, F = gu.shape
    D = wd.shape[1]
    tm = _tile(T, 256, 128)
    tn = _tile(F, 2816, 1408, 1024, 512, 256, 128)

    def body(dh_ref, f_ref, gp_ref, wd_ref, g_ref, u_ref, dgu_ref, df_ref, dgp_ref, df_scr):
        i, j = pl.program_id(0), pl.program_id(1)

        @pl.when((i == 0) & (j == 0))
        def _():
            dgp_ref[...] = jnp.zeros_like(dgp_ref)

        @pl.when(j == 0)
        def _():
            x = f_ref[...]
            dx, dgn = _rms_bwd(0.5 * dh_ref[...], x, _rstd(x), gp_ref[...])
            dgp_ref[...] += _colsum(dgn)
            df = dx.astype(BF16)
            df_scr[...] = df
            df_ref[...] = df

        dact = _dot_nt(df_scr[...], wd_ref[...])
        gt = g_ref[...].astype(F32)
        ut = u_ref[...].astype(F32)
        sg = jax.nn.sigmoid(gt)
        dgu_ref[0] = (dact * ut * (sg * (1.0 + gt * (1.0 - sg)))).astype(BF16)
        dgu_ref[1] = (dact * (gt * sg)).astype(BF16)

    return _call(
        body, (dh, f, gpost, wd, gu, gu), name=name, grid=(T // tm, F // tn),
        in_specs=[pl.BlockSpec((tm, D), lambda i, j: (i, 0)), pl.BlockSpec((tm, D), lambda i, j: (i, 0)),
                  pl.BlockSpec((1, D), lambda i, j: (0, 0)), pl.BlockSpec((tn, D), lambda i, j: (j, 0)),
                  pl.BlockSpec((None, tm, tn), lambda i, j: (0, i, j)), pl.BlockSpec((None, tm, tn), lambda i, j: (1, i, j))],
        out_specs=[pl.BlockSpec((2, tm, tn), lambda i, j: (0, i, j)), pl.BlockSpec((tm, D), lambda i, j: (i, 0)),
                   pl.BlockSpec((1, D), lambda i, j: (0, 0))],
        out_shape=[jax.ShapeDtypeStruct((2, T, F), BF16), jax.ShapeDtypeStruct((T, D), BF16),
                   jax.ShapeDtypeStruct((1, D), F32)],
        scratch_shapes=[pltpu.VMEM((tm, D), BF16)], sem=("arbitrary", "arbitrary"), rider=rider)


def _mm_rmsbwd(a, b, dh_in, h, g, *, nt, name, rider=None):
    P, T, K = a.shape
    D = h.shape[1]
    tm = _tile(T, 1024, 512, 256, 128)
    tk = _tile(K, 1792, 1408, 2048, 1024, 512, 256, 128)
    nk = K // tk

    def body(a_ref, b_ref, dh_ref, h_ref, g_ref, out_ref, dg_ref, acc):
        i, p, k = pl.program_id(0), pl.program_id(1), pl.program_id(2)

        @pl.when((i == 0) & (p == 0) & (k == 0))
        def _():
            dg_ref[...] = jnp.zeros_like(dg_ref)

        @pl.when((p == 0) & (k == 0))
        def _():
            acc[...] = jnp.zeros_like(acc)

        acc[...] += _dot_nt(a_ref[...], b_ref[...]) if nt else _dot(a_ref[...], b_ref[...])

        @pl.when((p == P - 1) & (k == nk - 1))
        def _():
            x = h_ref[...]
            dx, dgn = _rms_bwd(acc[...], x, _rstd(x), g_ref[...])
            dg_ref[...] += _colsum(dgn)
            out_ref[...] = dh_ref[...] + dx

    b_spec = (pl.BlockSpec((None, D, tk), lambda i, p, k: (p, 0, k)) if nt
              else pl.BlockSpec((None, tk, D), lambda i, p, k: (p, k, 0)))
    return _call(
        body, (a, b, dh_in, h, g), name=name, grid=(T // tm, P, nk),
        in_specs=[pl.BlockSpec((None, tm, tk), lambda i, p, k: (p, i, k)), b_spec,
                  pl.BlockSpec((tm, D), lambda i, p, k: (i, 0)), pl.BlockSpec((tm, D), lambda i, p, k: (i, 0)),
                  pl.BlockSpec((1, D), lambda i, p, k: (0, 0))],
        out_specs=[pl.BlockSpec((tm, D), lambda i, p, k: (i, 0)), pl.BlockSpec((1, D), lambda i, p, k: (0, 0))],
        out_shape=[jax.ShapeDtypeStruct((T, D), F32), jax.ShapeDtypeStruct((1, D), F32)],
        scratch_shapes=[pltpu.VMEM((tm, D), F32)], sem=("arbitrary", "arbitrary", "arbitrary"), rider=rider)


def _mm_tn(a, b, *, swiglu=False, b_plane=0, name):
    T, N = b.shape[-2:]
    K = a.shape[2]
    P = 1 if swiglu else a.shape[0]
    tk = _tile(K, 2816, 1024, 512, 256, 128)
    tn = _tile(N, 1792, 1024, 512, 256, 128)
    tt = _tile(T, 512, 256, 128)
    nt_ = T // tt
    nkb = K // tk

    def body(*refs):
        if swiglu:
            g_ref, u_ref, b_ref = refs[:3]
        else:
            a_ref, b_ref = refs[:2]
        out_ref, acc = refs[-2], refs[-1]
        t = pl.program_id(3)

        @pl.when(t == 0)
        def _():
            acc[...] = jnp.zeros_like(acc)

        if swiglu:
            gt = g_ref[...].astype(F32)
            at = (gt * jax.nn.sigmoid(gt) * u_ref[...].astype(F32)).astype(BF16)
        else:
            at = a_ref[...].astype(BF16)
        acc[...] += _dot_tn(at, b_ref[...].astype(BF16))

        @pl.when(t == nt_ - 1)
        def _():
            out_ref[...] = acc[...].astype(BF16)

    if swiglu:
        a_specs = [pl.BlockSpec((None, tt, tk), lambda p, i, j, t: (0, t, i)),
                   pl.BlockSpec((None, tt, tk), lambda p, i, j, t: (1, t, i))]
        a_args = [a, a]
    else:
        a_specs = [pl.BlockSpec((None, tt, tk), lambda p, i, j, t: (p, t, i))]
        a_args = [a]
    if b.ndim == 3:
        in_specs = a_specs + [pl.BlockSpec((None, tt, tn), lambda p, i, j, t: (b_plane, t, j))]
    else:
        in_specs = a_specs + [pl.BlockSpec((tt, tn), lambda p, i, j, t: (t, j))]
    return pl.pallas_call(
        body, name=name, grid=(P, nkb, N // tn, nt_),
        in_specs=in_specs,
        out_specs=pl.BlockSpec((tk, tn), lambda p, i, j, t: (p * nkb + i, j)),
        out_shape=jax.ShapeDtypeStruct((P * K, N), BF16),
        scratch_shapes=[pltpu.VMEM((tk, tn), F32)],
        compiler_params=_params("parallel", "parallel", "parallel", "arbitrary"),
    )(*a_args, b)


def _row(D):
    return pl.BlockSpec((1, D), lambda i: (0, 0))


def _full(shape):
    return pl.BlockSpec(shape, lambda i: (0,) * len(shape))


def _merge_out(ys, proj, bgate, wbrs, wo, h, gpost, *, name):
    T, D = h.shape
    tm = _tile(T, 512, 256, 128)

    def body(ya, yb, yc, yd, g0, g1, g2, g3, bg_ref, wa, wb, wc, wd_, wo_ref, h_ref, gp_ref, hn_ref, o_ref, mg_ref):
        merged = jnp.zeros((tm, D), F32)
        for b, (y_ref, gt_ref, w_ref) in enumerate(zip((ya, yb, yc, yd), (g0, g1, g2, g3), (wa, wb, wc, wd_))):
            gate = jax.nn.sigmoid(gt_ref[...].astype(F32) + bg_ref[b:b + 1, :])
            merged = merged + gate * _dot(y_ref[...], w_ref[...])
        mb = merged.astype(BF16)
        mg_ref[...] = mb
        o = _dot(mb, wo_ref[...])
        o_ref[...] = o
        hn_ref[...] = h_ref[...] + o * _rstd(o) * gp_ref[...]

    tok = lambda w: pl.BlockSpec((tm, w), lambda i: (i, 0))
    gate_specs = [pl.BlockSpec((tm, D), lambda i, b=b: (i, b)) for b in range(N_BRANCH)]
    return pl.pallas_call(
        body, name=name, grid=(T // tm,),
        in_specs=[tok(A_WIDTH), tok(B_WIDTH), tok(C_WIDTH), tok(D_WIDTH)] + gate_specs
        + [_full((N_BRANCH, D))] + [_full(w.shape) for w in wbrs] + [_full((D, D)), tok(D), _row(D)],
        out_specs=[tok(D), tok(D), tok(D)],
        out_shape=[jax.ShapeDtypeStruct((T, D), F32), jax.ShapeDtypeStruct((T, D), F32), jax.ShapeDtypeStruct((T, D), BF16)],
        compiler_params=_params("parallel"),
    )(*ys, proj, proj, proj, proj, bgate, *wbrs, wo, h, gpost)


def _merge_out_bwd(dh, o, gpost, wo, ys, proj, bgate, wbrs, *, name):
    T, D = o.shape
    tm = _tile(T, 256, 128)
    widths = (A_WIDTH, B_WIDTH, C_WIDTH, D_WIDTH)

    def body(dh_ref, o_ref, gp_ref, wo_ref, ya, yb, yc, yd, g0, g1, g2, g3, bg_ref, wa, wb, wc, wd_,
             do_ref, dz_ref, dya, dyb, dyc, dyd, dgt_ref, dgp_ref, dbg_ref):
        @pl.when(pl.program_id(0) == 0)
        def _():
            dgp_ref[...] = jnp.zeros_like(dgp_ref)
            dbg_ref[...] = jnp.zeros_like(dbg_ref)

        x = o_ref[...]
        do, dgn = _rms_bwd(dh_ref[...], x, _rstd(x), gp_ref[...])
        dgp_ref[...] += _colsum(dgn)
        dob = do.astype(BF16)
        do_ref[...] = dob
        dmerged = _dot_nt(dob, wo_ref[...])
        for b, (y_ref, gt_ref, w_ref, dy_ref) in enumerate(zip((ya, yb, yc, yd), (g0, g1, g2, g3), (wa, wb, wc, wd_),
                                                               (dya, dyb, dyc, dyd))):
            gate = jax.nn.sigmoid(gt_ref[...].astype(F32) + bg_ref[b:b + 1, :])
            z = _dot(y_ref[...], w_ref[...])
            dz = (dmerged * gate).astype(BF16)
            dz_ref[b] = dz
            dy_ref[...] = _dot_nt(dz, w_ref[...]).astype(BF16)
            dgate = dmerged * z * gate * (1.0 - gate)
            dgt_ref[:, b * D:(b + 1) * D] = dgate.astype(BF16)
            dbg_ref[b:b + 1, :] += _colsum(dgate)

    tok = lambda w: pl.BlockSpec((tm, w), lambda i: (i, 0))
    gate_specs = [pl.BlockSpec((tm, D), lambda i, b=b: (i, b)) for b in range(N_BRANCH)]
    return pl.pallas_call(
        body, name=name, grid=(T // tm,),
        in_specs=[tok(D), tok(D), _row(D), _full((D, D))] + [tok(w) for w in widths] + gate_specs
        + [_full((N_BRANCH, D))] + [_full(w.shape) for w in wbrs],
        out_specs=[tok(D), pl.BlockSpec((N_BRANCH, tm, D), lambda i: (0, i, 0))] + [tok(w) for w in widths]
        + [tok(N_BRANCH * D), _row(D), _full((N_BRANCH, D))],
        out_shape=[jax.ShapeDtypeStruct((T, D), BF16), jax.ShapeDtypeStruct((N_BRANCH, T, D), BF16)]
        + [jax.ShapeDtypeStruct((T, w), BF16) for w in widths]
        + [jax.ShapeDtypeStruct((T, N_BRANCH * D), BF16), jax.ShapeDtypeStruct((1, D), F32),
           jax.ShapeDtypeStruct((N_BRANCH, D), F32)],
        compiler_params=_params("arbitrary"),
    )(dh, o, gpost, wo, *ys, proj, proj, proj, proj, bgate, *wbrs)


def _ple(h, pe, ggate, wpg, wple, gpost, *, name):
    T, D = h.shape
    E = pe.shape[1]
    tm = _tile(T, 512, 256, 128)

    def body(h_ref, p_ref, gg_ref, wpg_ref, wple_ref, gp_ref, out_ref):
        x = h_ref[...]
        n = (x * _rstd(x) * gg_ref[...]).astype(BF16)
        pg = jax.nn.sigmoid(_dot(n, wpg_ref[...]))
        e = _dot(p_ref[...].astype(BF16), wple_ref[...])
        out_ref[...] = x + pg * (e * _rstd(e) * gp_ref[...])

    tok = lambda w: pl.BlockSpec((tm, w), lambda i: (i, 0))
    return pl.pallas_call(
        body, name=name, grid=(T // tm,),
        in_specs=[tok(D), tok(E), _row(D), _full((D, D)), _full((E, D)), _row(D)],
        out_specs=tok(D), out_shape=jax.ShapeDtypeStruct((T, D), F32),
        compiler_params=_params("parallel"),
    )(h, pe, ggate, wpg, wple, gpost)


def _ple_bwd(dh, h, pe, ggate, wpg, wple, gpost, *, name):
    T, D = h.shape
    E = pe.shape[1]
    tm = _tile(T, 256, 128)

    def body(dh_ref, h_ref, p_ref, gg_ref, wpg_ref, wple_ref, gp_ref, dhi_ref, de_ref, dpgl_ref, n_ref, dgg_ref, dgp_ref):
        @pl.when(pl.program_id(0) == 0)
        def _():
            dgg_ref[...] = jnp.zeros_like(dgg_ref)
            dgp_ref[...] = jnp.zeros_like(dgp_ref)

        dh = dh_ref[...]
        x = h_ref[...]
        r = _rstd(x)
        n = (x * r * gg_ref[...]).astype(BF16)
        n_ref[...] = n
        pg = jax.nn.sigmoid(_dot(n, wpg_ref[...]))
        e = _dot(p_ref[...].astype(BF16), wple_ref[...])
        re = _rstd(e)
        de, dgn = _rms_bwd(dh * pg, e, re, gp_ref[...])
        dgp_ref[...] += _colsum(dgn)
        de_ref[...] = de.astype(BF16)
        dpgl = (dh * (e * re * gp_ref[...]) * pg * (1.0 - pg)).astype(BF16)
        dpgl_ref[...] = dpgl
        dn = _dot_nt(dpgl, wpg_ref[...])
        dx, dgn2 = _rms_bwd(dn, x, r, gg_ref[...])
        dgg_ref[...] += _colsum(dgn2)
        dhi_ref[...] = dh + dx

    tok = lambda w: pl.BlockSpec((tm, w), lambda i: (i, 0))
    return pl.pallas_call(
        body, name=name, grid=(T // tm,),
        in_specs=[tok(D), tok(D), tok(E), _row(D), _full((D, D)), _full((E, D)), _row(D)],
        out_specs=[tok(D), tok(D), tok(D), tok(D), _row(D), _row(D)],
        out_shape=[jax.ShapeDtypeStruct((T, D), F32), jax.ShapeDtypeStruct((T, D), BF16), jax.ShapeDtypeStruct((T, D), BF16),
                   jax.ShapeDtypeStruct((T, D), BF16), jax.ShapeDtypeStruct((1, D), F32), jax.ShapeDtypeStruct((1, D), F32)],
        compiler_params=_params("arbitrary"),
    )(dh, h, pe, ggate, wpg, wple, gpost)


def _loss_head(y, target, *, name):
    T, D = y.shape
    tm = _tile(T, 512, 256, 128)

    def body(y_ref, t_ref, l_ref, dy_ref):
        @pl.when(pl.program_id(0) == 0)
        def _():
            l_ref[...] = jnp.zeros_like(l_ref)

        err = y_ref[...] - t_ref[...]
        dy_ref[...] = err / D
        l_ref[...] += 0.5 * jnp.sum(jnp.mean(err * err, axis=-1, keepdims=True), axis=0, keepdims=True)

    tok = pl.BlockSpec((tm, D), lambda i: (i, 0))
    return pl.pallas_call(
        body, name=name, grid=(T // tm,),
        in_specs=[tok, tok], out_specs=[_full((8, 128)), tok],
        out_shape=[jax.ShapeDtypeStruct((8, 128), F32), jax.ShapeDtypeStruct((T, D), F32)],
        compiler_params=_params("arbitrary"),
    )(y, target)


def _layout(D):
    off = {'gates': 0}
    off['a'] = N_BRANCH * D
    off['b'] = off['a'] + 3 * A_WIDTH
    off['c'] = off['b'] + 3 * B_WIDTH
    off['d'] = off['c'] + 2 * C_WIDTH
    off['af'] = off['d'] + D_WIDTH + 2 * D_KV_WIDTH
    off['end'] = off['af'] + AF_PAD
    return off


def _seq_spec(S, width, col):
    assert col % width == 0
    return pl.BlockSpec((None, S, width), lambda b: (b, 0, col // width))


def _split3(x):
    hi = x.astype(BF16)
    r1 = x - hi.astype(F32)
    mid = r1.astype(BF16)
    lo = (r1 - mid.astype(F32)).astype(BF16)
    return hi, mid, lo


def _fox_cumsum(af_ref, bf_ref, c_scr, ct_scr):
    S = af_ref.shape[0]
    cb = _tile(S, 256, 128)
    tril = (lax.broadcasted_iota(jnp.int32, (cb, cb), 0) >= lax.broadcasted_iota(jnp.int32, (cb, cb), 1)).astype(BF16)
    carry = jnp.zeros((1, AF_LANES), F32)
    for j in range(S // cb):
        rows = slice(j * cb, (j + 1) * cb)
        hi, mid, lo = _split3(jax.nn.log_sigmoid(af_ref[rows, :] + bf_ref[...]))
        cblk = _dot(tril, hi) + _dot(tril, mid) + _dot(tril, lo) + carry
        c_scr[rows, :] = cblk
        carry = cblk[cb - 1:cb, :]
    ct_scr[...] = c_scr[...].T


def _fox_probs(q_ref, k_ref, c_scr, ct_scr, h, i, bq):
    end = (i + 1) * bq
    qs, hs = slice(i * bq, end), slice(HEAD_DIM * h, HEAD_DIM * (h + 1))
    s = _dot_nt(q_ref[qs, hs], k_ref[0:end, hs]) * HEAD_DIM ** -0.5
    s = s + (c_scr[qs, h:h + 1] - ct_scr[h:h + 1, 0:end])
    row = i * bq + lax.broadcasted_iota(jnp.int32, (bq, end), 0)
    col = lax.broadcasted_iota(jnp.int32, (bq, end), 1)
    s = jnp.where(row >= col, s, NEG_INF)
    e = jnp.exp(s - jnp.max(s, axis=-1, keepdims=True))
    return e / jnp.sum(e, axis=-1, keepdims=True)


def _fox_fwd(proj3, af3, bfor, off, *, name, rider=None):
    Bn, S, _ = proj3.shape
    bq = _tile(S, 256, 128)

    def body(q_ref, k_ref, v_ref, af_ref, bf_ref, o_ref, c_scr, ct_scr):
        _fox_cumsum(af_ref, bf_ref, c_scr, ct_scr)
        for h in range(A_HEADS):
            hs = slice(HEAD_DIM * h, HEAD_DIM * (h + 1))
            for i in range(S // bq):
                p = _fox_probs(q_ref, k_ref, c_scr, ct_scr, h, i, bq)
                o_ref[i * bq:(i + 1) * bq, hs] = _dot(p.astype(BF16), v_ref[0:(i + 1) * bq, hs]).astype(BF16)

    return _call(
        body, (proj3, proj3, proj3, af3, bfor), name=name, grid=(Bn,),
        in_specs=[_seq_spec(S, A_WIDTH, off['a']), _seq_spec(S, A_WIDTH, off['a'] + A_WIDTH),
                  _seq_spec(S, A_WIDTH, off['a'] + 2 * A_WIDTH), _seq_spec(S, AF_LANES, 0), _full((1, AF_LANES))],
        out_specs=[_seq_spec(S, A_WIDTH, 0)], out_shape=[jax.ShapeDtypeStruct((Bn, S, A_WIDTH), BF16)],
        scratch_shapes=[pltpu.VMEM((S, AF_LANES), F32), pltpu.VMEM((AF_LANES, S), F32)],
        sem=("parallel",), rider=rider)[0]


def _fox_bwd(proj3, af3, bfor, dya3, off, *, name, rider=None):
    Bn, S, _ = proj3.shape
    bq = _tile(S, 256, 128)
    cb = _tile(S, 256, 128)
    scale = HEAD_DIM ** -0.5

    def body(q_ref, k_ref, v_ref, af_ref, bf_ref, do_ref, dqkv_ref, da_ref, dbf_ref,
             c_scr, ct_scr, dk_scr, dv_scr, dc_scr, dct_scr):
        @pl.when(pl.program_id(0) == 0)
        def _():
            dbf_ref[...] = jnp.zeros_like(dbf_ref)

        _fox_cumsum(af_ref, bf_ref, c_scr, ct_scr)
        dk_scr[...] = jnp.zeros_like(dk_scr)
        dv_scr[...] = jnp.zeros_like(dv_scr)
        dc_scr[...] = jnp.zeros_like(dc_scr)
        dct_scr[...] = jnp.zeros_like(dct_scr)
        for h in range(A_HEADS):
            hs = slice(HEAD_DIM * h, HEAD_DIM * (h + 1))
            for i in range(S // bq):
                end = (i + 1) * bq
                qs = slice(i * bq, end)
                p = _fox_probs(q_ref, k_ref, c_scr, ct_scr, h, i, bq)
                doh = do_ref[qs, hs]
                dp = _dot_nt(doh, v_ref[0:end, hs])
                ds = p * (dp - jnp.sum(p * dp, axis=-1, keepdims=True))
                dsb = ds.astype(BF16)
                dqkv_ref[qs, hs] = (_dot(dsb, k_ref[0:end, hs]) * scale).astype(BF16)
                dk_scr[0:end, hs] += _dot_tn(dsb, q_ref[qs, hs]) * scale
                dv_scr[0:end, hs] += _dot_tn(p.astype(BF16), doh)
                dc_scr[qs, h:h + 1] += jnp.sum(ds, axis=-1, keepdims=True)
                dct_scr[h:h + 1, 0:end] += -jnp.sum(ds, axis=0, keepdims=True)
        dqkv_ref[:, A_WIDTH:2 * A_WIDTH] = dk_scr[...].astype(BF16)
        dqkv_ref[:, 2 * A_WIDTH:3 * A_WIDTH] = dv_scr[...].astype(BF16)
        dc_scr[...] += dct_scr[...].T
        triu = (lax.broadcasted_iota(jnp.int32, (cb, cb), 0) <= lax.broadcasted_iota(jnp.int32, (cb, cb), 1)).astype(BF16)
        carry = jnp.zeros((1, AF_LANES), F32)
        dbf = jnp.zeros((1, AF_LANES), F32)
        for j in reversed(range(S // cb)):
            rows = slice(j * cb, (j + 1) * cb)
            hi, mid, lo = _split3(dc_scr[rows, :])
            dlf = _dot(triu, hi) + _dot(triu, mid) + _dot(triu, lo) + carry
            carry = dlf[0:1, :]
            da = dlf * jax.nn.sigmoid(-(af_ref[rows, :] + bf_ref[...]))
            dbf = dbf + _colsum(da)
            da_ref[rows, 0:AF_LANES] = da.astype(BF16)
        da_ref[:, AF_LANES:AF_PAD] = jnp.zeros((S, AF_PAD - AF_LANES), BF16)
        dbf_ref[...] += dbf

    return _call(
        body, (proj3, proj3, proj3, af3, bfor, dya3), name=name, grid=(Bn,),
        in_specs=[_seq_spec(S, A_WIDTH, off['a']), _seq_spec(S, A_WIDTH, off['a'] + A_WIDTH),
                  _seq_spec(S, A_WIDTH, off['a'] + 2 * A_WIDTH), _seq_spec(S, AF_LANES, 0), _full((1, AF_LANES)),
                  _seq_spec(S, A_WIDTH, 0)],
        out_specs=[_seq_spec(S, 3 * A_WIDTH, 0), _seq_spec(S, AF_PAD, 0), _full((1, AF_LANES))],
        out_shape=[jax.ShapeDtypeStruct((Bn, S, 3 * A_WIDTH), BF16), jax.ShapeDtypeStruct((Bn, S, AF_PAD), BF16),
                   jax.ShapeDtypeStruct((1, AF_LANES), F32)],
        scratch_shapes=[pltpu.VMEM((S, AF_LANES), F32), pltpu.VMEM((AF_LANES, S), F32), pltpu.VMEM((S, A_WIDTH), F32),
                        pltpu.VMEM((S, A_WIDTH), F32), pltpu.VMEM((S, AF_LANES), F32), pltpu.VMEM((AF_LANES, S), F32)],
        sem=("arbitrary",), rider=rider)


def _shift_down(z, s):
    if s == 0:
        return z
    row = lax.broadcasted_iota(jnp.int32, z.shape, 0)
    return jnp.where(row >= s, pltpu.roll(z, s, 0), 0.0)


def _shift_up(z, s):
    if s == 0:
        return z
    n = z.shape[0]
    row = lax.broadcasted_iota(jnp.int32, z.shape, 0)
    return jnp.where(row < n - s, pltpu.roll(z, n - s, 0), 0.0)


def _conv_fwd(z, w_ref, K):
    acc = jnp.zeros_like(z)
    for k in range(K):
        acc = acc + w_ref[k:k + 1, :] * _shift_down(z, K - 1 - k)
    return acc


def _conv_bwd(dy, z, w_ref, dw_ref, K):
    dz = jnp.zeros_like(z)
    for k in range(K):
        dz = dz + w_ref[k:k + 1, :] * _shift_up(dy, K - 1 - k)
        dw_ref[k:k + 1, :] += _colsum(dy * _shift_down(z, K - 1 - k))
    return dz


def _sconv_fwd(proj3, w, off, *, name):
    Bn, S, _ = proj3.shape

    def body(bg_ref, cg_ref, xb_ref, w_ref, o_ref):
        z = cg_ref[...].astype(F32) * xb_ref[...].astype(F32)
        o_ref[...] = (bg_ref[...].astype(F32) * _conv_fwd(z, w_ref, SHORT_CONV)).astype(BF16)

    return pl.pallas_call(
        body, name=name, grid=(Bn,),
        in_specs=[_seq_spec(S, B_WIDTH, off['b'] + j * B_WIDTH) for j in range(3)] + [_full((SHORT_CONV, B_WIDTH))],
        out_specs=_seq_spec(S, B_WIDTH, 0), out_shape=jax.ShapeDtypeStruct((Bn, S, B_WIDTH), BF16),
        compiler_params=_params("parallel"),
    )(proj3, proj3, proj3, w)


def _sconv_bwd(proj3, w, dyb3, off, *, name):
    Bn, S, _ = proj3.shape

    def body(bg_ref, cg_ref, xb_ref, w_ref, do_ref, din_ref, dw_ref):
        @pl.when(pl.program_id(0) == 0)
        def _():
            dw_ref[...] = jnp.zeros_like(dw_ref)

        cg, xb = cg_ref[...].astype(F32), xb_ref[...].astype(F32)
        z = cg * xb
        do = do_ref[...].astype(F32)
        din_ref[:, 0:B_WIDTH] = (do * _conv_fwd(z, w_ref, SHORT_CONV)).astype(BF16)
        dz = _conv_bwd(do * bg_ref[...].astype(F32), z, w_ref, dw_ref, SHORT_CONV)
        din_ref[:, B_WIDTH:2 * B_WIDTH] = (dz * xb).astype(BF16)
        din_ref[:, 2 * B_WIDTH:3 * B_WIDTH] = (dz * cg).astype(BF16)

    return pl.pallas_call(
        body, name=name, grid=(Bn,),
        in_specs=[_seq_spec(S, B_WIDTH, off['b'] + j * B_WIDTH) for j in range(3)]
        + [_full((SHORT_CONV, B_WIDTH)), _seq_spec(S, B_WIDTH, 0)],
        out_specs=[_seq_spec(S, 3 * B_WIDTH, 0), _full((SHORT_CONV, B_WIDTH))],
        out_shape=[jax.ShapeDtypeStruct((Bn, S, 3 * B_WIDTH), BF16), jax.ShapeDtypeStruct((SHORT_CONV, B_WIDTH), F32)],
        compiler_params=_params("arbitrary"),
    )(proj3, proj3, proj3, w, dyb3)


def _cconv_pre(cin_ref, w_ref, cb_ref):
    x = cin_ref[...].astype(F32)
    a, gt = x[:, 0:C_WIDTH], x[:, C_WIDTH:2 * C_WIDTH]
    sg = jax.nn.sigmoid(gt)
    glu = a * sg
    y0 = _conv_fwd(glu, w_ref, CONF_CONV) + cb_ref[...]
    mu = jnp.mean(y0, axis=-1, keepdims=True)
    xc = y0 - mu
    rs = lax.rsqrt(jnp.mean(xc * xc, axis=-1, keepdims=True) + EPS)
    return a, sg, glu, xc * rs, rs


def _cconv_fwd(proj3, w, cbias, lg, lb, off, *, name):
    Bn, S, _ = proj3.shape

    def body(cin_ref, w_ref, cb_ref, lg_ref, lb_ref, o_ref):
        _, _, _, xh, _ = _cconv_pre(cin_ref, w_ref, cb_ref)
        ln = xh * lg_ref[...] + lb_ref[...]
        o_ref[...] = (ln * jax.nn.sigmoid(ln)).astype(BF16)

    return pl.pallas_call(
        body, name=name, grid=(Bn,),
        in_specs=[_seq_spec(S, 2 * C_WIDTH, off['c']), _full((CONF_CONV, C_WIDTH)), _full((1, C_WIDTH)),
                  _full((1, C_WIDTH)), _full((1, C_WIDTH))],
        out_specs=_seq_spec(S, C_WIDTH, 0), out_shape=jax.ShapeDtypeStruct((Bn, S, C_WIDTH), BF16),
        compiler_params=_params("parallel"),
    )(proj3, w, cbias, lg, lb)


def _cconv_bwd(proj3, w, cbias, lg, lb, dyc3, off, *, name):
    Bn, S, _ = proj3.shape

    def body(cin_ref, w_ref, cb_ref, lg_ref, lb_ref, do_ref, din_ref, dw_ref, dcb_ref, dlg_ref, dlb_ref):
        @pl.when(pl.program_id(0) == 0)
        def _():
            for r in (dw_ref, dcb_ref, dlg_ref, dlb_ref):
                r[...] = jnp.zeros_like(r)

        a, sg, glu, xh, rs = _cconv_pre(cin_ref, w_ref, cb_ref)
        ln = xh * lg_ref[...] + lb_ref[...]
        sl = jax.nn.sigmoid(ln)
        dln = do_ref[...].astype(F32) * (sl * (1.0 + ln * (1.0 - sl)))
        dlg_ref[...] += _colsum(dln * xh)
        dlb_ref[...] += _colsum(dln)
        dxh = dln * lg_ref[...]
        dy0 = rs * (dxh - jnp.mean(dxh, axis=-1, keepdims=True) - xh * jnp.mean(dxh * xh, axis=-1, keepdims=True))
        dcb_ref[...] += _colsum(dy0)
        dglu = _conv_bwd(dy0, glu, w_ref, dw_ref, CONF_CONV)
        din_ref[:, 0:C_WIDTH] = (dglu * sg).astype(BF16)
        din_ref[:, C_WIDTH:2 * C_WIDTH] = (dglu * a * sg * (1.0 - sg)).astype(BF16)

    vec = _full((1, C_WIDTH))
    return pl.pallas_call(
        body, name=name, grid=(Bn,),
        in_specs=[_seq_spec(S, 2 * C_WIDTH, off['c']), _full((CONF_CONV, C_WIDTH)), vec, vec, vec, _seq_spec(S, C_WIDTH, 0)],
        out_specs=[_seq_spec(S, 2 * C_WIDTH, 0), _full((CONF_CONV, C_WIDTH)), vec, vec, vec],
        out_shape=[jax.ShapeDtypeStruct((Bn, S, 2 * C_WIDTH), BF16), jax.ShapeDtypeStruct((CONF_CONV, C_WIDTH), F32)]
        + [jax.ShapeDtypeStruct((1, C_WIDTH), F32)] * 3,
        compiler_params=_params("arbitrary"),
    )(proj3, w, cbias, lg, lb, dyc3)


def _swa_band(x_ref, g, nb):
    xb = x_ref[:, HEAD_DIM * g:HEAD_DIM * (g + 1)].reshape(nb, Q_BLOCK, HEAD_DIM)
    prev = jnp.concatenate([jnp.zeros((1, Q_BLOCK, HEAD_DIM), xb.dtype), xb[:-1]], axis=0)
    return jnp.concatenate([prev, xb], axis=1)


def _swa_probs(q_ref, kband, bias_ref, sk_ref, h, nb):
    qh = q_ref[:, HEAD_DIM * h:HEAD_DIM * (h + 1)].reshape(nb, Q_BLOCK, HEAD_DIM)
    s = jnp.einsum('nqd,nsd->nqs', qh, kband, preferred_element_type=F32) * HEAD_DIM ** -0.5 + bias_ref[h][None]
    shape = (nb, Q_BLOCK, 2 * Q_BLOCK)
    n = lax.broadcasted_iota(jnp.int32, shape, 0)
    dist = lax.broadcasted_iota(jnp.int32, shape, 1) + Q_BLOCK - lax.broadcasted_iota(jnp.int32, shape, 2)
    col = lax.broadcasted_iota(jnp.int32, shape, 2)
    valid = (dist >= 0) & (dist < WINDOW) & ((n > 0) | (col >= Q_BLOCK))
    s = jnp.where(valid, s, NEG_INF)
    sink = sk_ref[h:h + 1, 0:1].reshape(1, 1, 1)
    m = jnp.maximum(jnp.max(s, axis=-1, keepdims=True), sink)
    e = jnp.exp(s - m)
    es = jnp.exp(sink - m)
    den = jnp.sum(e, axis=-1, keepdims=True) + es
    return qh, e / den, es / den


def _swa_fwd(proj3, band_bias, sinks, off, *, name):
    Bn, S, _ = proj3.shape
    nb = S // Q_BLOCK

    def body(q_ref, k_ref, v_ref, bias_ref, sk_ref, o_ref):
        for g in range(D_KV_HEADS):
            kband, vband = _swa_band(k_ref, g, nb), _swa_band(v_ref, g, nb)
            for h in range(g * D_GROUP, (g + 1) * D_GROUP):
                _, p, _ = _swa_probs(q_ref, kband, bias_ref, sk_ref, h, nb)
                out = jnp.einsum('nqs,nsd->nqd', p.astype(BF16), vband, preferred_element_type=F32)
                o_ref[:, HEAD_DIM * h:HEAD_DIM * (h + 1)] = out.reshape(S, HEAD_DIM).astype(BF16)

    kcol = off['d'] + D_WIDTH
    return pl.pallas_call(
        body, name=name, grid=(Bn,),
        in_specs=[_seq_spec(S, D_WIDTH, off['d']), _seq_spec(S, D_KV_WIDTH, kcol), _seq_spec(S, D_KV_WIDTH, kcol + D_KV_WIDTH),
                  _full((D_Q_HEADS, Q_BLOCK, 2 * Q_BLOCK)), _full((D_Q_HEADS, 128))],
        out_specs=_seq_spec(S, D_WIDTH, 0), out_shape=jax.ShapeDtypeStruct((Bn, S, D_WIDTH), BF16),
        compiler_params=_params("parallel"),
    )(proj3, proj3, proj3, band_bias, sinks)


def _swa_bwd(proj3, band_bias, sinks, dyd3, off, *, name):
    Bn, S, _ = proj3.shape
    nb = S // Q_BLOCK
    scale = HEAD_DIM ** -0.5

    def body(q_ref, k_ref, v_ref, bias_ref, sk_ref, do_ref, dqkv_ref, dband_ref, dsk_ref):
        @pl.when(pl.program_id(0) == 0)
        def _():
            dband_ref[...] = jnp.zeros_like(dband_ref)
            dsk_ref[...] = jnp.zeros_like(dsk_ref)

        def unband(acc):
            prev, cur = acc[:, 0:Q_BLOCK, :], acc[:, Q_BLOCK:2 * Q_BLOCK, :]
            nxt = jnp.concatenate([prev[1:], jnp.zeros((1, Q_BLOCK, HEAD_DIM), F32)], axis=0)
            return (cur + nxt).reshape(S, HEAD_DIM).astype(BF16)

        for g in range(D_KV_HEADS):
            kband, vband = _swa_band(k_ref, g, nb), _swa_band(v_ref, g, nb)
            dkb = jnp.zeros((nb, 2 * Q_BLOCK, HEAD_DIM), F32)
            dvb = jnp.zeros((nb, 2 * Q_BLOCK, HEAD_DIM), F32)
            for h in range(g * D_GROUP, (g + 1) * D_GROUP):
                hs = slice(HEAD_DIM * h, HEAD_DIM * (h + 1))
                qh, p, ps = _swa_probs(q_ref, kband, bias_ref, sk_ref, h, nb)
                doh = do_ref[:, hs].reshape(nb, Q_BLOCK, HEAD_DIM)
                dp = jnp.einsum('nqd,nsd->nqs', doh, vband, preferred_element_type=F32)
                delta = jnp.sum(p * dp, axis=-1, keepdims=True)
                ds = p * (dp - delta)
                dsink = jnp.sum(jnp.sum(-ps * delta, axis=0), axis=0, keepdims=True)
                dsk_ref[h:h + 1, :] += jnp.broadcast_to(dsink, (1, 128))
                dband_ref[h] += jnp.sum(ds, axis=0)
                dsb = ds.astype(BF16)
                dq = jnp.einsum('nqs,nsd->nqd', dsb, kband, preferred_element_type=F32) * scale
                dqkv_ref[:, hs] = dq.reshape(S, HEAD_DIM).astype(BF16)
                dkb = dkb + jnp.einsum('nqs,nqd->nsd', dsb, qh, preferred_element_type=F32) * scale
                dvb = dvb + jnp.einsum('nqs,nqd->nsd', p.astype(BF16), doh, preferred_element_type=F32)
            dqkv_ref[:, D_WIDTH + HEAD_DIM * g:D_WIDTH + HEAD_DIM * (g + 1)] = unband(dkb)
            dqkv_ref[:, D_WIDTH + D_KV_WIDTH + HEAD_DIM * g:D_WIDTH + D_KV_WIDTH + HEAD_DIM * (g + 1)] = unband(dvb)

    kcol = off['d'] + D_WIDTH
    wq = D_WIDTH + 2 * D_KV_WIDTH
    return pl.pallas_call(
        body, name=name, grid=(Bn,),
        in_specs=[_seq_spec(S, D_WIDTH, off['d']), _seq_spec(S, D_KV_WIDTH, kcol), _seq_spec(S, D_KV_WIDTH, kcol + D_KV_WIDTH),
                  _full((D_Q_HEADS, Q_BLOCK, 2 * Q_BLOCK)), _full((D_Q_HEADS, 128)), _seq_spec(S, D_WIDTH, 0)],
        out_specs=[_seq_spec(S, wq, 0), _full((D_Q_HEADS, Q_BLOCK, 2 * Q_BLOCK)), _full((D_Q_HEADS, 128))],
        out_shape=[jax.ShapeDtypeStruct((Bn, S, wq), BF16), jax.ShapeDtypeStruct((D_Q_HEADS, Q_BLOCK, 2 * Q_BLOCK), F32),
                   jax.ShapeDtypeStruct((D_Q_HEADS, 128), F32)],
        compiler_params=_params("arbitrary"),
    )(proj3, proj3, proj3, band_bias, sinks, dyd3)


def _assemble(pieces, *, name):
    T = pieces[0].shape[0]
    widths = [p.shape[1] for p in pieces]
    tm = _tile(T, 512, 256, 128)

    def body(*refs):
        out_ref = refs[-1]
        col = 0
        for r, w in zip(refs[:-1], widths):
            out_ref[:, col:col + w] = r[...]
            col += w

    return pl.pallas_call(
        body, name=name, grid=(T // tm,),
        in_specs=[pl.BlockSpec((tm, w), lambda i: (i, 0)) for w in widths],
        out_specs=pl.BlockSpec((tm, sum(widths)), lambda i: (i, 0)),
        out_shape=jax.ShapeDtypeStruct((T, sum(widths)), BF16),
        compiler_params=_params("parallel"),
    )(*pieces)


def _relbias_grad(dband, onehot, *, name):
    L, H, n = dband.shape
    R = onehot.shape[0]

    def body(d_ref, oh_ref, out_ref):
        tot = d_ref[0]
        for l in range(1, L):
            tot = tot + d_ref[l]
        out_ref[...] = lax.dot_general(oh_ref[...], tot, (((1,), (1,)), ((), ())), preferred_element_type=F32,
                                       precision=lax.Precision.HIGHEST)

    return pl.pallas_call(
        body, name=name, out_shape=jax.ShapeDtypeStruct((R, H), F32),
        compiler_params=pltpu.CompilerParams(vmem_limit_bytes=VMEM_LIMIT_V7X),
    )(dband, onehot)


def _all_reduce(buf, *, name):
    R, C = buf.shape

    def body(x_ref, o_ref, land, send, recv):
        me = _my_id()
        land[pl.ds(me, 1)] = x_ref[...][None]
        sends = []
        for k in range(1, N_DEV):
            to, _ = _peer(k)
            cp = pltpu.make_async_remote_copy(src_ref=x_ref, dst_ref=land.at[me], send_sem=send.at[k - 1],
                                              recv_sem=recv.at[k - 1], device_id=to, device_id_type=MESH)
            cp.start()
            sends.append(cp)
        for k in range(1, N_DEV):
            frm, frm_id = _peer(k)
            pltpu.make_async_remote_copy(src_ref=x_ref, dst_ref=land.at[frm_id], send_sem=send.at[k - 1],
                                         recv_sem=recv.at[k - 1], device_id=frm, device_id_type=MESH).wait_recv()
        for cp in sends:
            cp.wait_send()
        acc = land[0]
        for d in range(1, N_DEV):
            acc = acc + land[d]
        o_ref[...] = acc

    vmem = pl.BlockSpec(memory_space=pltpu.VMEM)
    return pl.pallas_call(
        body, name=name, in_specs=[vmem], out_specs=vmem, out_shape=jax.ShapeDtypeStruct((R, C), F32),
        scratch_shapes=[pltpu.VMEM((N_DEV, R, C), F32), pltpu.SemaphoreType.DMA((N_DEV - 1,)),
                        pltpu.SemaphoreType.DMA((N_DEV - 1,))],
        compiler_params=pltpu.CompilerParams(vmem_limit_bytes=VMEM_LIMIT_V7X),
    )(buf)


def _row_tile(rows, row_bytes, align):
    fits = [t for t in range(align, rows + 1, align) if rows % t == 0]
    small = [t for t in fits if t * row_bytes <= 2**20]
    return max(small) if small else (min(fits) if fits else rows)


def _sum8(recvs, *, name):
    L = len(recvs)
    _, rows, C = recvs[0].shape
    tr = _row_tile(rows, C * 4 * L, 16)

    def body(*refs):
        o_ref = refs[-1]
        for l, r_ref in enumerate(refs[:-1]):
            acc = r_ref[0].astype(F32)
            for d in range(1, N_DEV):
                acc = acc + r_ref[d].astype(F32)
            o_ref[l] = acc

    return pl.pallas_call(
        body, name=name, grid=(rows // tr,),
        in_specs=[pl.BlockSpec((N_DEV, tr, C), lambda i: (0, i, 0))] * L,
        out_specs=pl.BlockSpec((L, tr, C), lambda i: (0, i, 0)),
        out_shape=jax.ShapeDtypeStruct((L, rows, C), F32), compiler_params=_params("parallel"),
    )(*recvs)


def _band_bias(rel_bias, onehot, *, name):
    R, H = rel_bias.shape

    def body(rb_ref, oh_ref, out_ref):
        out_ref[...] = lax.dot_general(rb_ref[...], oh_ref[...], (((0,), (0,)), ((), ())), preferred_element_type=F32,
                                       precision=lax.Precision.HIGHEST)

    return pl.pallas_call(
        body, name=name, out_shape=jax.ShapeDtypeStruct((H, onehot.shape[1]), F32),
        compiler_params=pltpu.CompilerParams(vmem_limit_bytes=VMEM_LIMIT_V7X),
    )(rel_bias, onehot)


def _adamw(g, w, m, v, *, name):
    rows, C = g.shape
    tr = _row_tile(rows, C * 4, 8)

    def body(g_ref, w_ref, m_ref, v_ref, d_ref, mo_ref, vo_ref):
        gt = g_ref[...]
        mn = ADAM_B1 * m_ref[...] + (1.0 - ADAM_B1) * gt
        vn = ADAM_B2 * v_ref[...] + (1.0 - ADAM_B2) * jnp.square(gt)
        m_hat = mn / (1.0 - ADAM_B1 ** ADAM_STEP)
        v_hat = vn / (1.0 - ADAM_B2 ** ADAM_STEP)
        d_ref[...] = -ADAM_LR * (m_hat / (jnp.sqrt(v_hat) + ADAM_EPS) + ADAM_WD * w_ref[...])
        mo_ref[...] = mn
        vo_ref[...] = vn

    spec = pl.BlockSpec((tr, C), lambda i: (i, 0))
    return pl.pallas_call(
        body, name=name, grid=(rows // tr,), in_specs=[spec] * 4, out_specs=[spec] * 3,
        out_shape=[jax.ShapeDtypeStruct((rows, C), F32)] * 3, compiler_params=_params("parallel"),
    )(g, w, m, v)


def _in_splits():
    a_f = 3 * A_WIDTH
    b = a_f + A_HEADS
    c = b + 3 * B_WIDTH
    d = c + 2 * C_WIDTH
    gates = d + D_WIDTH + 2 * D_KV_WIDTH
    return a_f, b, c, d, gates


def _permute_in(w):
    a_f, b, c, d, gates = _in_splits()
    pad = jnp.zeros(w.shape[:-1] + (AF_PAD - A_HEADS,), w.dtype)
    return jnp.concatenate([w[..., gates:], w[..., :a_f], w[..., b:c], w[..., c:d], w[..., d:gates], w[..., a_f:b], pad], axis=-1)


def _unpermute_in(g, D):
    off = _layout(D)
    return jnp.concatenate([g[..., off['a']:off['b']], g[..., off['af']:off['af'] + A_HEADS], g[..., off['b']:off['c']],
                            g[..., off['c']:off['d']], g[..., off['d']:off['af']], g[..., :off['a']]], axis=-1)


def _bucket_onehot():
    dist = np.maximum(np.arange(Q_BLOCK)[:, None] + Q_BLOCK - np.arange(2 * Q_BLOCK)[None, :], 0)
    max_exact = REL_BUCKETS // 2
    large = max_exact + (np.log(np.maximum(dist, 1).astype(np.float32) / np.float32(max_exact))
                         / np.float32(math.log(REL_MAX_DIST / max_exact)) * np.float32(REL_BUCKETS - max_exact)).astype(np.int32)
    bucket = np.where(dist < max_exact, dist, np.minimum(large, REL_BUCKETS - 1))
    return (bucket.reshape(1, -1) == np.arange(REL_BUCKETS)[:, None]).astype(np.float32)


GATHER_SEQ = [('ffn1_up', ['ffn1_w_gu']), ('ffn1_down', ['ffn1_w_down']), ('proj', ['w_in']),
              ('fox', ['w_br_a', 'w_br_b', 'w_br_c', 'w_br_d', 'w_o', 'w_ple_gate', 'w_ple']),
              ('ffn2_up', ['ffn2_w_gu']), ('ffn2_down', ['ffn2_w_down'])]
GATHER_AHEAD = 2
SCATTER_HOSTS = {'ffn2_down_bwd': ['w_ple', 'w_ple_gate'], 'ffn2_up_bwd': ['ffn2_w_gu'],
                 'fox_bwd': ['ffn2_w_down', 'w_o', 'w_br_a', 'w_br_b', 'w_br_c', 'w_br_d'], 'proj_bwd': ['w_in'],
                 'ffn1_down_bwd': [], 'ffn1_up_bwd': ['ffn1_w_gu']}


def _pack(parts):
    flat = jnp.concatenate([q.reshape(-1).astype(F32) for q in parts])
    return jnp.pad(flat, (0, (-flat.shape[0]) % 1024)).reshape(-1, 128)


def _unpack(buf, shapes):
    flat, out, pos = buf.reshape(-1), [], 0
    for s in shapes:
        n = math.prod(s)
        out.append(flat[pos:pos + n].reshape(s))
        pos += n
    return out


def kernel(x, p, ffn1_norm_pre, ffn1_w_gu, ffn1_w_down, ffn1_norm_post, mix_norm_pre, w_in, b_forget, b_gate, conv_short, conv_dw, conv_dw_bias, conv_ln_gain, conv_ln_bias, attn_sinks, rel_bias, w_br_a, w_br_b, w_br_c, w_br_d, w_o, mix_norm_post, ffn2_norm_pre, ffn2_w_gu, ffn2_w_down, ffn2_norm_post, ple_norm_gate, w_ple_gate, w_ple, ple_norm_post, loss_target, m_ffn1_norm_pre, m_ffn1_w_gu, m_ffn1_w_down, m_ffn1_norm_post, m_mix_norm_pre, m_w_in, m_b_forget, m_b_gate, m_conv_short, m_conv_dw, m_conv_dw_bias, m_conv_ln_gain, m_conv_ln_bias, m_attn_sinks, m_rel_bias, m_w_br_a, m_w_br_b, m_w_br_c, m_w_br_d, m_w_o, m_mix_norm_post, m_ffn2_norm_pre, m_ffn2_w_gu, m_ffn2_w_down, m_ffn2_norm_post, m_ple_norm_gate, m_w_ple_gate, m_w_ple, m_ple_norm_post, v_ffn1_norm_pre, v_ffn1_w_gu, v_ffn1_w_down, v_ffn1_norm_post, v_mix_norm_pre, v_w_in, v_b_forget, v_b_gate, v_conv_short, v_conv_dw, v_conv_dw_bias, v_conv_ln_gain, v_conv_ln_bias, v_attn_sinks, v_rel_bias, v_w_br_a, v_w_br_b, v_w_br_c, v_w_br_d, v_w_o, v_mix_norm_post, v_ffn2_norm_pre, v_ffn2_w_gu, v_ffn2_w_down, v_ffn2_norm_post, v_ple_norm_gate, v_w_ple_gate, v_w_ple, v_ple_norm_post):
    a = dict(locals())
    Bn, S, D = x.shape
    T = Bn * S
    L, E = p.shape[0], p.shape[-1]
    F = ffn1_w_down.shape[1] * N_DEV
    off = _layout(D)
    PW = off['end']
    me = _my_id()

    shard = {n: a[n].astype(BF16) for n in BIG}
    shard['ffn1_w_gu'] = jnp.swapaxes(ffn1_w_gu, 1, 2).astype(BF16)
    shard['ffn2_w_gu'] = jnp.swapaxes(ffn2_w_gu, 1, 2).astype(BF16)
    shard['w_in'] = _permute_in(w_in).astype(BF16)
    is_col = lambda names: [n in COL_SHARDED for n in names]
    head = [n for _, names in GATHER_SEQ[:GATHER_AHEAD] for n in names]
    first = _Exchange("gather", [shard[n][0] for n in head], is_col(head))
    full = [dict(zip(head, first.run_alone("gather_head")))] + [{} for _ in range(1, L)]
    hosts = [h_ for h_, _ in GATHER_SEQ]

    def with_gather(fn, host, i, *args, **kw):
        li, lj = divmod(i * len(hosts) + hosts.index(host) + GATHER_AHEAD, len(hosts))
        names = GATHER_SEQ[lj][1]
        rider = _Exchange("gather", [shard[n][li] for n in names], is_col(names)) if li < L else None
        out = fn(*args, rider=rider, **kw)
        if rider is not None:
            full[li].update(zip(names, rider.result))
        return out

    cw = conv_short.shape[2]
    conv_full = [lax.dynamic_update_slice(jnp.zeros(c.shape[:2] + (cw * N_DEV,), F32), c, (0, 0, me * cw))
                 for c in (conv_short, conv_dw)]
    conv_shapes = [c.shape for c in conv_full]
    cs_all, cdw_all = _unpack(_all_reduce(_pack(conv_full), name="gather_conv"), conv_shapes)

    onehot = jnp.asarray(_bucket_onehot())
    band_bias = _band_bias(rel_bias, onehot, name="band_bias").reshape(D_Q_HEADS, Q_BLOCK, 2 * Q_BLOCK)

    def vec(name, i):
        return a[name][i][None]

    def lay(i):
        return dict(
            bfor=jnp.pad(b_forget[i], (0, AF_LANES - A_HEADS))[None], bgate=b_gate[i].reshape(N_BRANCH, D),
            cs=cs_all[i], cdw=cdw_all[i], cb=conv_dw_bias[i][None], lg=conv_ln_gain[i][None], lb=conv_ln_bias[i][None],
            sinks=jnp.broadcast_to(attn_sinks[i][:, None], (D_Q_HEADS, 128)),
            wbrs=[full[i][n] for n in ('w_br_a', 'w_br_b', 'w_br_c', 'w_br_d')], pe=p[i].reshape(T, E))

    def wgu(k, i):
        return full[i][f'ffn{k}_w_gu'].reshape(2, F, D)

    h = x.reshape(T, D)
    saved = []
    for i in range(L):
        s = dict(h0=h)
        s['gu1'], s['n1'] = with_gather(_rms_mm, 'ffn1_up', i, h, vec('ffn1_norm_pre', i), wgu(1, i), nt=True,
                                        out_dtype=BF16, save_n=True, name=f"ffn1_up_{i}")
        s['h1'], s['f1'] = with_gather(_ffn_down, 'ffn1_down', i, s['gu1'], full[i]['ffn1_w_down'], h,
                                       vec('ffn1_norm_post', i), name=f"ffn1_down_{i}")
        win = full[i]['w_in']
        proj, s['u'] = with_gather(_rms_mm, 'proj', i, s['h1'], vec('mix_norm_pre', i), win[None], nt=False,
                                   out_dtype=BF16, save_n=True, name=f"proj_{i}")
        af = _rms_mm(s['h1'], vec('mix_norm_pre', i), win[None, :, off['af']:off['af'] + AF_LANES], nt=False,
                     out_dtype=F32, save_n=False, name=f"proj_forget_{i}")
        s['proj'] = proj.reshape(T, PW)
        s['proj3'], s['af3'] = proj.reshape(Bn, S, PW), af.reshape(Bn, S, AF_LANES)
        q = lay(i)
        ya = with_gather(_fox_fwd, 'fox', i, s['proj3'], s['af3'], q['bfor'], off, name=f"fox_{i}")
        yb = _sconv_fwd(s['proj3'], q['cs'], off, name=f"sconv_{i}")
        yc = _cconv_fwd(s['proj3'], q['cdw'], q['cb'], q['lg'], q['lb'], off, name=f"cconv_{i}")
        yd = _swa_fwd(s['proj3'], band_bias, q['sinks'], off, name=f"swa_{i}")
        s['ys'] = [y.reshape(T, y.shape[-1]) for y in (ya, yb, yc, yd)]
        s['h2'], s['o'], s['merged'] = _merge_out(s['ys'], s['proj'], q['bgate'], q['wbrs'], full[i]['w_o'], s['h1'],
                                                  vec('mix_norm_post', i), name=f"merge_{i}")
        s['gu2'], s['n2'] = with_gather(_rms_mm, 'ffn2_up', i, s['h2'], vec('ffn2_norm_pre', i), wgu(2, i), nt=True,
                                        out_dtype=BF16, save_n=True, name=f"ffn2_up_{i}")
        s['h3'], s['f2'] = with_gather(_ffn_down, 'ffn2_down', i, s['gu2'], full[i]['ffn2_w_down'], s['h2'],
                                       vec('ffn2_norm_post', i), name=f"ffn2_down_{i}")
        h = _ple(s['h3'], q['pe'], vec('ple_norm_gate', i), full[i]['w_ple_gate'], full[i]['w_ple'],
                 vec('ple_norm_post', i), name=f"ple_{i}")
        saved.append(s)

    lpart, dh = _loss_head(h, loss_target.reshape(T, D), name="loss_head")
    loss = lax.psum(lpart[0, 0], MESH_AXES)

    gbuf = [{} for _ in range(L)]
    recv = [{} for _ in range(L)]
    sg = {n: [None] * L for n in WEIGHTS if n not in BIG and n != 'rel_bias'}
    dbands = [None] * L

    def wgrad(n, i, a_, b_, **kw):
        gbuf[i][n] = _mm_tn(a_, b_, name=f"d_{n}_{i}", **kw)

    def with_scatter(fn, host, i, *args, **kw):
        items = [(i, n) for n in SCATTER_HOSTS[host]]
        if host == 'ffn2_down_bwd' and i + 1 < L:
            items.append((i + 1, 'ffn1_w_down'))
        if host == 'ffn1_up_bwd' and i == 0:
            items.append((0, 'ffn1_w_down'))
        rider = _Exchange("scatter", [gbuf[l][n] for l, n in items], is_col([n for _, n in items])) if items else None
        out = fn(*args, rider=rider, **kw)
        if rider is not None:
            for (l, n), r in zip(items, rider.result):
                recv[l][n] = r
        return out

    def ffn_bwd(k, i, dh_out, s, h_in):
        dgu, df, sg[f'ffn{k}_norm_post'][i] = with_scatter(
            _ffn_down_bwd, f'ffn{k}_down_bwd', i, dh_out, s[f'f{k}'], vec(f'ffn{k}_norm_post', i), full[i][f'ffn{k}_w_down'],
            s[f'gu{k}'], name=f"ffn{k}_down_bwd_{i}")
        wgrad(f'ffn{k}_w_down', i, s[f'gu{k}'], df, swiglu=True)
        wgrad(f'ffn{k}_w_gu', i, dgu, s[f'n{k}'])
        dh_in, sg[f'ffn{k}_norm_pre'][i] = with_scatter(
            _mm_rmsbwd, f'ffn{k}_up_bwd', i, dgu, wgu(k, i), dh_out, h_in, vec(f'ffn{k}_norm_pre', i), nt=False,
            name=f"ffn{k}_up_bwd_{i}")
        return dh_in

    for i in reversed(range(L)):
        s, q = saved[i], lay(i)
        dh, de, dpgl, npg, sg['ple_norm_gate'][i], sg['ple_norm_post'][i] = _ple_bwd(
            dh, s['h3'], q['pe'], vec('ple_norm_gate', i), full[i]['w_ple_gate'], full[i]['w_ple'], vec('ple_norm_post', i),
            name=f"ple_bwd_{i}")
        wgrad('w_ple', i, q['pe'][None], de)
        wgrad('w_ple_gate', i, npg[None], dpgl)
        dh = ffn_bwd(2, i, dh, s, s['h2'])
        do, dz, *dys, dgates, sg['mix_norm_post'][i], dbg = _merge_out_bwd(
            dh, s['o'], vec('mix_norm_post', i), full[i]['w_o'], s['ys'], s['proj'], q['bgate'], q['wbrs'],
            name=f"merge_bwd_{i}")
        sg['b_gate'][i] = dbg.reshape(-1)
        wgrad('w_o', i, s['merged'][None], do)
        for b, n in enumerate(('w_br_a', 'w_br_b', 'w_br_c', 'w_br_d')):
            wgrad(n, i, s['ys'][b][None], dz, b_plane=b)
        dy3 = [d.reshape(Bn, S, d.shape[-1]) for d in dys]
        da, daf, dbf = with_scatter(_fox_bwd, 'fox_bwd', i, s['proj3'], s['af3'], q['bfor'], dy3[0], off,
                                    name=f"fox_bwd_{i}")
        sg['b_forget'][i] = dbf[0, :A_HEADS]
        db, sg['conv_short'][i] = _sconv_bwd(s['proj3'], q['cs'], dy3[1], off, name=f"sconv_bwd_{i}")
        dc, sg['conv_dw'][i], sg['conv_dw_bias'][i], sg['conv_ln_gain'][i], sg['conv_ln_bias'][i] = _cconv_bwd(
            s['proj3'], q['cdw'], q['cb'], q['lg'], q['lb'], dy3[2], off, name=f"cconv_bwd_{i}")
        dd, dbands[i], dsk = _swa_bwd(s['proj3'], band_bias, q['sinks'], dy3[3], off, name=f"swa_bwd_{i}")
        sg['attn_sinks'][i] = dsk[:, 0]
        dproj = _assemble([dgates] + [t.reshape(T, t.shape[-1]) for t in (da, db, dc, dd, daf)], name=f"dproj_{i}")
        wgrad('w_in', i, s['u'][None], dproj)
        dh, sg['mix_norm_pre'][i] = with_scatter(
            _mm_rmsbwd, 'proj_bwd', i, dproj[None], full[i]['w_in'][None], dh, s['h1'], vec('mix_norm_pre', i), nt=True,
            name=f"proj_bwd_{i}")
        dh = ffn_bwd(1, i, dh, s, s['h0'])
    grad_x = dh.reshape(Bn, S, D)

    d_rel = _relbias_grad(jnp.stack(dbands).reshape(L, D_Q_HEADS, -1), onehot, name="relbias_grad")
    small_g = [d_rel if n == 'rel_bias' else jnp.stack(sg[n]).reshape(a[n].shape) for n in SMALL]
    n_small_rows = _pack(small_g).shape[0]
    conv_g = [jnp.stack(sg[n]) for n in CONV_SHARDED]
    red = _all_reduce(jnp.concatenate([_pack(small_g), _pack(conv_g)]), name="reduce_small")
    g_small_buf = red[:n_small_rows]
    conv_gfull = _unpack(red[n_small_rows:], conv_shapes)
    conv_gloc = [lax.dynamic_slice_in_dim(g, me * cw, cw, axis=2) for g in conv_gfull]

    grads, deltas, new_m, new_v = {}, {}, {}, {}
    small_shapes = [a[n].shape for n in SMALL]
    res = _adamw(g_small_buf, *[_pack([a[pre + n] for n in SMALL]) for pre in ('', 'm_', 'v_')], name="adamw_small")
    for dst, buf in zip((grads, deltas, new_m, new_v), (g_small_buf,) + tuple(res)):
        dst.update(zip(SMALL, _unpack(buf, small_shapes)))
    loc_shapes = [a[n].shape for n in CONV_SHARDED]
    g_conv_buf = _pack(conv_gloc)
    res = _adamw(g_conv_buf, *[_pack([a[pre + n] for n in CONV_SHARDED]) for pre in ('', 'm_', 'v_')], name="adamw_conv")
    for dst, buf in zip((grads, deltas, new_m, new_v), (g_conv_buf,) + tuple(res)):
        dst.update(zip(CONV_SHARDED, _unpack(buf, loc_shapes)))

    for n in BIG:
        g = _sum8([recv[l][n] for l in range(L)], name=f"sum_{n}")
        if n in ('ffn1_w_gu', 'ffn2_w_gu'):
            g = jnp.swapaxes(g, 1, 2)
        elif n == 'w_in':
            g = _unpermute_in(g, D)
        C = g.shape[-1]
        res = _adamw(g.reshape(-1, C), *[a[pre + n].reshape(-1, C) for pre in ('', 'm_', 'v_')], name=f"adamw_{n}")
        grads[n] = g
        deltas[n], new_m[n], new_v[n] = [t.reshape(g.shape) for t in res]

    return (loss, grad_x, *[grads[n] for n in WEIGHTS], *[deltas[n] for n in WEIGHTS],
            *[new_m[n] for n in WEIGHTS], *[new_v[n] for n in WEIGHTS])
```

```python
import functools
import math

import jax
import jax.numpy as jnp
import numpy as np
from jax import lax
from jax.experimental import pallas as pl
from jax.experimental.pallas import tpu as pltpu

F32 = jnp.float32
BF16 = jnp.bfloat16

EPS = 1e-6
NEG_INF = -1e30
HEAD_DIM = 64
SCALE = HEAD_DIM ** -0.5
A_HEADS = 4
A_WIDTH = A_HEADS * HEAD_DIM
B_WIDTH = 256
C_WIDTH = 256
SHORT_CONV = 3
CONF_CONV = 31
D_Q_HEADS = 8
D_KV_HEADS = 2
D_GROUP = D_Q_HEADS // D_KV_HEADS
D_WIDTH = D_Q_HEADS * HEAD_DIM
D_KV_WIDTH = D_KV_HEADS * HEAD_DIM
WINDOW = 128
Q_BLOCK = 128
N_BRANCH = 4
REL_BUCKETS = 32
REL_MAX_DIST = 128
AF_PAD = 256
AF_LANES = 128

ADAM_LR = 0.001
ADAM_B1 = 0.9
ADAM_B2 = 0.999
ADAM_EPS = 1e-08
ADAM_WD = 0.01
ADAM_STEP = 10

N_DEV = 8
MESH_AXES = ("x", "y", "c")
VMEM_LIMIT_V7X = 56 * 2**20
MESH = pl.DeviceIdType.MESH

WEIGHTS = ['ffn1_norm_pre', 'ffn1_w_gu', 'ffn1_w_down', 'ffn1_norm_post', 'mix_norm_pre', 'w_in', 'b_forget',
           'b_gate', 'conv_short', 'conv_dw', 'conv_dw_bias', 'conv_ln_gain', 'conv_ln_bias', 'attn_sinks',
           'rel_bias', 'w_br_a', 'w_br_b', 'w_br_c', 'w_br_d', 'w_o', 'mix_norm_post', 'ffn2_norm_pre',
           'ffn2_w_gu', 'ffn2_w_down', 'ffn2_norm_post', 'ple_norm_gate', 'w_ple_gate', 'w_ple', 'ple_norm_post']
ARG_NAMES = ['x', 'p'] + WEIGHTS + ['loss_target'] + ['m_' + n for n in WEIGHTS] + ['v_' + n for n in WEIGHTS]
BIG = ['ffn1_w_gu', 'ffn1_w_down', 'w_in', 'w_br_a', 'w_br_b', 'w_br_c', 'w_br_d', 'w_o', 'ffn2_w_gu',
       'ffn2_w_down', 'w_ple_gate', 'w_ple']
COL_SHARDED = ('w_br_a', 'w_br_b', 'w_br_c', 'w_br_d', 'w_ple')
CONV_SHARDED = ('conv_short', 'conv_dw')
SMALL = [n for n in WEIGHTS if n not in BIG and n not in CONV_SHARDED]


def _tile(n, *prefs):
    for t in prefs:
        if n % t == 0:
            return t
    return n


def _chunks(n, width=768):
    return [(c, min(c + width, n)) for c in range(0, n, width)]


def _params(*sem):
    return pltpu.CompilerParams(dimension_semantics=sem, vmem_limit_bytes=VMEM_LIMIT_V7X)


def _dot(a, b):
    return jnp.dot(a, b, preferred_element_type=F32)


def _dot_nt(a, b):
    return lax.dot_general(a, b, (((1,), (1,)), ((), ())), preferred_element_type=F32)


def _dot_tn(a, b):
    return lax.dot_general(a, b, (((0,), (0,)), ((), ())), preferred_element_type=F32)


def _rstd(x):
    return lax.rsqrt(jnp.mean(x * x, axis=-1, keepdims=True) + EPS)


def _rms_bwd(dy, x, r, g):
    xh = x * r
    dxh = dy * g
    dx = r * (dxh - xh * jnp.mean(dxh * xh, axis=-1, keepdims=True))
    return dx, dy * xh


def _colsum(v):
    return jnp.sum(v, axis=0, keepdims=True)


def _my_id():
    return lax.axis_index("x") * 4 + lax.axis_index("y") * 2 + lax.axis_index("c")


def _peer(k):
    coords = []
    for bit, axis in zip((4, 2, 1), MESH_AXES):
        me = lax.axis_index(axis)
        coords.append(1 - me if k & bit else me)
    return tuple(coords), coords[0] * 4 + coords[1] * 2 + coords[2]


def _window(ref, col, d, size):
    start = pl.multiple_of(d * size, 8)
    return ref.at[:, pl.ds(start, size)] if col else ref.at[pl.ds(start, size), :]


ANY = pl.BlockSpec(memory_space=pl.ANY)


class _Exchange:
    def __init__(self, kind, arrays, cols):
        self.kind, self.arrays, self.cols = kind, list(arrays), list(cols)
        n = len(self.arrays)
        if kind == "gather":
            self.sizes = [a.shape[1] if c else a.shape[0] for a, c in zip(self.arrays, cols)]
            self.out_shape = [jax.ShapeDtypeStruct((a.shape[0], a.shape[1] * N_DEV) if c else (a.shape[0] * N_DEV, a.shape[1]),
                                                   a.dtype) for a, c in zip(self.arrays, cols)]
        else:
            self.sizes = [a.shape[1] // N_DEV if c else a.shape[0] // N_DEV for a, c in zip(self.arrays, cols)]
            self.out_shape = [jax.ShapeDtypeStruct((N_DEV, a.shape[0], s) if c else (N_DEV, s, a.shape[1]), a.dtype)
                              for a, c, s in zip(self.arrays, cols, self.sizes)]
        self.scratch = [pltpu.SemaphoreType.DMA((n, N_DEV - 1)), pltpu.SemaphoreType.DMA((n, N_DEV - 1)),
                        pltpu.SemaphoreType.DMA((n,))]
        self.result = None

    def _src(self, ins, w, d):
        return ins[w] if self.kind == "gather" else _window(ins[w], self.cols[w], d, self.sizes[w])

    def _dst(self, outs, w, d):
        return _window(outs[w], self.cols[w], d, self.sizes[w]) if self.kind == "gather" else outs[w].at[d]

    def _copies(self, ins, outs, send, recv, loc):
        me = _my_id()
        n = len(self.arrays)
        two_level = self.kind == "gather"
        local = [pltpu.make_async_copy(self._src(ins, w, me), self._dst(outs, w, me), loc.at[w]) for w in range(n)]
        sends, arrivals, forwards, handed = [], [], [], []
        sibling, _ = _peer(1)
        for k in range(1, N_DEV):
            peer, peer_id = _peer(k)
            for w in range(n):
                sems = dict(send_sem=send.at[w, k - 1], recv_sem=recv.at[w, k - 1])
                if two_level and k > 1 and k % 2 == 1:
                    _, via_id = _peer(k - 1)
                    forwards.append(pltpu.make_async_remote_copy(
                        src_ref=self._dst(outs, w, via_id), dst_ref=self._dst(outs, w, via_id), device_id=sibling,
                        device_id_type=MESH, **sems))
                    handed.append(pltpu.make_async_remote_copy(
                        src_ref=self._dst(outs, w, peer_id), dst_ref=self._dst(outs, w, peer_id), device_id=sibling,
                        device_id_type=MESH, **sems))
                    continue
                at = dict(device_id=peer, device_id_type=MESH, **sems)
                sends.append(pltpu.make_async_remote_copy(src_ref=self._src(ins, w, peer_id), dst_ref=self._dst(outs, w, me), **at))
                arrivals.append(pltpu.make_async_remote_copy(src_ref=self._src(ins, w, me), dst_ref=self._dst(outs, w, peer_id), **at))
        return local, sends, arrivals, forwards, handed

    def start(self, ins, outs, send, recv, loc):
        local, sends, _, _, _ = self._copies(ins, outs, send, recv, loc)
        for cp in local + sends:
            cp.start()

    def wait(self, ins, outs, send, recv, loc):
        local, sends, arrivals, forwards, handed = self._copies(ins, outs, send, recv, loc)
        for cp in arrivals:
            cp.wait_recv()
        for cp in forwards:
            cp.start()
        for cp in handed:
            cp.wait_recv()
        for cp in sends + forwards:
            cp.wait_send()
        for cp in local:
            cp.wait()

    def run_alone(self, name):
        n = len(self.arrays)

        def body(*refs):
            self.start(refs[:n], refs[n:2 * n], *refs[2 * n:])
            self.wait(refs[:n], refs[n:2 * n], *refs[2 * n:])

        self.result = pl.pallas_call(body, name=name, in_specs=[ANY] * n, out_specs=[ANY] * n, out_shape=self.out_shape,
                                     scratch_shapes=self.scratch)(*self.arrays)
        return self.result


def _call(body, args, *, name, grid, in_specs, out_specs, out_shape, scratch_shapes=(), sem, rider=None):
    if rider is None:
        return pl.pallas_call(body, name=name, grid=grid, in_specs=in_specs, out_specs=out_specs, out_shape=out_shape,
                              scratch_shapes=list(scratch_shapes), compiler_params=_params(*sem))(*args)
    n_in, n_out, n_scr, n_r = len(in_specs), len(out_shape), len(scratch_shapes), len(rider.arrays)

    def both(*refs):
        ins, r_in = refs[:n_in], refs[n_in:n_in + n_r]
        outs, r_out = refs[n_in + n_r:n_in + n_r + n_out], refs[n_in + n_r + n_out:n_in + 2 * n_r + n_out]
        scr, sems = refs[n_in + 2 * n_r + n_out:n_in + 2 * n_r + n_out + n_scr], refs[n_in + 2 * n_r + n_out + n_scr:]
        first = functools.reduce(lambda p, q: p & q, [pl.program_id(d) == 0 for d in range(len(grid))])
        last = functools.reduce(lambda p, q: p & q, [pl.program_id(d) == g - 1 for d, g in enumerate(grid)])

        @pl.when(first)
        def _():
            rider.start(r_in, r_out, *sems)

        body(*ins, *outs, *scr)

        @pl.when(last)
        def _():
            rider.wait(r_in, r_out, *sems)

    res = pl.pallas_call(
        both, name=name, grid=grid, in_specs=list(in_specs) + [ANY] * n_r, out_specs=list(out_specs) + [ANY] * n_r,
        out_shape=list(out_shape) + rider.out_shape, scratch_shapes=list(scratch_shapes) + rider.scratch,
        compiler_params=_params(*(("arbitrary",) * len(grid))))(*args, *rider.arrays)
    rider.result = res[n_out:]
    return res[:n_out]


def _rms_mm(h, g, w, *, nt, out_dtype, save_n, name, rider=None):
    T, D = h.shape
    P = w.shape[0]
    N = w.shape[1] if nt else w.shape[2]
    tm = _tile(T, 1024, 512, 256, 128)
    tn = _tile(N, 2816, 1792, 2048, 1408, 1024, 512, 256, 128)

    def body(h_ref, g_ref, w_ref, y_ref, *rest):
        n_scr = rest[-1]

        @pl.when((pl.program_id(1) == 0) & (pl.program_id(2) == 0))
        def _():
            x = h_ref[...]
            n = (x * _rstd(x) * g_ref[...]).astype(BF16)
            n_scr[...] = n
            if save_n:
                rest[0][...] = n

        wt = w_ref[...]
        y = _dot_nt(n_scr[...], wt) if nt else _dot(n_scr[...], wt)
        y_ref[...] = y.astype(out_dtype)

    w_spec = (pl.BlockSpec((None, tn, D), lambda i, p, j: (p, j, 0)) if nt
              else pl.BlockSpec((None, D, tn), lambda i, p, j: (p, 0, j)))
    out_shape = [jax.ShapeDtypeStruct((P, T, N), out_dtype)]
    out_specs = [pl.BlockSpec((None, tm, tn), lambda i, p, j: (p, i, j))]
    if save_n:
        out_shape.append(jax.ShapeDtypeStruct((T, D), BF16))
        out_specs.append(pl.BlockSpec((tm, D), lambda i, p, j: (i, 0)))
    res = _call(
        body, (h, g, w), name=name, grid=(T // tm, P, N // tn),
        in_specs=[pl.BlockSpec((tm, D), lambda i, p, j: (i, 0)), pl.BlockSpec((1, D), lambda i, p, j: (0, 0)), w_spec],
        out_specs=out_specs, out_shape=out_shape, scratch_shapes=[pltpu.VMEM((tm, D), BF16)],
        sem=("parallel", "arbitrary", "arbitrary"), rider=rider)
    return res if save_n else res[0]


def _ffn_down(gu, wd, h, gpost, *, name, rider=None):
    _, T, F = gu.shape
    D = wd.shape[1]
    tm = _tile(T, 512, 256, 128)
    tk = _tile(F, 2816, 1408, 1024, 512, 256, 128)
    nk = F // tk

    def body(g_ref, u_ref, wd_ref, h_ref, gp_ref, hn_ref, f_ref, acc):
        k = pl.program_id(1)

        @pl.when(k == 0)
        def _():
            acc[...] = jnp.zeros_like(acc)

        part = None
        for c0, c1 in _chunks(tk):
            gt = g_ref[:, c0:c1].astype(F32)
            a = (gt * jax.nn.sigmoid(gt) * u_ref[:, c0:c1].astype(F32)).astype(BF16)
            d = _dot(a, wd_ref[c0:c1, :])
            part = d if part is None else part + d
        acc[...] += part

        @pl.when(k == nk - 1)
        def _():
            f = acc[...]
            f_ref[...] = f
            hn_ref[...] = h_ref[...] + 0.5 * (f * _rstd(f) * gp_ref[...])

    return _call(
        body, (gu, gu, wd, h, gpost), name=name, grid=(T // tm, nk),
        in_specs=[pl.BlockSpec((None, tm, tk), lambda i, k: (0, i, k)), pl.BlockSpec((None, tm, tk), lambda i, k: (1, i, k)),
                  pl.BlockSpec((tk, D), lambda i, k: (k, 0)), pl.BlockSpec((tm, D), lambda i, k: (i, 0)),
                  pl.BlockSpec((1, D), lambda i, k: (0, 0))],
        out_specs=[pl.BlockSpec((tm, D), lambda i, k: (i, 0)), pl.BlockSpec((tm, D), lambda i, k: (i, 0))],
        out_shape=[jax.ShapeDtypeStruct((T, D), F32), jax.ShapeDtypeStruct((T, D), F32)],
        scratch_shapes=[pltpu.VMEM((tm, D), F32)], sem=("parallel", "arbitrary"), rider=rider)


def _ffn_down_bwd(dh, f, gpost, wd, gu, *, name, rider=None):
    _, T, F = gu.shape
    D = wd.shape[1]
    tm = _tile(T, 256, 128)
    tn = _tile(F, 2816, 1408, 1024, 512, 256, 128)

    def body(dh_ref, f_ref, gp_ref, wd_ref, g_ref, u_ref, dgu_ref, df_ref, dgp_ref, df_scr):
        i, j = pl.program_id(0), pl.program_id(1)

        @pl.when((i == 0) & (j == 0))
        def _():
            dgp_ref[...] = jnp.zeros_like(dgp_ref)

        @pl.when(j == 0)
        def _():
            x = f_ref[...]
            dx, dgn = _rms_bwd(0.5 * dh_ref[...], x, _rstd(x), gp_ref[...])
            dgp_ref[...] += _colsum(dgn)
            df = dx.astype(BF16)
            df_scr[...] = df
            df_ref[...] = df

        for c0, c1 in _chunks(tn):
            dact = _dot_nt(df_scr[...], wd_ref[c0:c1, :])
            gt = g_ref[:, c0:c1].astype(F32)
            ut = u_ref[:, c0:c1].astype(F32)
            sg = jax.nn.sigmoid(gt)
            dgu_ref[0, :, c0:c1] = (dact * ut * (sg * (1.0 + gt * (1.0 - sg)))).astype(BF16)
            dgu_ref[1, :, c0:c1] = (dact * (gt * sg)).astype(BF16)

    return _call(
        body, (dh, f, gpost, wd, gu, gu), name=name, grid=(T // tm, F // tn),
        in_specs=[pl.BlockSpec((tm, D), lambda i, j: (i, 0)), pl.BlockSpec((tm, D), lambda i, j: (i, 0)),
                  pl.BlockSpec((1, D), lambda i, j: (0, 0)), pl.BlockSpec((tn, D), lambda i, j: (j, 0)),
                  pl.BlockSpec((None, tm, tn), lambda i, j: (0, i, j)), pl.BlockSpec((None, tm, tn), lambda i, j: (1, i, j))],
        out_specs=[pl.BlockSpec((2, tm, tn), lambda i, j: (0, i, j)), pl.BlockSpec((tm, D), lambda i, j: (i, 0)),
                   pl.BlockSpec((1, D), lambda i, j: (0, 0))],
        out_shape=[jax.ShapeDtypeStruct((2, T, F), BF16), jax.ShapeDtypeStruct((T, D), BF16),
                   jax.ShapeDtypeStruct((1, D), F32)],
        scratch_shapes=[pltpu.VMEM((tm, D), BF16)], sem=("arbitrary", "arbitrary"), rider=rider)


def _mm_rmsbwd(a, b, dh_in, h, g, *, nt, name, rider=None):
    P, T, K = a.shape
    D = h.shape[1]
    tm = _tile(T, 1024, 512, 256, 128)
    tk = _tile(K, 1792, 1408, 2048, 1024, 512, 256, 128)
    nk = K // tk

    def body(a_ref, b_ref, dh_ref, h_ref, g_ref, out_ref, dg_ref, acc):
        i, p, k = pl.program_id(0), pl.program_id(1), pl.program_id(2)

        @pl.when((i == 0) & (p == 0) & (k == 0))
        def _():
            dg_ref[...] = jnp.zeros_like(dg_ref)

        @pl.when((p == 0) & (k == 0))
        def _():
            acc[...] = jnp.zeros_like(acc)

        acc[...] += _dot_nt(a_ref[...], b_ref[...]) if nt else _dot(a_ref[...], b_ref[...])

        @pl.when((p == P - 1) & (k == nk - 1))
        def _():
            x = h_ref[...]
            dx, dgn = _rms_bwd(acc[...], x, _rstd(x), g_ref[...])
            dg_ref[...] += _colsum(dgn)
            out_ref[...] = dh_ref[...] + dx

    b_spec = (pl.BlockSpec((None, D, tk), lambda i, p, k: (p, 0, k)) if nt
              else pl.BlockSpec((None, tk, D), lambda i, p, k: (p, k, 0)))
    return _call(
        body, (a, b, dh_in, h, g), name=name, grid=(T // tm, P, nk),
        in_specs=[pl.BlockSpec((None, tm, tk), lambda i, p, k: (p, i, k)), b_spec,
                  pl.BlockSpec((tm, D), lambda i, p, k: (i, 0)), pl.BlockSpec((tm, D), lambda i, p, k: (i, 0)),
                  pl.BlockSpec((1, D), lambda i, p, k: (0, 0))],
        out_specs=[pl.BlockSpec((tm, D), lambda i, p, k: (i, 0)), pl.BlockSpec((1, D), lambda i, p, k: (0, 0))],
        out_shape=[jax.ShapeDtypeStruct((T, D), F32), jax.ShapeDtypeStruct((1, D), F32)],
        scratch_shapes=[pltpu.VMEM((tm, D), F32)], sem=("arbitrary", "arbitrary", "arbitrary"), rider=rider)


def _mm_tn(a, b, *, swiglu=False, b_plane=0, name):
    T, N = b.shape[-2:]
    K = a.shape[2]
    P = 1 if swiglu else a.shape[0]
    tk = _tile(K, 2816, 1024, 512, 256, 128)
    tn = _tile(N, 1792, 1024, 512, 256, 128)
    tt = _tile(T, 512, 256, 128)
    nt_ = T // tt
    nkb = K // tk

    def body(*refs):
        if swiglu:
            g_ref, u_ref, b_ref = refs[:3]
        else:
            a_ref, b_ref = refs[:2]
        out_ref, acc = refs[-2], refs[-1]
        t = pl.program_id(3)

        @pl.when(t == 0)
        def _():
            acc[...] = jnp.zeros_like(acc)

        if swiglu:
            bt = b_ref[...].astype(BF16)
            for c0, c1 in _chunks(tk):
                gt = g_ref[:, c0:c1].astype(F32)
                at = (gt * jax.nn.sigmoid(gt) * u_ref[:, c0:c1].astype(F32)).astype(BF16)
                acc[c0:c1, :] += _dot_tn(at, bt)
        else:
            acc[...] += _dot_tn(a_ref[...].astype(BF16), b_ref[...].astype(BF16))

        @pl.when(t == nt_ - 1)
        def _():
            out_ref[...] = acc[...].astype(BF16)

    if swiglu:
        a_specs = [pl.BlockSpec((None, tt, tk), lambda p, i, j, t: (0, t, i)),
                   pl.BlockSpec((None, tt, tk), lambda p, i, j, t: (1, t, i))]
        a_args = [a, a]
    else:
        a_specs = [pl.BlockSpec((None, tt, tk), lambda p, i, j, t: (p, t, i))]
        a_args = [a]
    if b.ndim == 3:
        in_specs = a_specs + [pl.BlockSpec((None, tt, tn), lambda p, i, j, t: (b_plane, t, j))]
    else:
        in_specs = a_specs + [pl.BlockSpec((tt, tn), lambda p, i, j, t: (t, j))]
    return pl.pallas_call(
        body, name=name, grid=(P, nkb, N // tn, nt_),
        in_specs=in_specs,
        out_specs=pl.BlockSpec((tk, tn), lambda p, i, j, t: (p * nkb + i, j)),
        out_shape=jax.ShapeDtypeStruct((P * K, N), BF16),
        scratch_shapes=[pltpu.VMEM((tk, tn), F32)],
        compiler_params=_params("parallel", "parallel", "parallel", "arbitrary"),
    )(*a_args, b)


def _row(D):
    return pl.BlockSpec((1, D), lambda i: (0, 0))


def _full(shape):
    return pl.BlockSpec(shape, lambda i: (0,) * len(shape))


def _merge_out(ys, proj, bgate, wbrs, wo, h, gpost, *, name):
    T, D = h.shape
    tm = _tile(T, 512, 256, 128)

    def body(ya, yb, yc, yd, g0, g1, g2, g3, bg_ref, wa, wb, wc, wd_, wo_ref, h_ref, gp_ref, hn_ref, o_ref, mg_ref):
        merged = jnp.zeros((tm, D), F32)
        for b, (y_ref, gt_ref, w_ref) in enumerate(zip((ya, yb, yc, yd), (g0, g1, g2, g3), (wa, wb, wc, wd_))):
            gate = jax.nn.sigmoid(gt_ref[...].astype(F32) + bg_ref[b:b + 1, :])
            merged = merged + gate * _dot(y_ref[...], w_ref[...])
        mb = merged.astype(BF16)
        mg_ref[...] = mb
        o = _dot(mb, wo_ref[...])
        o_ref[...] = o
        hn_ref[...] = h_ref[...] + o * _rstd(o) * gp_ref[...]

    tok = lambda w: pl.BlockSpec((tm, w), lambda i: (i, 0))
    gate_specs = [pl.BlockSpec((tm, D), lambda i, b=b: (i, b)) for b in range(N_BRANCH)]
    return pl.pallas_call(
        body, name=name, grid=(T // tm,),
        in_specs=[tok(A_WIDTH), tok(B_WIDTH), tok(C_WIDTH), tok(D_WIDTH)] + gate_specs
        + [_full((N_BRANCH, D))] + [_full(w.shape) for w in wbrs] + [_full((D, D)), tok(D), _row(D)],
        out_specs=[tok(D), tok(D), tok(D)],
        out_shape=[jax.ShapeDtypeStruct((T, D), F32), jax.ShapeDtypeStruct((T, D), F32), jax.ShapeDtypeStruct((T, D), BF16)],
        compiler_params=_params("parallel"),
    )(*ys, proj, proj, proj, proj, bgate, *wbrs, wo, h, gpost)


def _merge_out_bwd(dh, o, gpost, wo, ys, proj, bgate, wbrs, *, name):
    T, D = o.shape
    tm = _tile(T, 256, 128)
    widths = (A_WIDTH, B_WIDTH, C_WIDTH, D_WIDTH)

    def body(dh_ref, o_ref, gp_ref, wo_ref, ya, yb, yc, yd, g0, g1, g2, g3, bg_ref, wa, wb, wc, wd_,
             do_ref, dz_ref, dya, dyb, dyc, dyd, dgt_ref, dgp_ref, dbg_ref):
        @pl.when(pl.program_id(0) == 0)
        def _():
            dgp_ref[...] = jnp.zeros_like(dgp_ref)
            dbg_ref[...] = jnp.zeros_like(dbg_ref)

        x = o_ref[...]
        do, dgn = _rms_bwd(dh_ref[...], x, _rstd(x), gp_ref[...])
        dgp_ref[...] += _colsum(dgn)
        dob = do.astype(BF16)
        do_ref[...] = dob
        dmerged = _dot_nt(dob, wo_ref[...])
        for b, (y_ref, gt_ref, w_ref, dy_ref) in enumerate(zip((ya, yb, yc, yd), (g0, g1, g2, g3), (wa, wb, wc, wd_),
                                                               (dya, dyb, dyc, dyd))):
            gate = jax.nn.sigmoid(gt_ref[...].astype(F32) + bg_ref[b:b + 1, :])
            z = _dot(y_ref[...], w_ref[...])
            dz = (dmerged * gate).astype(BF16)
            dz_ref[b] = dz
            dy_ref[...] = _dot_nt(dz, w_ref[...]).astype(BF16)
            dgate = dmerged * z * gate * (1.0 - gate)
            dgt_ref[:, b * D:(b + 1) * D] = dgate.astype(BF16)
            dbg_ref[b:b + 1, :] += _colsum(dgate)

    tok = lambda w: pl.BlockSpec((tm, w), lambda i: (i, 0))
    gate_specs = [pl.BlockSpec((tm, D), lambda i, b=b: (i, b)) for b in range(N_BRANCH)]
    return pl.pallas_call(
        body, name=name, grid=(T // tm,),
        in_specs=[tok(D), tok(D), _row(D), _full((D, D))] + [tok(w) for w in widths] + gate_specs
        + [_full((N_BRANCH, D))] + [_full(w.shape) for w in wbrs],
        out_specs=[tok(D), pl.BlockSpec((N_BRANCH, tm, D), lambda i: (0, i, 0))] + [tok(w) for w in widths]
        + [tok(N_BRANCH * D), _row(D), _full((N_BRANCH, D))],
        out_shape=[jax.ShapeDtypeStruct((T, D), BF16), jax.ShapeDtypeStruct((N_BRANCH, T, D), BF16)]
        + [jax.ShapeDtypeStruct((T, w), BF16) for w in widths]
        + [jax.ShapeDtypeStruct((T, N_BRANCH * D), BF16), jax.ShapeDtypeStruct((1, D), F32),
           jax.ShapeDtypeStruct((N_BRANCH, D), F32)],
        compiler_params=_params("arbitrary"),
    )(dh, o, gpost, wo, *ys, proj, proj, proj, proj, bgate, *wbrs)


def _ple(h, pe, ggate, wpg, wple, gpost, *, name):
    T, D = h.shape
    E = pe.shape[1]
    tm = _tile(T, 512, 256, 128)

    def body(h_ref, p_ref, gg_ref, wpg_ref, wple_ref, gp_ref, out_ref):
        x = h_ref[...]
        n = (x * _rstd(x) * gg_ref[...]).astype(BF16)
        pg = jax.nn.sigmoid(_dot(n, wpg_ref[...]))
        e = _dot(p_ref[...].astype(BF16), wple_ref[...])
        out_ref[...] = x + pg * (e * _rstd(e) * gp_ref[...])

    tok = lambda w: pl.BlockSpec((tm, w), lambda i: (i, 0))
    return pl.pallas_call(
        body, name=name, grid=(T // tm,),
        in_specs=[tok(D), tok(E), _row(D), _full((D, D)), _full((E, D)), _row(D)],
        out_specs=tok(D), out_shape=jax.ShapeDtypeStruct((T, D), F32),
        compiler_params=_params("parallel"),
    )(h, pe, ggate, wpg, wple, gpost)


def _ple_bwd(dh, h, pe, ggate, wpg, wple, gpost, *, name):
    T, D = h.shape
    E = pe.shape[1]
    tm = _tile(T, 256, 128)

    def body(dh_ref, h_ref, p_ref, gg_ref, wpg_ref, wple_ref, gp_ref, dhi_ref, de_ref, dpgl_ref, n_ref, dgg_ref, dgp_ref):
        @pl.when(pl.program_id(0) == 0)
        def _():
            dgg_ref[...] = jnp.zeros_like(dgg_ref)
            dgp_ref[...] = jnp.zeros_like(dgp_ref)

        dh = dh_ref[...]
        x = h_ref[...]
        r = _rstd(x)
        n = (x * r * gg_ref[...]).astype(BF16)
        n_ref[...] = n
        pg = jax.nn.sigmoid(_dot(n, wpg_ref[...]))
        e = _dot(p_ref[...].astype(BF16), wple_ref[...])
        re = _rstd(e)
        de, dgn = _rms_bwd(dh * pg, e, re, gp_ref[...])
        dgp_ref[...] += _colsum(dgn)
        de_ref[...] = de.astype(BF16)
        dpgl = (dh * (e * re * gp_ref[...]) * pg * (1.0 - pg)).astype(BF16)
        dpgl_ref[...] = dpgl
        dn = _dot_nt(dpgl, wpg_ref[...])
        dx, dgn2 = _rms_bwd(dn, x, r, gg_ref[...])
        dgg_ref[...] += _colsum(dgn2)
        dhi_ref[...] = dh + dx

    tok = lambda w: pl.BlockSpec((tm, w), lambda i: (i, 0))
    return pl.pallas_call(
        body, name=name, grid=(T // tm,),
        in_specs=[tok(D), tok(D), tok(E), _row(D), _full((D, D)), _full((E, D)), _row(D)],
        out_specs=[tok(D), tok(D), tok(D), tok(D), _row(D), _row(D)],
        out_shape=[jax.ShapeDtypeStruct((T, D), F32), jax.ShapeDtypeStruct((T, D), BF16), jax.ShapeDtypeStruct((T, D), BF16),
                   jax.ShapeDtypeStruct((T, D), BF16), jax.ShapeDtypeStruct((1, D), F32), jax.ShapeDtypeStruct((1, D), F32)],
        compiler_params=_params("arbitrary"),
    )(dh, h, pe, ggate, wpg, wple, gpost)


def _loss_head(y, target, *, name):
    T, D = y.shape
    tm = _tile(T, 512, 256, 128)

    def body(y_ref, t_ref, l_ref, dy_ref):
        @pl.when(pl.program_id(0) == 0)
        def _():
            l_ref[...] = jnp.zeros_like(l_ref)

        err = y_ref[...] - t_ref[...]
        dy_ref[...] = err / D
        l_ref[...] += 0.5 * jnp.sum(jnp.mean(err * err, axis=-1, keepdims=True), axis=0, keepdims=True)

    tok = pl.BlockSpec((tm, D), lambda i: (i, 0))
    return pl.pallas_call(
        body, name=name, grid=(T // tm,),
        in_specs=[tok, tok], out_specs=[_full((8, 128)), tok],
        out_shape=[jax.ShapeDtypeStruct((8, 128), F32), jax.ShapeDtypeStruct((T, D), F32)],
        compiler_params=_params("arbitrary"),
    )(y, target)


def _layout(D):
    off = {'gates': 0}
    off['a'] = N_BRANCH * D
    off['b'] = off['a'] + 3 * A_WIDTH
    off['c'] = off['b'] + 3 * B_WIDTH
    off['d'] = off['c'] + 2 * C_WIDTH
    off['af'] = off['d'] + D_WIDTH + 2 * D_KV_WIDTH
    off['end'] = off['af'] + AF_PAD
    return off


def _seq_spec(S, width, col):
    assert col % width == 0
    return pl.BlockSpec((None, S, width), lambda b: (b, 0, col // width))


def _split3(x):
    hi = x.astype(BF16)
    r1 = x - hi.astype(F32)
    mid = r1.astype(BF16)
    lo = (r1 - mid.astype(F32)).astype(BF16)
    return hi, mid, lo


def _fox_cumsum(af_ref, bf_ref, c_scr, ct_scr):
    S = af_ref.shape[0]
    cb = _tile(S, 256, 128)
    tril = (lax.broadcasted_iota(jnp.int32, (cb, cb), 0) >= lax.broadcasted_iota(jnp.int32, (cb, cb), 1)).astype(BF16)
    carry = jnp.zeros((1, AF_LANES), F32)
    for j in range(S // cb):
        rows = slice(j * cb, (j + 1) * cb)
        hi, mid, lo = _split3(jax.nn.log_sigmoid(af_ref[rows, :] + bf_ref[...]))
        cblk = _dot(tril, hi) + _dot(tril, mid) + _dot(tril, lo) + carry
        c_scr[rows, :] = cblk
        carry = cblk[cb - 1:cb, :]
    ct_scr[...] = c_scr[...].T


def _fox_probs(q_ref, k_ref, c_scr, ct_scr, h, i, bq):
    end = (i + 1) * bq
    qs, hs = slice(i * bq, end), slice(HEAD_DIM * h, HEAD_DIM * (h + 1))
    s = _dot_nt(q_ref[qs, hs] * SCALE, k_ref[0:end, hs])
    s = s + (c_scr[qs, h:h + 1] - ct_scr[h:h + 1, 0:end])
    row = i * bq + lax.broadcasted_iota(jnp.int32, (bq, end), 0)
    col = lax.broadcasted_iota(jnp.int32, (bq, end), 1)
    s = jnp.where(row >= col, s, NEG_INF)
    e = jnp.exp(s - jnp.max(s, axis=-1, keepdims=True))
    return e * (1.0 / jnp.sum(e, axis=-1, keepdims=True))


def _fox_fwd(proj3, af3, bfor, off, *, name, rider=None):
    Bn, S, _ = proj3.shape
    bq = _tile(S, 256, 128)

    def body(q_ref, k_ref, v_ref, af_ref, bf_ref, o_ref, c_scr, ct_scr):
        _fox_cumsum(af_ref, bf_ref, c_scr, ct_scr)
        for h in range(A_HEADS):
            hs = slice(HEAD_DIM * h, HEAD_DIM * (h + 1))
            for i in range(S // bq):
                p = _fox_probs(q_ref, k_ref, c_scr, ct_scr, h, i, bq)
                o_ref[i * bq:(i + 1) * bq, hs] = _dot(p.astype(BF16), v_ref[0:(i + 1) * bq, hs]).astype(BF16)

    return _call(
        body, (proj3, proj3, proj3, af3, bfor), name=name, grid=(Bn,),
        in_specs=[_seq_spec(S, A_WIDTH, off['a']), _seq_spec(S, A_WIDTH, off['a'] + A_WIDTH),
                  _seq_spec(S, A_WIDTH, off['a'] + 2 * A_WIDTH), _seq_spec(S, AF_LANES, 0), _full((1, AF_LANES))],
        out_specs=[_seq_spec(S, A_WIDTH, 0)], out_shape=[jax.ShapeDtypeStruct((Bn, S, A_WIDTH), BF16)],
        scratch_shapes=[pltpu.VMEM((S, AF_LANES), F32), pltpu.VMEM((AF_LANES, S), F32)],
        sem=("parallel",), rider=rider)[0]


def _fox_bwd(proj3, af3, bfor, dya3, off, *, name, rider=None):
    Bn, S, _ = proj3.shape
    bq = _tile(S, 256, 128)
    cb = _tile(S, 256, 128)
    scale = HEAD_DIM ** -0.5

    def body(q_ref, k_ref, v_ref, af_ref, bf_ref, do_ref, dqkv_ref, da_ref, dbf_ref,
             c_scr, ct_scr, dk_scr, dv_scr, dc_scr, dct_scr):
        @pl.when(pl.program_id(0) == 0)
        def _():
            dbf_ref[...] = jnp.zeros_like(dbf_ref)

        _fox_cumsum(af_ref, bf_ref, c_scr, ct_scr)
        dk_scr[...] = jnp.zeros_like(dk_scr)
        dv_scr[...] = jnp.zeros_like(dv_scr)
        dc_scr[...] = jnp.zeros_like(dc_scr)
        dct_scr[...] = jnp.zeros_like(dct_scr)
        for h in range(A_HEADS):
            hs = slice(HEAD_DIM * h, HEAD_DIM * (h + 1))
            for i in range(S // bq):
                end = (i + 1) * bq
                qs = slice(i * bq, end)
                p = _fox_probs(q_ref, k_ref, c_scr, ct_scr, h, i, bq)
                doh = do_ref[qs, hs]
                dp = _dot_nt(doh, v_ref[0:end, hs])
                ds = p * (dp - jnp.sum(p * dp, axis=-1, keepdims=True))
                dsb = ds.astype(BF16)
                dqkv_ref[qs, hs] = (_dot(dsb, k_ref[0:end, hs]) * SCALE).astype(BF16)
                dk_scr[0:end, hs] += _dot_tn(dsb, q_ref[qs, hs] * SCALE)
                dv_scr[0:end, hs] += _dot_tn(p.astype(BF16), doh)
                dc_scr[qs, h:h + 1] += jnp.sum(ds, axis=-1, keepdims=True)
                dct_scr[h:h + 1, 0:end] += -jnp.sum(ds, axis=0, keepdims=True)
        dqkv_ref[:, A_WIDTH:2 * A_WIDTH] = dk_scr[...].astype(BF16)
        dqkv_ref[:, 2 * A_WIDTH:3 * A_WIDTH] = dv_scr[...].astype(BF16)
        dc_scr[...] += dct_scr[...].T
        triu = (lax.broadcasted_iota(jnp.int32, (cb, cb), 0) <= lax.broadcasted_iota(jnp.int32, (cb, cb), 1)).astype(BF16)
        carry = jnp.zeros((1, AF_LANES), F32)
        dbf = jnp.zeros((1, AF_LANES), F32)
        for j in reversed(range(S // cb)):
            rows = slice(j * cb, (j + 1) * cb)
            hi, mid, lo = _split3(dc_scr[rows, :])
            dlf = _dot(triu, hi) + _dot(triu, mid) + _dot(triu, lo) + carry
            carry = dlf[0:1, :]
            da = dlf * jax.nn.sigmoid(-(af_ref[rows, :] + bf_ref[...]))
            dbf = dbf + _colsum(da)
            da_ref[rows, 0:AF_LANES] = da.astype(BF16)
        da_ref[:, AF_LANES:AF_PAD] = jnp.zeros((S, AF_PAD - AF_LANES), BF16)
        dbf_ref[...] += dbf

    return _call(
        body, (proj3, proj3, proj3, af3, bfor, dya3), name=name, grid=(Bn,),
        in_specs=[_seq_spec(S, A_WIDTH, off['a']), _seq_spec(S, A_WIDTH, off['a'] + A_WIDTH),
                  _seq_spec(S, A_WIDTH, off['a'] + 2 * A_WIDTH), _seq_spec(S, AF_LANES, 0), _full((1, AF_LANES)),
                  _seq_spec(S, A_WIDTH, 0)],
        out_specs=[_seq_spec(S, 3 * A_WIDTH, 0), _seq_spec(S, AF_PAD, 0), _full((1, AF_LANES))],
        out_shape=[jax.ShapeDtypeStruct((Bn, S, 3 * A_WIDTH), BF16), jax.ShapeDtypeStruct((Bn, S, AF_PAD), BF16),
                   jax.ShapeDtypeStruct((1, AF_LANES), F32)],
        scratch_shapes=[pltpu.VMEM((S, AF_LANES), F32), pltpu.VMEM((AF_LANES, S), F32), pltpu.VMEM((S, A_WIDTH), F32),
                        pltpu.VMEM((S, A_WIDTH), F32), pltpu.VMEM((S, AF_LANES), F32), pltpu.VMEM((AF_LANES, S), F32)],
        sem=("arbitrary",), rider=rider)


def _shift_down(z, s):
    if s == 0:
        return z
    row = lax.broadcasted_iota(jnp.int32, z.shape, 0)
    return jnp.where(row >= s, pltpu.roll(z, s, 0), 0.0)


def _shift_up(z, s):
    if s == 0:
        return z
    n = z.shape[0]
    row = lax.broadcasted_iota(jnp.int32, z.shape, 0)
    return jnp.where(row < n - s, pltpu.roll(z, n - s, 0), 0.0)


def _conv_fwd(z, w_ref, K):
    acc = jnp.zeros_like(z)
    for k in range(K):
        acc = acc + w_ref[k:k + 1, :] * _shift_down(z, K - 1 - k)
    return acc


def _conv_bwd(dy, z, w_ref, dw_ref, K):
    dz = jnp.zeros_like(z)
    for k in range(K):
        dz = dz + w_ref[k:k + 1, :] * _shift_up(dy, K - 1 - k)
        dw_ref[k:k + 1, :] += _colsum(dy * _shift_down(z, K - 1 - k))
    return dz


def _sconv_fwd(proj3, w, off, *, name):
    Bn, S, _ = proj3.shape

    def body(bg_ref, cg_ref, xb_ref, w_ref, o_ref):
        z = cg_ref[...].astype(F32) * xb_ref[...].astype(F32)
        o_ref[...] = (bg_ref[...].astype(F32) * _conv_fwd(z, w_ref, SHORT_CONV)).astype(BF16)

    return pl.pallas_call(
        body, name=name, grid=(Bn,),
        in_specs=[_seq_spec(S, B_WIDTH, off['b'] + j * B_WIDTH) for j in range(3)] + [_full((SHORT_CONV, B_WIDTH))],
        out_specs=_seq_spec(S, B_WIDTH, 0), out_shape=jax.ShapeDtypeStruct((Bn, S, B_WIDTH), BF16),
        compiler_params=_params("parallel"),
    )(proj3, proj3, proj3, w)


def _sconv_bwd(proj3, w, dyb3, off, *, name):
    Bn, S, _ = proj3.shape

    def body(bg_ref, cg_ref, xb_ref, w_ref, do_ref, din_ref, dw_ref):
        @pl.when(pl.program_id(0) == 0)
        def _():
            dw_ref[...] = jnp.zeros_like(dw_ref)

        cg, xb = cg_ref[...].astype(F32), xb_ref[...].astype(F32)
        z = cg * xb
        do = do_ref[...].astype(F32)
        din_ref[:, 0:B_WIDTH] = (do * _conv_fwd(z, w_ref, SHORT_CONV)).astype(BF16)
        dz = _conv_bwd(do * bg_ref[...].astype(F32), z, w_ref, dw_ref, SHORT_CONV)
        din_ref[:, B_WIDTH:2 * B_WIDTH] = (dz * xb).astype(BF16)
        din_ref[:, 2 * B_WIDTH:3 * B_WIDTH] = (dz * cg).astype(BF16)

    return pl.pallas_call(
        body, name=name, grid=(Bn,),
        in_specs=[_seq_spec(S, B_WIDTH, off['b'] + j * B_WIDTH) for j in range(3)]
        + [_full((SHORT_CONV, B_WIDTH)), _seq_spec(S, B_WIDTH, 0)],
        out_specs=[_seq_spec(S, 3 * B_WIDTH, 0), _full((SHORT_CONV, B_WIDTH))],
        out_shape=[jax.ShapeDtypeStruct((Bn, S, 3 * B_WIDTH), BF16), jax.ShapeDtypeStruct((SHORT_CONV, B_WIDTH), F32)],
        compiler_params=_params("arbitrary"),
    )(proj3, proj3, proj3, w, dyb3)


def _cconv_pre(cin_ref, w_ref, cb_ref):
    x = cin_ref[...].astype(F32)
    a, gt = x[:, 0:C_WIDTH], x[:, C_WIDTH:2 * C_WIDTH]
    sg = jax.nn.sigmoid(gt)
    glu = a * sg
    y0 = _conv_fwd(glu, w_ref, CONF_CONV) + cb_ref[...]
    mu = jnp.mean(y0, axis=-1, keepdims=True)
    xc = y0 - mu
    rs = lax.rsqrt(jnp.mean(xc * xc, axis=-1, keepdims=True) + EPS)
    return a, sg, glu, xc * rs, rs


def _cconv_fwd(proj3, w, cbias, lg, lb, off, *, name):
    Bn, S, _ = proj3.shape

    def body(cin_ref, w_ref, cb_ref, lg_ref, lb_ref, o_ref):
        _, _, _, xh, _ = _cconv_pre(cin_ref, w_ref, cb_ref)
        ln = xh * lg_ref[...] + lb_ref[...]
        o_ref[...] = (ln * jax.nn.sigmoid(ln)).astype(BF16)

    return pl.pallas_call(
        body, name=name, grid=(Bn,),
        in_specs=[_seq_spec(S, 2 * C_WIDTH, off['c']), _full((CONF_CONV, C_WIDTH)), _full((1, C_WIDTH)),
                  _full((1, C_WIDTH)), _full((1, C_WIDTH))],
        out_specs=_seq_spec(S, C_WIDTH, 0), out_shape=jax.ShapeDtypeStruct((Bn, S, C_WIDTH), BF16),
        compiler_params=_params("parallel"),
    )(proj3, w, cbias, lg, lb)


def _cconv_bwd(proj3, w, cbias, lg, lb, dyc3, off, *, name):
    Bn, S, _ = proj3.shape

    def body(cin_ref, w_ref, cb_ref, lg_ref, lb_ref, do_ref, din_ref, dw_ref, dcb_ref, dlg_ref, dlb_ref):
        @pl.when(pl.program_id(0) == 0)
        def _():
            for r in (dw_ref, dcb_ref, dlg_ref, dlb_ref):
                r[...] = jnp.zeros_like(r)

        a, sg, glu, xh, rs = _cconv_pre(cin_ref, w_ref, cb_ref)
        ln = xh * lg_ref[...] + lb_ref[...]
        sl = jax.nn.sigmoid(ln)
        dln = do_ref[...].astype(F32) * (sl * (1.0 + ln * (1.0 - sl)))
        dlg_ref[...] += _colsum(dln * xh)
        dlb_ref[...] += _colsum(dln)
        dxh = dln * lg_ref[...]
        dy0 = rs * (dxh - jnp.mean(dxh, axis=-1, keepdims=True) - xh * jnp.mean(dxh * xh, axis=-1, keepdims=True))
        dcb_ref[...] += _colsum(dy0)
        dglu = _conv_bwd(dy0, glu, w_ref, dw_ref, CONF_CONV)
        din_ref[:, 0:C_WIDTH] = (dglu * sg).astype(BF16)
        din_ref[:, C_WIDTH:2 * C_WIDTH] = (dglu * a * sg * (1.0 - sg)).astype(BF16)

    vec = _full((1, C_WIDTH))
    return pl.pallas_call(
        body, name=name, grid=(Bn,),
        in_specs=[_seq_spec(S, 2 * C_WIDTH, off['c']), _full((CONF_CONV, C_WIDTH)), vec, vec, vec, _seq_spec(S, C_WIDTH, 0)],
        out_specs=[_seq_spec(S, 2 * C_WIDTH, 0), _full((CONF_CONV, C_WIDTH)), vec, vec, vec],
        out_shape=[jax.ShapeDtypeStruct((Bn, S, 2 * C_WIDTH), BF16), jax.ShapeDtypeStruct((CONF_CONV, C_WIDTH), F32)]
        + [jax.ShapeDtypeStruct((1, C_WIDTH), F32)] * 3,
        compiler_params=_params("arbitrary"),
    )(proj3, w, cbias, lg, lb, dyc3)


def _swa_band(x_ref, g, nb):
    xb = x_ref[:, HEAD_DIM * g:HEAD_DIM * (g + 1)].reshape(nb, Q_BLOCK, HEAD_DIM)
    prev = jnp.concatenate([jnp.zeros((1, Q_BLOCK, HEAD_DIM), xb.dtype), xb[:-1]], axis=0)
    return jnp.concatenate([prev, xb], axis=1)


def _swa_probs(q_ref, kband, bias_ref, sk_ref, h, nb):
    qh = (q_ref[:, HEAD_DIM * h:HEAD_DIM * (h + 1)] * SCALE).reshape(nb, Q_BLOCK, HEAD_DIM)
    s = jnp.einsum('nqd,nsd->nqs', qh, kband, preferred_element_type=F32) + bias_ref[h][None]
    shape = (nb, Q_BLOCK, 2 * Q_BLOCK)
    n = lax.broadcasted_iota(jnp.int32, shape, 0)
    dist = lax.broadcasted_iota(jnp.int32, shape, 1) + Q_BLOCK - lax.broadcasted_iota(jnp.int32, shape, 2)
    col = lax.broadcasted_iota(jnp.int32, shape, 2)
    valid = (dist >= 0) & (dist < WINDOW) & ((n > 0) | (col >= Q_BLOCK))
    s = jnp.where(valid, s, NEG_INF)
    sink = sk_ref[h:h + 1, 0:1].reshape(1, 1, 1)
    m = jnp.maximum(jnp.max(s, axis=-1, keepdims=True), sink)
    e = jnp.exp(s - m)
    es = jnp.exp(sink - m)
    inv = 1.0 / (jnp.sum(e, axis=-1, keepdims=True) + es)
    return qh, e * inv, es * inv


def _swa_fwd(proj3, band_bias, sinks, off, *, name):
    Bn, S, _ = proj3.shape
    nb = S // Q_BLOCK

    def body(q_ref, k_ref, v_ref, bias_ref, sk_ref, o_ref):
        for g in range(D_KV_HEADS):
            kband, vband = _swa_band(k_ref, g, nb), _swa_band(v_ref, g, nb)
            for h in range(g * D_GROUP, (g + 1) * D_GROUP):
                _, p, _ = _swa_probs(q_ref, kband, bias_ref, sk_ref, h, nb)
                out = jnp.einsum('nqs,nsd->nqd', p.astype(BF16), vband, preferred_element_type=F32)
                o_ref[:, HEAD_DIM * h:HEAD_DIM * (h + 1)] = out.reshape(S, HEAD_DIM).astype(BF16)

    kcol = off['d'] + D_WIDTH
    return pl.pallas_call(
        body, name=name, grid=(Bn,),
        in_specs=[_seq_spec(S, D_WIDTH, off['d']), _seq_spec(S, D_KV_WIDTH, kcol), _seq_spec(S, D_KV_WIDTH, kcol + D_KV_WIDTH),
                  _full((D_Q_HEADS, Q_BLOCK, 2 * Q_BLOCK)), _full((D_Q_HEADS, 128))],
        out_specs=_seq_spec(S, D_WIDTH, 0), out_shape=jax.ShapeDtypeStruct((Bn, S, D_WIDTH), BF16),
        compiler_params=_params("parallel"),
    )(proj3, proj3, proj3, band_bias, sinks)


def _swa_bwd(proj3, band_bias, sinks, dyd3, off, *, name):
    Bn, S, _ = proj3.shape
    nb = S // Q_BLOCK
    scale = HEAD_DIM ** -0.5

    def body(q_ref, k_ref, v_ref, bias_ref, sk_ref, do_ref, dqkv_ref, dband_ref, dsk_ref):
        @pl.when(pl.program_id(0) == 0)
        def _():
            dband_ref[...] = jnp.zeros_like(dband_ref)
            dsk_ref[...] = jnp.zeros_like(dsk_ref)

        def unband(acc):
            prev, cur = acc[:, 0:Q_BLOCK, :], acc[:, Q_BLOCK:2 * Q_BLOCK, :]
            nxt = jnp.concatenate([prev[1:], jnp.zeros((1, Q_BLOCK, HEAD_DIM), F32)], axis=0)
            return (cur + nxt).reshape(S, HEAD_DIM).astype(BF16)

        for g in range(D_KV_HEADS):
            kband, vband = _swa_band(k_ref, g, nb), _swa_band(v_ref, g, nb)
            dkb = jnp.zeros((nb, 2 * Q_BLOCK, HEAD_DIM), F32)
            dvb = jnp.zeros((nb, 2 * Q_BLOCK, HEAD_DIM), F32)
            for h in range(g * D_GROUP, (g + 1) * D_GROUP):
                hs = slice(HEAD_DIM * h, HEAD_DIM * (h + 1))
                qh, p, ps = _swa_probs(q_ref, kband, bias_ref, sk_ref, h, nb)
                doh = do_ref[:, hs].reshape(nb, Q_BLOCK, HEAD_DIM)
                dp = jnp.einsum('nqd,nsd->nqs', doh, vband, preferred_element_type=F32)
                delta = jnp.sum(p * dp, axis=-1, keepdims=True)
                ds = p * (dp - delta)
                dsink = jnp.sum(jnp.sum(-ps * delta, axis=0), axis=0, keepdims=True)
                dsk_ref[h:h + 1, :] += jnp.broadcast_to(dsink, (1, 128))
                dband_ref[h] += jnp.sum(ds, axis=0)
                dsb = ds.astype(BF16)
                dq = jnp.einsum('nqs,nsd->nqd', dsb, kband, preferred_element_type=F32) * scale
                dqkv_ref[:, hs] = dq.reshape(S, HEAD_DIM).astype(BF16)
                dkb = dkb + jnp.einsum('nqs,nqd->nsd', dsb, qh, preferred_element_type=F32)
                dvb = dvb + jnp.einsum('nqs,nqd->nsd', p.astype(BF16), doh, preferred_element_type=F32)
            dqkv_ref[:, D_WIDTH + HEAD_DIM * g:D_WIDTH + HEAD_DIM * (g + 1)] = unband(dkb)
            dqkv_ref[:, D_WIDTH + D_KV_WIDTH + HEAD_DIM * g:D_WIDTH + D_KV_WIDTH + HEAD_DIM * (g + 1)] = unband(dvb)

    kcol = off['d'] + D_WIDTH
    wq = D_WIDTH + 2 * D_KV_WIDTH
    return pl.pallas_call(
        body, name=name, grid=(Bn,),
        in_specs=[_seq_spec(S, D_WIDTH, off['d']), _seq_spec(S, D_KV_WIDTH, kcol), _seq_spec(S, D_KV_WIDTH, kcol + D_KV_WIDTH),
                  _full((D_Q_HEADS, Q_BLOCK, 2 * Q_BLOCK)), _full((D_Q_HEADS, 128)), _seq_spec(S, D_WIDTH, 0)],
        out_specs=[_seq_spec(S, wq, 0), _full((D_Q_HEADS, Q_BLOCK, 2 * Q_BLOCK)), _full((D_Q_HEADS, 128))],
        out_shape=[jax.ShapeDtypeStruct((Bn, S, wq), BF16), jax.ShapeDtypeStruct((D_Q_HEADS, Q_BLOCK, 2 * Q_BLOCK), F32),
                   jax.ShapeDtypeStruct((D_Q_HEADS, 128), F32)],
        compiler_params=_params("arbitrary"),
    )(proj3, proj3, proj3, band_bias, sinks, dyd3)


def _assemble(pieces, *, name):
    T = pieces[0].shape[0]
    widths = [p.shape[1] for p in pieces]
    tm = _tile(T, 512, 256, 128)

    def body(*refs):
        out_ref = refs[-1]
        col = 0
        for r, w in zip(refs[:-1], widths):
            out_ref[:, col:col + w] = r[...]
            col += w

    return pl.pallas_call(
        body, name=name, grid=(T // tm,),
        in_specs=[pl.BlockSpec((tm, w), lambda i: (i, 0)) for w in widths],
        out_specs=pl.BlockSpec((tm, sum(widths)), lambda i: (i, 0)),
        out_shape=jax.ShapeDtypeStruct((T, sum(widths)), BF16),
        compiler_params=_params("parallel"),
    )(*pieces)


def _relbias_grad(dband, onehot, *, name):
    L, H, n = dband.shape
    R = onehot.shape[0]

    def body(d_ref, oh_ref, out_ref):
        tot = d_ref[0]
        for l in range(1, L):
            tot = tot + d_ref[l]
        out_ref[...] = lax.dot_general(oh_ref[...], tot, (((1,), (1,)), ((), ())), preferred_element_type=F32,
                                       precision=lax.Precision.HIGHEST)

    return pl.pallas_call(
        body, name=name, out_shape=jax.ShapeDtypeStruct((R, H), F32),
        compiler_params=pltpu.CompilerParams(vmem_limit_bytes=VMEM_LIMIT_V7X),
    )(dband, onehot)


def _all_reduce(buf, *, name):
    R, C = buf.shape

    def body(x_ref, o_ref, land, send, recv):
        me = _my_id()
        land[pl.ds(me, 1)] = x_ref[...][None]
        sends = []
        for k in range(1, N_DEV):
            to, _ = _peer(k)
            cp = pltpu.make_async_remote_copy(src_ref=x_ref, dst_ref=land.at[me], send_sem=send.at[k - 1],
                                              recv_sem=recv.at[k - 1], device_id=to, device_id_type=MESH)
            cp.start()
            sends.append(cp)
        for k in range(1, N_DEV):
            frm, frm_id = _peer(k)
            pltpu.make_async_remote_copy(src_ref=x_ref, dst_ref=land.at[frm_id], send_sem=send.at[k - 1],
                                         recv_sem=recv.at[k - 1], device_id=frm, device_id_type=MESH).wait_recv()
        for cp in sends:
            cp.wait_send()
        acc = land[0]
        for d in range(1, N_DEV):
            acc = acc + land[d]
        o_ref[...] = acc

    vmem = pl.BlockSpec(memory_space=pltpu.VMEM)
    return pl.pallas_call(
        body, name=name, in_specs=[vmem], out_specs=vmem, out_shape=jax.ShapeDtypeStruct((R, C), F32),
        scratch_shapes=[pltpu.VMEM((N_DEV, R, C), F32), pltpu.SemaphoreType.DMA((N_DEV - 1,)),
                        pltpu.SemaphoreType.DMA((N_DEV - 1,))],
        compiler_params=pltpu.CompilerParams(vmem_limit_bytes=VMEM_LIMIT_V7X),
    )(buf)


def _row_tile(rows, row_bytes, align):
    fits = [t for t in range(align, rows + 1, align) if rows % t == 0]
    small = [t for t in fits if t * row_bytes <= 2**20]
    return max(small) if small else (min(fits) if fits else rows)


def _sum8(recvs, *, name):
    L = len(recvs)
    _, rows, C = recvs[0].shape
    tr = _row_tile(rows, C * 4 * L, 16)

    def body(*refs):
        o_ref = refs[-1]
        for l, r_ref in enumerate(refs[:-1]):
            acc = r_ref[0].astype(F32)
            for d in range(1, N_DEV):
                acc = acc + r_ref[d].astype(F32)
            o_ref[l] = acc

    return pl.pallas_call(
        body, name=name, grid=(rows // tr,),
        in_specs=[pl.BlockSpec((N_DEV, tr, C), lambda i: (0, i, 0))] * L,
        out_specs=pl.BlockSpec((L, tr, C), lambda i: (0, i, 0)),
        out_shape=jax.ShapeDtypeStruct((L, rows, C), F32), compiler_params=_params("parallel"),
    )(*recvs)


def _band_bias(rel_bias, onehot, *, name):
    R, H = rel_bias.shape

    def body(rb_ref, oh_ref, out_ref):
        out_ref[...] = lax.dot_general(rb_ref[...], oh_ref[...], (((0,), (0,)), ((), ())), preferred_element_type=F32,
                                       precision=lax.Precision.HIGHEST)

    return pl.pallas_call(
        body, name=name, out_shape=jax.ShapeDtypeStruct((H, onehot.shape[1]), F32),
        compiler_params=pltpu.CompilerParams(vmem_limit_bytes=VMEM_LIMIT_V7X),
    )(rel_bias, onehot)


def _adamw(g, w, m, v, *, name):
    rows, C = g.shape
    tr = _row_tile(rows, C * 4, 8)

    def body(g_ref, w_ref, m_ref, v_ref, d_ref, mo_ref, vo_ref):
        gt = g_ref[...]
        mn = ADAM_B1 * m_ref[...] + (1.0 - ADAM_B1) * gt
        vn = ADAM_B2 * v_ref[...] + (1.0 - ADAM_B2) * jnp.square(gt)
        m_hat = mn / (1.0 - ADAM_B1 ** ADAM_STEP)
        v_hat = vn / (1.0 - ADAM_B2 ** ADAM_STEP)
        d_ref[...] = -ADAM_LR * (m_hat / (jnp.sqrt(v_hat) + ADAM_EPS) + ADAM_WD * w_ref[...])
        mo_ref[...] = mn
        vo_ref[...] = vn

    spec = pl.BlockSpec((tr, C), lambda i: (i, 0))
    return pl.pallas_call(
        body, name=name, grid=(rows // tr,), in_specs=[spec] * 4, out_specs=[spec] * 3,
        out_shape=[jax.ShapeDtypeStruct((rows, C), F32)] * 3, compiler_params=_params("parallel"),
    )(g, w, m, v)


def _in_splits():
    a_f = 3 * A_WIDTH
    b = a_f + A_HEADS
    c = b + 3 * B_WIDTH
    d = c + 2 * C_WIDTH
    gates = d + D_WIDTH + 2 * D_KV_WIDTH
    return a_f, b, c, d, gates


def _permute_in(w):
    a_f, b, c, d, gates = _in_splits()
    pad = jnp.zeros(w.shape[:-1] + (AF_PAD - A_HEADS,), w.dtype)
    return jnp.concatenate([w[..., gates:], w[..., :a_f], w[..., b:c], w[..., c:d], w[..., d:gates], w[..., a_f:b], pad], axis=-1)


def _unpermute_in(g, D):
    off = _layout(D)
    return jnp.concatenate([g[..., off['a']:off['b']], g[..., off['af']:off['af'] + A_HEADS], g[..., off['b']:off['c']],
                            g[..., off['c']:off['d']], g[..., off['d']:off['af']], g[..., :off['a']]], axis=-1)


def _bucket_onehot():
    dist = np.maximum(np.arange(Q_BLOCK)[:, None] + Q_BLOCK - np.arange(2 * Q_BLOCK)[None, :], 0)
    max_exact = REL_BUCKETS // 2
    large = max_exact + (np.log(np.maximum(dist, 1).astype(np.float32) / np.float32(max_exact))
                         / np.float32(math.log(REL_MAX_DIST / max_exact)) * np.float32(REL_BUCKETS - max_exact)).astype(np.int32)
    bucket = np.where(dist < max_exact, dist, np.minimum(large, REL_BUCKETS - 1))
    return (bucket.reshape(1, -1) == np.arange(REL_BUCKETS)[:, None]).astype(np.float32)


GATHER_SEQ = [('ffn1_up', ['ffn1_w_gu']), ('ffn1_down', ['ffn1_w_down']), ('proj', ['w_in']),
              ('fox', ['w_br_a', 'w_br_b', 'w_br_c', 'w_br_d', 'w_o', 'w_ple_gate', 'w_ple']),
              ('ffn2_up', ['ffn2_w_gu']), ('ffn2_down', ['ffn2_w_down'])]
GATHER_AHEAD = 2
SCATTER_HOSTS = {'ffn2_down_bwd': ['w_ple', 'w_ple_gate'], 'ffn2_up_bwd': ['ffn2_w_gu'],
                 'fox_bwd': ['ffn2_w_down', 'w_o', 'w_br_a', 'w_br_b', 'w_br_c', 'w_br_d'], 'proj_bwd': ['w_in'],
                 'ffn1_down_bwd': [], 'ffn1_up_bwd': ['ffn1_w_gu']}


def _pack(parts):
    flat = jnp.concatenate([q.reshape(-1).astype(F32) for q in parts])
    return jnp.pad(flat, (0, (-flat.shape[0]) % 1024)).reshape(-1, 128)


def _unpack(buf, shapes):
    flat, out, pos = buf.reshape(-1), [], 0
    for s in shapes:
        n = math.prod(s)
        out.append(flat[pos:pos + n].reshape(s))
        pos += n
    return out


def kernel(x, p, ffn1_norm_pre, ffn1_w_gu, ffn1_w_down, ffn1_norm_post, mix_norm_pre, w_in, b_forget, b_gate, conv_short, conv_dw, conv_dw_bias, conv_ln_gain, conv_ln_bias, attn_sinks, rel_bias, w_br_a, w_br_b, w_br_c, w_br_d, w_o, mix_norm_post, ffn2_norm_pre, ffn2_w_gu, ffn2_w_down, ffn2_norm_post, ple_norm_gate, w_ple_gate, w_ple, ple_norm_post, loss_target, m_ffn1_norm_pre, m_ffn1_w_gu, m_ffn1_w_down, m_ffn1_norm_post, m_mix_norm_pre, m_w_in, m_b_forget, m_b_gate, m_conv_short, m_conv_dw, m_conv_dw_bias, m_conv_ln_gain, m_conv_ln_bias, m_attn_sinks, m_rel_bias, m_w_br_a, m_w_br_b, m_w_br_c, m_w_br_d, m_w_o, m_mix_norm_post, m_ffn2_norm_pre, m_ffn2_w_gu, m_ffn2_w_down, m_ffn2_norm_post, m_ple_norm_gate, m_w_ple_gate, m_w_ple, m_ple_norm_post, v_ffn1_norm_pre, v_ffn1_w_gu, v_ffn1_w_down, v_ffn1_norm_post, v_mix_norm_pre, v_w_in, v_b_forget, v_b_gate, v_conv_short, v_conv_dw, v_conv_dw_bias, v_conv_ln_gain, v_conv_ln_bias, v_attn_sinks, v_rel_bias, v_w_br_a, v_w_br_b, v_w_br_c, v_w_br_d, v_w_o, v_mix_norm_post, v_ffn2_norm_pre, v_ffn2_w_gu, v_ffn2_w_down, v_ffn2_norm_post, v_ple_norm_gate, v_w_ple_gate, v_w_ple, v_ple_norm_post):
    a = dict(locals())
    Bn, S, D = x.shape
    T = Bn * S
    L, E = p.shape[0], p.shape[-1]
    F = ffn1_w_down.shape[1] * N_DEV
    off = _layout(D)
    PW = off['end']
    me = _my_id()

    shard = {n: a[n].astype(BF16) for n in BIG}
    shard['ffn1_w_gu'] = jnp.swapaxes(ffn1_w_gu, 1, 2).astype(BF16)
    shard['ffn2_w_gu'] = jnp.swapaxes(ffn2_w_gu, 1, 2).astype(BF16)
    shard['w_in'] = _permute_in(w_in).astype(BF16)
    is_col = lambda names: [n in COL_SHARDED for n in names]
    head = [n for _, names in GATHER_SEQ[:GATHER_AHEAD] for n in names]
    first = _Exchange("gather", [shard[n][0] for n in head], is_col(head))
    full = [dict(zip(head, first.run_alone("gather_head")))] + [{} for _ in range(1, L)]
    hosts = [h_ for h_, _ in GATHER_SEQ]

    def with_gather(fn, host, i, *args, **kw):
        li, lj = divmod(i * len(hosts) + hosts.index(host) + GATHER_AHEAD, len(hosts))
        names = GATHER_SEQ[lj][1]
        rider = _Exchange("gather", [shard[n][li] for n in names], is_col(names)) if li < L else None
        out = fn(*args, rider=rider, **kw)
        if rider is not None:
            full[li].update(zip(names, rider.result))
        return out

    cw = conv_short.shape[2]
    conv_full = [lax.dynamic_update_slice(jnp.zeros(c.shape[:2] + (cw * N_DEV,), F32), c, (0, 0, me * cw))
                 for c in (conv_short, conv_dw)]
    conv_shapes = [c.shape for c in conv_full]
    cs_all, cdw_all = _unpack(_all_reduce(_pack(conv_full), name="gather_conv"), conv_shapes)

    onehot = jnp.asarray(_bucket_onehot())
    band_bias = _band_bias(rel_bias, onehot, name="band_bias").reshape(D_Q_HEADS, Q_BLOCK, 2 * Q_BLOCK)

    def vec(name, i):
        return a[name][i][None]

    def lay(i):
        return dict(
            bfor=jnp.pad(b_forget[i], (0, AF_LANES - A_HEADS))[None], bgate=b_gate[i].reshape(N_BRANCH, D),
            cs=cs_all[i], cdw=cdw_all[i], cb=conv_dw_bias[i][None], lg=conv_ln_gain[i][None], lb=conv_ln_bias[i][None],
            sinks=jnp.broadcast_to(attn_sinks[i][:, None], (D_Q_HEADS, 128)),
            wbrs=[full[i][n] for n in ('w_br_a', 'w_br_b', 'w_br_c', 'w_br_d')], pe=p[i].reshape(T, E))

    def wgu(k, i):
        return full[i][f'ffn{k}_w_gu'].reshape(2, F, D)

    h = x.reshape(T, D)
    saved = []
    for i in range(L):
        s = dict(h0=h)
        s['gu1'], s['n1'] = with_gather(_rms_mm, 'ffn1_up', i, h, vec('ffn1_norm_pre', i), wgu(1, i), nt=True,
                                        out_dtype=BF16, save_n=True, name=f"ffn1_up_{i}")
        s['h1'], s['f1'] = with_gather(_ffn_down, 'ffn1_down', i, s['gu1'], full[i]['ffn1_w_down'], h,
                                       vec('ffn1_norm_post', i), name=f"ffn1_down_{i}")
        win = full[i]['w_in']
        proj, s['u'] = with_gather(_rms_mm, 'proj', i, s['h1'], vec('mix_norm_pre', i), win[None], nt=False,
                                   out_dtype=BF16, save_n=True, name=f"proj_{i}")
        af = _rms_mm(s['h1'], vec('mix_norm_pre', i), win[None, :, off['af']:off['af'] + AF_LANES], nt=False,
                     out_dtype=F32, save_n=False, name=f"proj_forget_{i}")
        s['proj'] = proj.reshape(T, PW)
        s['proj3'], s['af3'] = proj.reshape(Bn, S, PW), af.reshape(Bn, S, AF_LANES)
        q = lay(i)
        ya = with_gather(_fox_fwd, 'fox', i, s['proj3'], s['af3'], q['bfor'], off, name=f"fox_{i}")
        yb = _sconv_fwd(s['proj3'], q['cs'], off, name=f"sconv_{i}")
        yc = _cconv_fwd(s['proj3'], q['cdw'], q['cb'], q['lg'], q['lb'], off, name=f"cconv_{i}")
        yd = _swa_fwd(s['proj3'], band_bias, q['sinks'], off, name=f"swa_{i}")
        s['ys'] = [y.reshape(T, y.shape[-1]) for y in (ya, yb, yc, yd)]
        s['h2'], s['o'], s['merged'] = _merge_out(s['ys'], s['proj'], q['bgate'], q['wbrs'], full[i]['w_o'], s['h1'],
                                                  vec('mix_norm_post', i), name=f"merge_{i}")
        s['gu2'], s['n2'] = with_gather(_rms_mm, 'ffn2_up', i, s['h2'], vec('ffn2_norm_pre', i), wgu(2, i), nt=True,
                                        out_dtype=BF16, save_n=True, name=f"ffn2_up_{i}")
        s['h3'], s['f2'] = with_gather(_ffn_down, 'ffn2_down', i, s['gu2'], full[i]['ffn2_w_down'], s['h2'],
                                       vec('ffn2_norm_post', i), name=f"ffn2_down_{i}")
        h = _ple(s['h3'], q['pe'], vec('ple_norm_gate', i), full[i]['w_ple_gate'], full[i]['w_ple'],
                 vec('ple_norm_post', i), name=f"ple_{i}")
        saved.append(s)

    lpart, dh = _loss_head(h, loss_target.reshape(T, D), name="loss_head")
    loss = lax.psum(lpart[0, 0], MESH_AXES)

    gbuf = [{} for _ in range(L)]
    recv = [{} for _ in range(L)]
    sg = {n: [None] * L for n in WEIGHTS if n not in BIG and n != 'rel_bias'}
    dbands = [None] * L

    def wgrad(n, i, a_, b_, **kw):
        gbuf[i][n] = _mm_tn(a_, b_, name=f"d_{n}_{i}", **kw)

    def with_scatter(fn, host, i, *args, **kw):
        items = [(i, n) for n in SCATTER_HOSTS[host]]
        if host == 'ffn2_down_bwd' and i + 1 < L:
            items.append((i + 1, 'ffn1_w_down'))
        if host == 'ffn1_up_bwd' and i == 0:
            items.append((0, 'ffn1_w_down'))
        rider = _Exchange("scatter", [gbuf[l][n] for l, n in items], is_col([n for _, n in items])) if items else None
        out = fn(*args, rider=rider, **kw)
        if rider is not None:
            for (l, n), r in zip(items, rider.result):
                recv[l][n] = r
        return out

    def ffn_bwd(k, i, dh_out, s, h_in):
        dgu, df, sg[f'ffn{k}_norm_post'][i] = with_scatter(
            _ffn_down_bwd, f'ffn{k}_down_bwd', i, dh_out, s[f'f{k}'], vec(f'ffn{k}_norm_post', i), full[i][f'ffn{k}_w_down'],
            s[f'gu{k}'], name=f"ffn{k}_down_bwd_{i}")
        wgrad(f'ffn{k}_w_down', i, s[f'gu{k}'], df, swiglu=True)
        wgrad(f'ffn{k}_w_gu', i, dgu, s[f'n{k}'])
        dh_in, sg[f'ffn{k}_norm_pre'][i] = with_scatter(
            _mm_rmsbwd, f'ffn{k}_up_bwd', i, dgu, wgu(k, i), dh_out, h_in, vec(f'ffn{k}_norm_pre', i), nt=False,
            name=f"ffn{k}_up_bwd_{i}")
        return dh_in

    for i in reversed(range(L)):
        s, q = saved[i], lay(i)
        dh, de, dpgl, npg, sg['ple_norm_gate'][i], sg['ple_norm_post'][i] = _ple_bwd(
            dh, s['h3'], q['pe'], vec('ple_norm_gate', i), full[i]['w_ple_gate'], full[i]['w_ple'], vec('ple_norm_post', i),
            name=f"ple_bwd_{i}")
        wgrad('w_ple', i, q['pe'][None], de)
        wgrad('w_ple_gate', i, npg[None], dpgl)
        dh = ffn_bwd(2, i, dh, s, s['h2'])
        do, dz, *dys, dgates, sg['mix_norm_post'][i], dbg = _merge_out_bwd(
            dh, s['o'], vec('mix_norm_post', i), full[i]['w_o'], s['ys'], s['proj'], q['bgate'], q['wbrs'],
            name=f"merge_bwd_{i}")
        sg['b_gate'][i] = dbg.reshape(-1)
        wgrad('w_o', i, s['merged'][None], do)
        for b, n in enumerate(('w_br_a', 'w_br_b', 'w_br_c', 'w_br_d')):
            wgrad(n, i, s['ys'][b][None], dz, b_plane=b)
        dy3 = [d.reshape(Bn, S, d.shape[-1]) for d in dys]
        da, daf, dbf = with_scatter(_fox_bwd, 'fox_bwd', i, s['proj3'], s['af3'], q['bfor'], dy3[0], off,
                                    name=f"fox_bwd_{i}")
        sg['b_forget'][i] = dbf[0, :A_HEADS]
        db, sg['conv_short'][i] = _sconv_bwd(s['proj3'], q['cs'], dy3[1], off, name=f"sconv_bwd_{i}")
        dc, sg['conv_dw'][i], sg['conv_dw_bias'][i], sg['conv_ln_gain'][i], sg['conv_ln_bias'][i] = _cconv_bwd(
            s['proj3'], q['cdw'], q['cb'], q['lg'], q['lb'], dy3[2], off, name=f"cconv_bwd_{i}")
        dd, dbands[i], dsk = _swa_bwd(s['proj3'], band_bias, q['sinks'], dy3[3], off, name=f"swa_bwd_{i}")
        sg['attn_sinks'][i] = dsk[:, 0]
        dproj = _assemble([dgates] + [t.reshape(T, t.shape[-1]) for t in (da, db, dc, dd, daf)], name=f"dproj_{i}")
        wgrad('w_in', i, s['u'][None], dproj)
        dh, sg['mix_norm_pre'][i] = with_scatter(
            _mm_rmsbwd, 'proj_bwd', i, dproj[None], full[i]['w_in'][None], dh, s['h1'], vec('mix_norm_pre', i), nt=True,
            name=f"proj_bwd_{i}")
        dh = ffn_bwd(1, i, dh, s, s['h0'])
    grad_x = dh.reshape(Bn, S, D)

    d_rel = _relbias_grad(jnp.stack(dbands).reshape(L, D_Q_HEADS, -1), onehot, name="relbias_grad")
    small_g = [d_rel if n == 'rel_bias' else jnp.stack(sg[n]).reshape(a[n].shape) for n in SMALL]
    n_small_rows = _pack(small_g).shape[0]
    conv_g = [jnp.stack(sg[n]) for n in CONV_SHARDED]
    red = _all_reduce(jnp.concatenate([_pack(small_g), _pack(conv_g)]), name="reduce_small")
    g_small_buf = red[:n_small_rows]
    conv_gfull = _unpack(red[n_small_rows:], conv_shapes)
    conv_gloc = [lax.dynamic_slice_in_dim(g, me * cw, cw, axis=2) for g in conv_gfull]

    grads, deltas, new_m, new_v = {}, {}, {}, {}
    small_shapes = [a[n].shape for n in SMALL]
    res = _adamw(g_small_buf, *[_pack([a[pre + n] for n in SMALL]) for pre in ('', 'm_', 'v_')], name="adamw_small")
    for dst, buf in zip((grads, deltas, new_m, new_v), (g_small_buf,) + tuple(res)):
        dst.update(zip(SMALL, _unpack(buf, small_shapes)))
    loc_shapes = [a[n].shape for n in CONV_SHARDED]
    g_conv_buf = _pack(conv_gloc)
    res = _adamw(g_conv_buf, *[_pack([a[pre + n] for n in CONV_SHARDED]) for pre in ('', 'm_', 'v_')], name="adamw_conv")
    for dst, buf in zip((grads, deltas, new_m, new_v), (g_conv_buf,) + tuple(res)):
        dst.update(zip(CONV_SHARDED, _unpack(buf, loc_shapes)))

    for n in BIG:
        g = _sum8([recv[l][n] for l in range(L)], name=f"sum_{n}")
        if n in ('ffn1_w_gu', 'ffn2_w_gu'):
            g = jnp.swapaxes(g, 1, 2)
        elif n == 'w_in':
            g = _unpermute_in(g, D)
        C = g.shape[-1]
        res = _adamw(g.reshape(-1, C), *[a[pre + n].reshape(-1, C) for pre in ('', 'm_', 'v_')], name=f"adamw_{n}")
        grads[n] = g
        deltas[n], new_m[n], new_v[n] = [t.reshape(g.shape) for t in res]

    return (loss, grad_x, *[grads[n] for n in WEIGHTS], *[deltas[n] for n in WEIGHTS],
            *[new_m[n] for n in WEIGHTS], *[new_v[n] for n in WEIGHTS])
```

```python
import functools
import math

import jax
import jax.numpy as jnp
import numpy as np
from jax import lax
from jax.experimental import pallas as pl
from jax.experimental.pallas import tpu as pltpu

F32 = jnp.float32
BF16 = jnp.bfloat16

EPS = 1e-6
NEG_INF = -1e30
HEAD_DIM = 64
SCALE = HEAD_DIM ** -0.5
A_HEADS = 4
A_WIDTH = A_HEADS * HEAD_DIM
B_WIDTH = 256
C_WIDTH = 256
SHORT_CONV = 3
CONF_CONV = 31
D_Q_HEADS = 8
D_KV_HEADS = 2
D_GROUP = D_Q_HEADS // D_KV_HEADS
D_WIDTH = D_Q_HEADS * HEAD_DIM
D_KV_WIDTH = D_KV_HEADS * HEAD_DIM
WINDOW = 128
Q_BLOCK = 128
N_BRANCH = 4
REL_BUCKETS = 32
REL_MAX_DIST = 128
AF_PAD = 256
AF_LANES = 128

ADAM_LR = 0.001
ADAM_B1 = 0.9
ADAM_B2 = 0.999
ADAM_EPS = 1e-08
ADAM_WD = 0.01
ADAM_STEP = 10

N_DEV = 8
MESH_AXES = ("x", "y", "c")
VMEM_LIMIT_V7X = 56 * 2**20
MESH = pl.DeviceIdType.MESH

WEIGHTS = ['ffn1_norm_pre', 'ffn1_w_gu', 'ffn1_w_down', 'ffn1_norm_post', 'mix_norm_pre', 'w_in', 'b_forget',
           'b_gate', 'conv_short', 'conv_dw', 'conv_dw_bias', 'conv_ln_gain', 'conv_ln_bias', 'attn_sinks',
           'rel_bias', 'w_br_a', 'w_br_b', 'w_br_c', 'w_br_d', 'w_o', 'mix_norm_post', 'ffn2_norm_pre',
           'ffn2_w_gu', 'ffn2_w_down', 'ffn2_norm_post', 'ple_norm_gate', 'w_ple_gate', 'w_ple', 'ple_norm_post']
ARG_NAMES = ['x', 'p'] + WEIGHTS + ['loss_target'] + ['m_' + n for n in WEIGHTS] + ['v_' + n for n in WEIGHTS]
BIG = ['ffn1_w_gu', 'ffn1_w_down', 'w_in', 'w_br_a', 'w_br_b', 'w_br_c', 'w_br_d', 'w_o', 'ffn2_w_gu',
       'ffn2_w_down', 'w_ple_gate', 'w_ple']
COL_SHARDED = ('w_br_a', 'w_br_b', 'w_br_c', 'w_br_d', 'w_ple')
CONV_SHARDED = ('conv_short', 'conv_dw')
SMALL = [n for n in WEIGHTS if n not in BIG and n not in CONV_SHARDED]


def _tile(n, *prefs):
    for t in prefs:
        if n % t == 0:
            return t
    return n


def _chunks(n, width=768):
    return [(c, min(c + width, n)) for c in range(0, n, width)]


def _params(*sem):
    return pltpu.CompilerParams(dimension_semantics=sem, vmem_limit_bytes=VMEM_LIMIT_V7X)


def _dot(a, b):
    return jnp.dot(a, b, preferred_element_type=F32)


def _dot_nt(a, b):
    return lax.dot_general(a, b, (((1,), (1,)), ((), ())), preferred_element_type=F32)


def _dot_tn(a, b):
    return lax.dot_general(a, b, (((0,), (0,)), ((), ())), preferred_element_type=F32)


def _rstd(x):
    return lax.rsqrt(jnp.mean(x * x, axis=-1, keepdims=True) + EPS)


def _rms_bwd(dy, x, r, g):
    xh = x * r
    dxh = dy * g
    dx = r * (dxh - xh * jnp.mean(dxh * xh, axis=-1, keepdims=True))
    return dx, dy * xh


def _colsum(v):
    return jnp.sum(v, axis=0, keepdims=True)


def _my_id():
    return lax.axis_index("x") * 4 + lax.axis_index("y") * 2 + lax.axis_index("c")


def _peer(k):
    coords = []
    for bit, axis in zip((4, 2, 1), MESH_AXES):
        me = lax.axis_index(axis)
        coords.append(1 - me if k & bit else me)
    return tuple(coords), coords[0] * 4 + coords[1] * 2 + coords[2]


def _window(ref, col, d, size):
    start = pl.multiple_of(d * size, 8)
    return ref.at[:, pl.ds(start, size)] if col else ref.at[pl.ds(start, size), :]


ANY = pl.BlockSpec(memory_space=pl.ANY)


class _Exchange:
    def __init__(self, kind, arrays, cols):
        self.kind, self.arrays, self.cols = kind, list(arrays), list(cols)
        n = len(self.arrays)
        if kind == "gather":
            self.sizes = [a.shape[1] if c else a.shape[0] for a, c in zip(self.arrays, cols)]
            self.out_shape = [jax.ShapeDtypeStruct((a.shape[0], a.shape[1] * N_DEV) if c else (a.shape[0] * N_DEV, a.shape[1]),
                                                   a.dtype) for a, c in zip(self.arrays, cols)]
        else:
            self.sizes = [a.shape[1] // N_DEV if c else a.shape[0] // N_DEV for a, c in zip(self.arrays, cols)]
            self.out_shape = [jax.ShapeDtypeStruct((N_DEV, a.shape[0], s) if c else (N_DEV, s, a.shape[1]), a.dtype)
                              for a, c, s in zip(self.arrays, cols, self.sizes)]
        self.scratch = [pltpu.SemaphoreType.DMA((n, N_DEV - 1)), pltpu.SemaphoreType.DMA((n, N_DEV - 1)),
                        pltpu.SemaphoreType.DMA((n,))]
        self.result = None

    def _src(self, ins, w, d):
        return ins[w] if self.kind == "gather" else _window(ins[w], self.cols[w], d, self.sizes[w])

    def _dst(self, outs, w, d):
        return _window(outs[w], self.cols[w], d, self.sizes[w]) if self.kind == "gather" else outs[w].at[d]

    def _copies(self, ins, outs, send, recv, loc):
        me = _my_id()
        n = len(self.arrays)
        two_level = self.kind == "gather"
        local = [pltpu.make_async_copy(self._src(ins, w, me), self._dst(outs, w, me), loc.at[w]) for w in range(n)]
        sends, arrivals, forwards, handed = [], [], [], []
        sibling, _ = _peer(1)
        for k in range(1, N_DEV):
            peer, peer_id = _peer(k)
            for w in range(n):
                sems = dict(send_sem=send.at[w, k - 1], recv_sem=recv.at[w, k - 1])
                if two_level and k > 1 and k % 2 == 1:
                    _, via_id = _peer(k - 1)
                    forwards.append(pltpu.make_async_remote_copy(
                        src_ref=self._dst(outs, w, via_id), dst_ref=self._dst(outs, w, via_id), device_id=sibling,
                        device_id_type=MESH, **sems))
                    handed.append(pltpu.make_async_remote_copy(
                        src_ref=self._dst(outs, w, peer_id), dst_ref=self._dst(outs, w, peer_id), device_id=sibling,
                        device_id_type=MESH, **sems))
                    continue
                at = dict(device_id=peer, device_id_type=MESH, **sems)
                sends.append(pltpu.make_async_remote_copy(src_ref=self._src(ins, w, peer_id), dst_ref=self._dst(outs, w, me), **at))
                arrivals.append(pltpu.make_async_remote_copy(src_ref=self._src(ins, w, me), dst_ref=self._dst(outs, w, peer_id), **at))
        return local, sends, arrivals, forwards, handed

    def start(self, ins, outs, send, recv, loc):
        local, sends, _, _, _ = self._copies(ins, outs, send, recv, loc)
        for cp in local + sends:
            cp.start()

    def wait(self, ins, outs, send, recv, loc):
        local, sends, arrivals, forwards, handed = self._copies(ins, outs, send, recv, loc)
        for cp in arrivals:
            cp.wait_recv()
        for cp in forwards:
            cp.start()
        for cp in handed:
            cp.wait_recv()
        for cp in sends + forwards:
            cp.wait_send()
        for cp in local:
            cp.wait()

    def run_alone(self, name):
        n = len(self.arrays)

        def body(*refs):
            self.start(refs[:n], refs[n:2 * n], *refs[2 * n:])
            self.wait(refs[:n], refs[n:2 * n], *refs[2 * n:])

        self.result = pl.pallas_call(body, name=name, in_specs=[ANY] * n, out_specs=[ANY] * n, out_shape=self.out_shape,
                                     scratch_shapes=self.scratch)(*self.arrays)
        return self.result


def _call(body, args, *, name, grid, in_specs, out_specs, out_shape, scratch_shapes=(), sem, rider=None):
    if rider is None:
        return pl.pallas_call(body, name=name, grid=grid, in_specs=in_specs, out_specs=out_specs, out_shape=out_shape,
                              scratch_shapes=list(scratch_shapes), compiler_params=_params(*sem))(*args)
    n_in, n_out, n_scr, n_r = len(in_specs), len(out_shape), len(scratch_shapes), len(rider.arrays)

    def both(*refs):
        ins, r_in = refs[:n_in], refs[n_in:n_in + n_r]
        outs, r_out = refs[n_in + n_r:n_in + n_r + n_out], refs[n_in + n_r + n_out:n_in + 2 * n_r + n_out]
        scr, sems = refs[n_in + 2 * n_r + n_out:n_in + 2 * n_r + n_out + n_scr], refs[n_in + 2 * n_r + n_out + n_scr:]
        first = functools.reduce(lambda p, q: p & q, [pl.program_id(d) == 0 for d in range(len(grid))])
        last = functools.reduce(lambda p, q: p & q, [pl.program_id(d) == g - 1 for d, g in enumerate(grid)])

        @pl.when(first)
        def _():
            rider.start(r_in, r_out, *sems)

        body(*ins, *outs, *scr)

        @pl.when(last)
        def _():
            rider.wait(r_in, r_out, *sems)

    res = pl.pallas_call(
        both, name=name, grid=grid, in_specs=list(in_specs) + [ANY] * n_r, out_specs=list(out_specs) + [ANY] * n_r,
        out_shape=list(out_shape) + rider.out_shape, scratch_shapes=list(scratch_shapes) + rider.scratch,
        compiler_params=_params(*(("arbitrary",) * len(grid))))(*args, *rider.arrays)
    rider.result = res[n_out:]
    return res[:n_out]


def _rms_mm(h, g, w, *, nt, out_dtype, save_n, name, rider=None, f32_cols=None):
    T, D = h.shape
    P = w.shape[0]
    N = w.shape[1] if nt else w.shape[2]
    tm = _tile(T, 1024, 512, 256, 128)
    tn = _tile(N, 2816, 1792, 2048, 1408, 1024, 512, 256, 128)

    if f32_cols is not None:
        c0, cw = f32_cols
        jt, lo = divmod(c0, tn)
        assert P == 1 and lo + cw <= tn

    def body(h_ref, g_ref, w_ref, y_ref, *rest):
        n_scr = rest[-1]

        @pl.when((pl.program_id(1) == 0) & (pl.program_id(2) == 0))
        def _():
            x = h_ref[...]
            n = (x * _rstd(x) * g_ref[...]).astype(BF16)
            n_scr[...] = n
            if save_n:
                rest[0][...] = n

        wt = w_ref[...]
        y = _dot_nt(n_scr[...], wt) if nt else _dot(n_scr[...], wt)
        y_ref[...] = y.astype(out_dtype)
        if f32_cols is not None:
            @pl.when(pl.program_id(2) == jt)
            def _():
                rest[-2][...] = y[:, lo:lo + cw]

    w_spec = (pl.BlockSpec((None, tn, D), lambda i, p, j: (p, j, 0)) if nt
              else pl.BlockSpec((None, D, tn), lambda i, p, j: (p, 0, j)))
    out_shape = [jax.ShapeDtypeStruct((P, T, N), out_dtype)]
    out_specs = [pl.BlockSpec((None, tm, tn), lambda i, p, j: (p, i, j))]
    if save_n:
        out_shape.append(jax.ShapeDtypeStruct((T, D), BF16))
        out_specs.append(pl.BlockSpec((tm, D), lambda i, p, j: (i, 0)))
    if f32_cols is not None:
        out_shape.append(jax.ShapeDtypeStruct((T, cw), F32))
        out_specs.append(pl.BlockSpec((tm, cw), lambda i, p, j: (i, 0)))
    res = _call(
        body, (h, g, w), name=name, grid=(T // tm, P, N // tn),
        in_specs=[pl.BlockSpec((tm, D), lambda i, p, j: (i, 0)), pl.BlockSpec((1, D), lambda i, p, j: (0, 0)), w_spec],
        out_specs=out_specs, out_shape=out_shape, scratch_shapes=[pltpu.VMEM((tm, D), BF16)],
        sem=("parallel", "arbitrary", "arbitrary"), rider=rider)
    return res if save_n else res[0]


def _ffn_down(gu, wd, h, gpost, *, name, rider=None):
    _, T, F = gu.shape
    D = wd.shape[1]
    tm = _tile(T, 512, 256, 128)
    tk = _tile(F, 2816, 1408, 1024, 512, 256, 128)
    nk = F // tk

    def body(g_ref, u_ref, wd_ref, h_ref, gp_ref, hn_ref, f_ref, acc):
        k = pl.program_id(1)

        @pl.when(k == 0)
        def _():
            acc[...] = jnp.zeros_like(acc)

        part = None
        for c0, c1 in _chunks(tk):
            gt = g_ref[:, c0:c1].astype(F32)
            a = (gt * jax.nn.sigmoid(gt) * u_ref[:, c0:c1].astype(F32)).astype(BF16)
            d = _dot(a, wd_ref[c0:c1, :])
            part = d if part is None else part + d
        acc[...] += part

        @pl.when(k == nk - 1)
        def _():
            f = acc[...]
            f_ref[...] = f
            hn_ref[...] = h_ref[...] + 0.5 * (f * _rstd(f) * gp_ref[...])

    return _call(
        body, (gu, gu, wd, h, gpost), name=name, grid=(T // tm, nk),
        in_specs=[pl.BlockSpec((None, tm, tk), lambda i, k: (0, i, k)), pl.BlockSpec((None, tm, tk), lambda i, k: (1, i, k)),
                  pl.BlockSpec((tk, D), lambda i, k: (k, 0)), pl.BlockSpec((tm, D), lambda i, k: (i, 0)),
                  pl.BlockSpec((1, D), lambda i, k: (0, 0))],
        out_specs=[pl.BlockSpec((tm, D), lambda i, k: (i, 0)), pl.BlockSpec((tm, D), lambda i, k: (i, 0))],
        out_shape=[jax.ShapeDtypeStruct((T, D), F32), jax.ShapeDtypeStruct((T, D), F32)],
        scratch_shapes=[pltpu.VMEM((tm, D), F32)], sem=("parallel", "arbitrary"), rider=rider)


def _ffn_down_bwd(dh, f, gpost, wd, gu, *, name, rider=None):
    _, T, F = gu.shape
    D = wd.shape[1]
    tm = _tile(T, 256, 128)
    tn = _tile(F, 2816, 1408, 1024, 512, 256, 128)

    def body(dh_ref, f_ref, gp_ref, wd_ref, g_ref, u_ref, dgu_ref, df_ref, dgp_ref, df_scr):
        i, j = pl.program_id(0), pl.program_id(1)

        @pl.when((i == 0) & (j == 0))
        def _():
            dgp_ref[...] = jnp.zeros_like(dgp_ref)

        @pl.when(j == 0)
        def _():
            x = f_ref[...]
            dx, dgn = _rms_bwd(0.5 * dh_ref[...], x, _rstd(x), gp_ref[...])
            dgp_ref[...] += _colsum(dgn)
            df = dx.astype(BF16)
            df_scr[...] = df
            df_ref[...] = df

        for c0, c1 in _chunks(tn):
            dact = _dot_nt(df_scr[...], wd_ref[c0:c1, :])
            gt = g_ref[:, c0:c1].astype(F32)
            ut = u_ref[:, c0:c1].astype(F32)
            sg = jax.nn.sigmoid(gt)
            dgu_ref[0, :, c0:c1] = (dact * ut * (sg * (1.0 + gt * (1.0 - sg)))).astype(BF16)
            dgu_ref[1, :, c0:c1] = (dact * (gt * sg)).astype(BF16)

    return _call(
        body, (dh, f, gpost, wd, gu, gu), name=name, grid=(T // tm, F // tn),
        in_specs=[pl.BlockSpec((tm, D), lambda i, j: (i, 0)), pl.BlockSpec((tm, D), lambda i, j: (i, 0)),
                  pl.BlockSpec((1, D), lambda i, j: (0, 0)), pl.BlockSpec((tn, D), lambda i, j: (j, 0)),
                  pl.BlockSpec((None, tm, tn), lambda i, j: (0, i, j)), pl.BlockSpec((None, tm, tn), lambda i, j: (1, i, j))],
        out_specs=[pl.BlockSpec((2, tm, tn), lambda i, j: (0, i, j)), pl.BlockSpec((tm, D), lambda i, j: (i, 0)),
                   pl.BlockSpec((1, D), lambda i, j: (0, 0))],
        out_shape=[jax.ShapeDtypeStruct((2, T, F), BF16), jax.ShapeDtypeStruct((T, D), BF16),
                   jax.ShapeDtypeStruct((1, D), F32)],
        scratch_shapes=[pltpu.VMEM((tm, D), BF16)], sem=("arbitrary", "arbitrary"), rider=rider)


def _mm_rmsbwd(a, b, dh_in, h, g, *, nt, name, rider=None):
    P, T, K = a.shape
    D = h.shape[1]
    tm = _tile(T, 1024, 512, 256, 128)
    tk = _tile(K, 1792, 1408, 2048, 1024, 512, 256, 128)
    nk = K // tk

    def body(a_ref, b_ref, dh_ref, h_ref, g_ref, out_ref, dg_ref, acc):
        i, p, k = pl.program_id(0), pl.program_id(1), pl.program_id(2)

        @pl.when((i == 0) & (p == 0) & (k == 0))
        def _():
            dg_ref[...] = jnp.zeros_like(dg_ref)

        @pl.when((p == 0) & (k == 0))
        def _():
            acc[...] = jnp.zeros_like(acc)

        acc[...] += _dot_nt(a_ref[...], b_ref[...]) if nt else _dot(a_ref[...], b_ref[...])

        @pl.when((p == P - 1) & (k == nk - 1))
        def _():
            x = h_ref[...]
            dx, dgn = _rms_bwd(acc[...], x, _rstd(x), g_ref[...])
            dg_ref[...] += _colsum(dgn)
            out_ref[...] = dh_ref[...] + dx

    b_spec = (pl.BlockSpec((None, D, tk), lambda i, p, k: (p, 0, k)) if nt
              else pl.BlockSpec((None, tk, D), lambda i, p, k: (p, k, 0)))
    return _call(
        body, (a, b, dh_in, h, g), name=name, grid=(T // tm, P, nk),
        in_specs=[pl.BlockSpec((None, tm, tk), lambda i, p, k: (p, i, k)), b_spec,
                  pl.BlockSpec((tm, D), lambda i, p, k: (i, 0)), pl.BlockSpec((tm, D), lambda i, p, k: (i, 0)),
                  pl.BlockSpec((1, D), lambda i, p, k: (0, 0))],
        out_specs=[pl.BlockSpec((tm, D), lambda i, p, k: (i, 0)), pl.BlockSpec((1, D), lambda i, p, k: (0, 0))],
        out_shape=[jax.ShapeDtypeStruct((T, D), F32), jax.ShapeDtypeStruct((1, D), F32)],
        scratch_shapes=[pltpu.VMEM((tm, D), F32)], sem=("arbitrary", "arbitrary", "arbitrary"), rider=rider)


def _mm_tn(a, b, *, swiglu=False, b_plane=0, name, rider=None):
    T, N = b.shape[-2:]
    K = a.shape[2]
    P = 1 if swiglu else a.shape[0]
    tk = _tile(K, 2816, 1024, 512, 256, 128)
    tn = _tile(N, 1792, 1024, 512, 256, 128)
    tt = _tile(T, 512, 256, 128)
    nt_ = T // tt
    nkb = K // tk

    def body(*refs):
        if swiglu:
            g_ref, u_ref, b_ref = refs[:3]
        else:
            a_ref, b_ref = refs[:2]
        out_ref, acc = refs[-2], refs[-1]
        t = pl.program_id(3)

        @pl.when(t == 0)
        def _():
            acc[...] = jnp.zeros_like(acc)

        if swiglu:
            bt = b_ref[...].astype(BF16)
            for c0, c1 in _chunks(tk):
                gt = g_ref[:, c0:c1].astype(F32)
                at = (gt * jax.nn.sigmoid(gt) * u_ref[:, c0:c1].astype(F32)).astype(BF16)
                acc[c0:c1, :] += _dot_tn(at, bt)
        else:
            acc[...] += _dot_tn(a_ref[...].astype(BF16), b_ref[...].astype(BF16))

        @pl.when(t == nt_ - 1)
        def _():
            out_ref[...] = acc[...].astype(BF16)

    if swiglu:
        a_specs = [pl.BlockSpec((None, tt, tk), lambda p, i, j, t: (0, t, i)),
                   pl.BlockSpec((None, tt, tk), lambda p, i, j, t: (1, t, i))]
        a_args = [a, a]
    else:
        a_specs = [pl.BlockSpec((None, tt, tk), lambda p, i, j, t: (p, t, i))]
        a_args = [a]
    if b.ndim == 3:
        in_specs = a_specs + [pl.BlockSpec((None, tt, tn), lambda p, i, j, t: (b_plane, t, j))]
    else:
        in_specs = a_specs + [pl.BlockSpec((tt, tn), lambda p, i, j, t: (t, j))]
    return _call(
        body, (*a_args, b), name=name, grid=(P, nkb, N // tn, nt_), in_specs=in_specs,
        out_specs=[pl.BlockSpec((tk, tn), lambda p, i, j, t: (p * nkb + i, j))],
        out_shape=[jax.ShapeDtypeStruct((P * K, N), BF16)], scratch_shapes=[pltpu.VMEM((tk, tn), F32)],
        sem=("parallel", "parallel", "parallel", "arbitrary"), rider=rider)[0]


def _row(D):
    return pl.BlockSpec((1, D), lambda i: (0, 0))


def _full(shape):
    return pl.BlockSpec(shape, lambda i: (0,) * len(shape))


def _merge_out(ys, proj, bgate, wbrs, wo, h, gpost, *, name):
    T, D = h.shape
    tm = _tile(T, 512, 256, 128)

    def body(ya, yb, yc, yd, g0, g1, g2, g3, bg_ref, wa, wb, wc, wd_, wo_ref, h_ref, gp_ref, hn_ref, o_ref, mg_ref):
        merged = jnp.zeros((tm, D), F32)
        for b, (y_ref, gt_ref, w_ref) in enumerate(zip((ya, yb, yc, yd), (g0, g1, g2, g3), (wa, wb, wc, wd_))):
            gate = jax.nn.sigmoid(gt_ref[...].astype(F32) + bg_ref[b:b + 1, :])
            merged = merged + gate * _dot(y_ref[...], w_ref[...])
        mb = merged.astype(BF16)
        mg_ref[...] = mb
        o = _dot(mb, wo_ref[...])
        o_ref[...] = o
        hn_ref[...] = h_ref[...] + o * _rstd(o) * gp_ref[...]

    tok = lambda w: pl.BlockSpec((tm, w), lambda i: (i, 0))
    gate_specs = [pl.BlockSpec((tm, D), lambda i, b=b: (i, b)) for b in range(N_BRANCH)]
    return pl.pallas_call(
        body, name=name, grid=(T // tm,),
        in_specs=[tok(A_WIDTH), tok(B_WIDTH), tok(C_WIDTH), tok(D_WIDTH)] + gate_specs
        + [_full((N_BRANCH, D))] + [_full(w.shape) for w in wbrs] + [_full((D, D)), tok(D), _row(D)],
        out_specs=[tok(D), tok(D), tok(D)],
        out_shape=[jax.ShapeDtypeStruct((T, D), F32), jax.ShapeDtypeStruct((T, D), F32), jax.ShapeDtypeStruct((T, D), BF16)],
        compiler_params=_params("parallel"),
    )(*ys, proj, proj, proj, proj, bgate, *wbrs, wo, h, gpost)


def _merge_out_bwd(dh, o, gpost, wo, ys, proj, bgate, wbrs, *, name):
    T, D = o.shape
    tm = _tile(T, 256, 128)
    widths = (A_WIDTH, B_WIDTH, C_WIDTH, D_WIDTH)

    def body(dh_ref, o_ref, gp_ref, wo_ref, ya, yb, yc, yd, g0, g1, g2, g3, bg_ref, wa, wb, wc, wd_,
             do_ref, dz_ref, dya, dyb, dyc, dyd, dgt_ref, dgp_ref, dbg_ref):
        @pl.when(pl.program_id(0) == 0)
        def _():
            dgp_ref[...] = jnp.zeros_like(dgp_ref)
            dbg_ref[...] = jnp.zeros_like(dbg_ref)

        x = o_ref[...]
        do, dgn = _rms_bwd(dh_ref[...], x, _rstd(x), gp_ref[...])
        dgp_ref[...] += _colsum(dgn)
        dob = do.astype(BF16)
        do_ref[...] = dob
        dmerged = _dot_nt(dob, wo_ref[...])
        for b, (y_ref, gt_ref, w_ref, dy_ref) in enumerate(zip((ya, yb, yc, yd), (g0, g1, g2, g3), (wa, wb, wc, wd_),
                                                               (dya, dyb, dyc, dyd))):
            gate = jax.nn.sigmoid(gt_ref[...].astype(F32) + bg_ref[b:b + 1, :])
            z = _dot(y_ref[...], w_ref[...])
            dz = (dmerged * gate).astype(BF16)
            dz_ref[b] = dz
            dy_ref[...] = _dot_nt(dz, w_ref[...]).astype(BF16)
            dgate = dmerged * z * gate * (1.0 - gate)
            dgt_ref[:, b * D:(b + 1) * D] = dgate.astype(BF16)
            dbg_ref[b:b + 1, :] += _colsum(dgate)

    tok = lambda w: pl.BlockSpec((tm, w), lambda i: (i, 0))
    gate_specs = [pl.BlockSpec((tm, D), lambda i, b=b: (i, b)) for b in range(N_BRANCH)]
    return pl.pallas_call(
        body, name=name, grid=(T // tm,),
        in_specs=[tok(D), tok(D), _row(D), _full((D, D))] + [tok(w) for w in widths] + gate_specs
        + [_full((N_BRANCH, D))] + [_full(w.shape) for w in wbrs],
        out_specs=[tok(D), pl.BlockSpec((N_BRANCH, tm, D), lambda i: (0, i, 0))] + [tok(w) for w in widths]
        + [tok(N_BRANCH * D), _row(D), _full((N_BRANCH, D))],
        out_shape=[jax.ShapeDtypeStruct((T, D), BF16), jax.ShapeDtypeStruct((N_BRANCH, T, D), BF16)]
        + [jax.ShapeDtypeStruct((T, w), BF16) for w in widths]
        + [jax.ShapeDtypeStruct((T, proj.shape[1]), BF16), jax.ShapeDtypeStruct((1, D), F32),
           jax.ShapeDtypeStruct((N_BRANCH, D), F32)],
        compiler_params=_params("arbitrary"),
    )(dh, o, gpost, wo, *ys, proj, proj, proj, proj, bgate, *wbrs)


def _ple(h, pe, ggate, wpg, wple, gpost, *, name):
    T, D = h.shape
    E = pe.shape[1]
    tm = _tile(T, 512, 256, 128)

    def body(h_ref, p_ref, gg_ref, wpg_ref, wple_ref, gp_ref, out_ref):
        x = h_ref[...]
        n = (x * _rstd(x) * gg_ref[...]).astype(BF16)
        pg = jax.nn.sigmoid(_dot(n, wpg_ref[...]))
        e = _dot(p_ref[...].astype(BF16), wple_ref[...])
        out_ref[...] = x + pg * (e * _rstd(e) * gp_ref[...])

    tok = lambda w: pl.BlockSpec((tm, w), lambda i: (i, 0))
    return pl.pallas_call(
        body, name=name, grid=(T // tm,),
        in_specs=[tok(D), tok(E), _row(D), _full((D, D)), _full((E, D)), _row(D)],
        out_specs=tok(D), out_shape=jax.ShapeDtypeStruct((T, D), F32),
        compiler_params=_params("parallel"),
    )(h, pe, ggate, wpg, wple, gpost)


def _ple_bwd(dh, h, pe, ggate, wpg, wple, gpost, *, name):
    T, D = h.shape
    E = pe.shape[1]
    tm = _tile(T, 256, 128)

    def body(dh_ref, h_ref, p_ref, gg_ref, wpg_ref, wple_ref, gp_ref, dhi_ref, de_ref, dpgl_ref, n_ref, dgg_ref, dgp_ref):
        @pl.when(pl.program_id(0) == 0)
        def _():
            dgg_ref[...] = jnp.zeros_like(dgg_ref)
            dgp_ref[...] = jnp.zeros_like(dgp_ref)

        dh = dh_ref[...]
        x = h_ref[...]
        r = _rstd(x)
        n = (x * r * gg_ref[...]).astype(BF16)
        n_ref[...] = n
        pg = jax.nn.sigmoid(_dot(n, wpg_ref[...]))
        e = _dot(p_ref[...].astype(BF16), wple_ref[...])
        re = _rstd(e)
        de, dgn = _rms_bwd(dh * pg, e, re, gp_ref[...])
        dgp_ref[...] += _colsum(dgn)
        de_ref[...] = de.astype(BF16)
        dpgl = (dh * (e * re * gp_ref[...]) * pg * (1.0 - pg)).astype(BF16)
        dpgl_ref[...] = dpgl
        dn = _dot_nt(dpgl, wpg_ref[...])
        dx, dgn2 = _rms_bwd(dn, x, r, gg_ref[...])
        dgg_ref[...] += _colsum(dgn2)
        dhi_ref[...] = dh + dx

    tok = lambda w: pl.BlockSpec((tm, w), lambda i: (i, 0))
    return pl.pallas_call(
        body, name=name, grid=(T // tm,),
        in_specs=[tok(D), tok(D), tok(E), _row(D), _full((D, D)), _full((E, D)), _row(D)],
        out_specs=[tok(D), tok(D), tok(D), tok(D), _row(D), _row(D)],
        out_shape=[jax.ShapeDtypeStruct((T, D), F32), jax.ShapeDtypeStruct((T, D), BF16), jax.ShapeDtypeStruct((T, D), BF16),
                   jax.ShapeDtypeStruct((T, D), BF16), jax.ShapeDtypeStruct((1, D), F32), jax.ShapeDtypeStruct((1, D), F32)],
        compiler_params=_params("arbitrary"),
    )(dh, h, pe, ggate, wpg, wple, gpost)


def _loss_head(y, target, *, name):
    T, D = y.shape
    tm = _tile(T, 512, 256, 128)

    def body(y_ref, t_ref, l_ref, dy_ref):
        @pl.when(pl.program_id(0) == 0)
        def _():
            l_ref[...] = jnp.zeros_like(l_ref)

        err = y_ref[...] - t_ref[...]
        dy_ref[...] = err / D
        l_ref[...] += 0.5 * jnp.sum(jnp.mean(err * err, axis=-1, keepdims=True), axis=0, keepdims=True)

    tok = pl.BlockSpec((tm, D), lambda i: (i, 0))
    return pl.pallas_call(
        body, name=name, grid=(T // tm,),
        in_specs=[tok, tok], out_specs=[_full((8, 128)), tok],
        out_shape=[jax.ShapeDtypeStruct((8, 128), F32), jax.ShapeDtypeStruct((T, D), F32)],
        compiler_params=_params("arbitrary"),
    )(y, target)


def _layout(D):
    off = {'gates': 0}
    off['a'] = N_BRANCH * D
    off['b'] = off['a'] + 3 * A_WIDTH
    off['c'] = off['b'] + 3 * B_WIDTH
    off['d'] = off['c'] + 2 * C_WIDTH
    off['af'] = off['d'] + D_WIDTH + 2 * D_KV_WIDTH
    off['end'] = off['af'] + AF_PAD
    return off


def _seq_spec(S, width, col):
    assert col % width == 0
    return pl.BlockSpec((None, S, width), lambda b: (b, 0, col // width))


def _split3(x):
    hi = x.astype(BF16)
    r1 = x - hi.astype(F32)
    mid = r1.astype(BF16)
    lo = (r1 - mid.astype(F32)).astype(BF16)
    return hi, mid, lo


def _fox_cumsum(af_ref, bf_ref, c_scr, ct_scr):
    S = af_ref.shape[0]
    cb = _tile(S, 256, 128)
    tril = (lax.broadcasted_iota(jnp.int32, (cb, cb), 0) >= lax.broadcasted_iota(jnp.int32, (cb, cb), 1)).astype(BF16)
    carry = jnp.zeros((1, AF_LANES), F32)
    for j in range(S // cb):
        rows = slice(j * cb, (j + 1) * cb)
        hi, mid, lo = _split3(jax.nn.log_sigmoid(af_ref[rows, :] + bf_ref[...]))
        cblk = _dot(tril, hi) + _dot(tril, mid) + _dot(tril, lo) + carry
        c_scr[rows, :] = cblk
        carry = cblk[cb - 1:cb, :]
    ct_scr[...] = c_scr[...].T


def _fox_probs(q_ref, k_ref, c_scr, ct_scr, h, i, bq):
    end = (i + 1) * bq
    qs, hs = slice(i * bq, end), slice(HEAD_DIM * h, HEAD_DIM * (h + 1))
    s = _dot_nt(q_ref[qs, hs] * SCALE, k_ref[0:end, hs])
    s = s + (c_scr[qs, h:h + 1] - ct_scr[h:h + 1, 0:end])
    row = i * bq + lax.broadcasted_iota(jnp.int32, (bq, end), 0)
    col = lax.broadcasted_iota(jnp.int32, (bq, end), 1)
    s = jnp.where(row >= col, s, NEG_INF)
    e = jnp.exp(s - jnp.max(s, axis=-1, keepdims=True))
    return e * (1.0 / jnp.sum(e, axis=-1, keepdims=True))


def _fox_fwd(proj3, af3, bfor, off, *, name, rider=None):
    Bn, S, _ = proj3.shape
    bq = _tile(S, 256, 128)

    def body(q_ref, k_ref, v_ref, af_ref, bf_ref, o_ref, c_scr, ct_scr):
        _fox_cumsum(af_ref, bf_ref, c_scr, ct_scr)
        for h in range(A_HEADS):
            hs = slice(HEAD_DIM * h, HEAD_DIM * (h + 1))
            for i in range(S // bq):
                p = _fox_probs(q_ref, k_ref, c_scr, ct_scr, h, i, bq)
                o_ref[i * bq:(i + 1) * bq, hs] = _dot(p.astype(BF16), v_ref[0:(i + 1) * bq, hs]).astype(BF16)

    return _call(
        body, (proj3, proj3, proj3, af3, bfor), name=name, grid=(Bn,),
        in_specs=[_seq_spec(S, A_WIDTH, off['a']), _seq_spec(S, A_WIDTH, off['a'] + A_WIDTH),
                  _seq_spec(S, A_WIDTH, off['a'] + 2 * A_WIDTH), _seq_spec(S, AF_LANES, 0), _full((1, AF_LANES))],
        out_specs=[_seq_spec(S, A_WIDTH, 0)], out_shape=[jax.ShapeDtypeStruct((Bn, S, A_WIDTH), BF16)],
        scratch_shapes=[pltpu.VMEM((S, AF_LANES), F32), pltpu.VMEM((AF_LANES, S), F32)],
        sem=("parallel",), rider=rider)[0]


def _fox_bwd(proj3, af3, bfor, dya3, off, *, name, rider=None):
    Bn, S, _ = proj3.shape
    bq = _tile(S, 256, 128)
    cb = _tile(S, 256, 128)
    scale = HEAD_DIM ** -0.5

    def body(q_ref, k_ref, v_ref, af_ref, bf_ref, do_ref, dqkv_ref, da_ref, dbf_ref,
             c_scr, ct_scr, dk_scr, dv_scr, dc_scr, dct_scr):
        @pl.when(pl.program_id(0) == 0)
        def _():
            dbf_ref[...] = jnp.zeros_like(dbf_ref)

        _fox_cumsum(af_ref, bf_ref, c_scr, ct_scr)
        dk_scr[...] = jnp.zeros_like(dk_scr)
        dv_scr[...] = jnp.zeros_like(dv_scr)
        dc_scr[...] = jnp.zeros_like(dc_scr)
        dct_scr[...] = jnp.zeros_like(dct_scr)
        for h in range(A_HEADS):
            hs = slice(HEAD_DIM * h, HEAD_DIM * (h + 1))
            for i in range(S // bq):
                end = (i + 1) * bq
                qs = slice(i * bq, end)
                p = _fox_probs(q_ref, k_ref, c_scr, ct_scr, h, i, bq)
                doh = do_ref[qs, hs]
                dp = _dot_nt(doh, v_ref[0:end, hs])
                ds = p * (dp - jnp.sum(p * dp, axis=-1, keepdims=True))
                dsb = ds.astype(BF16)
                dqkv_ref[qs, hs] = (_dot(dsb, k_ref[0:end, hs]) * SCALE).astype(BF16)
                dk_scr[0:end, hs] += _dot_tn(dsb, q_ref[qs, hs] * SCALE)
                dv_scr[0:end, hs] += _dot_tn(p.astype(BF16), doh)
                dc_scr[qs, h:h + 1] += jnp.sum(ds, axis=-1, keepdims=True)
                dct_scr[h:h + 1, 0:end] += -jnp.sum(ds, axis=0, keepdims=True)
        dqkv_ref[:, A_WIDTH:2 * A_WIDTH] = dk_scr[...].astype(BF16)
        dqkv_ref[:, 2 * A_WIDTH:3 * A_WIDTH] = dv_scr[...].astype(BF16)
        dc_scr[...] += dct_scr[...].T
        triu = (lax.broadcasted_iota(jnp.int32, (cb, cb), 0) <= lax.broadcasted_iota(jnp.int32, (cb, cb), 1)).astype(BF16)
        carry = jnp.zeros((1, AF_LANES), F32)
        dbf = jnp.zeros((1, AF_LANES), F32)
        for j in reversed(range(S // cb)):
            rows = slice(j * cb, (j + 1) * cb)
            hi, mid, lo = _split3(dc_scr[rows, :])
            dlf = _dot(triu, hi) + _dot(triu, mid) + _dot(triu, lo) + carry
            carry = dlf[0:1, :]
            da = dlf * jax.nn.sigmoid(-(af_ref[rows, :] + bf_ref[...]))
            dbf = dbf + _colsum(da)
            da_ref[rows, 0:AF_LANES] = da.astype(BF16)
        da_ref[:, AF_LANES:AF_PAD] = jnp.zeros((S, AF_PAD - AF_LANES), BF16)
        dbf_ref[...] += dbf

    return _call(
        body, (proj3, proj3, proj3, af3, bfor, dya3), name=name, grid=(Bn,),
        in_specs=[_seq_spec(S, A_WIDTH, off['a']), _seq_spec(S, A_WIDTH, off['a'] + A_WIDTH),
                  _seq_spec(S, A_WIDTH, off['a'] + 2 * A_WIDTH), _seq_spec(S, AF_LANES, 0), _full((1, AF_LANES)),
                  _seq_spec(S, A_WIDTH, 0)],
        out_specs=[_seq_spec(S, 3 * A_WIDTH, 0), _seq_spec(S, AF_PAD, 0), _full((1, AF_LANES))],
        out_shape=[jax.ShapeDtypeStruct((Bn, S, 3 * A_WIDTH), BF16), jax.ShapeDtypeStruct((Bn, S, AF_PAD), BF16),
                   jax.ShapeDtypeStruct((1, AF_LANES), F32)],
        scratch_shapes=[pltpu.VMEM((S, AF_LANES), F32), pltpu.VMEM((AF_LANES, S), F32), pltpu.VMEM((S, A_WIDTH), F32),
                        pltpu.VMEM((S, A_WIDTH), F32), pltpu.VMEM((S, AF_LANES), F32), pltpu.VMEM((AF_LANES, S), F32)],
        sem=("arbitrary",), rider=rider)


def _shift_down(z, s):
    if s == 0:
        return z
    row = lax.broadcasted_iota(jnp.int32, z.shape, 0)
    return jnp.where(row >= s, pltpu.roll(z, s, 0), 0.0)


def _shift_up(z, s):
    if s == 0:
        return z
    n = z.shape[0]
    row = lax.broadcasted_iota(jnp.int32, z.shape, 0)
    return jnp.where(row < n - s, pltpu.roll(z, n - s, 0), 0.0)


def _conv_fwd(z, w_ref, K):
    acc = jnp.zeros_like(z)
    for k in range(K):
        acc = acc + w_ref[k:k + 1, :] * _shift_down(z, K - 1 - k)
    return acc


def _conv_bwd(dy, z, w_ref, dw_ref, K):
    dz = jnp.zeros_like(z)
    for k in range(K):
        dz = dz + w_ref[k:k + 1, :] * _shift_up(dy, K - 1 - k)
        dw_ref[k:k + 1, :] += _colsum(dy * _shift_down(z, K - 1 - k))
    return dz


def _sconv_fwd(proj3, w, off, *, name):
    Bn, S, _ = proj3.shape

    def body(bg_ref, cg_ref, xb_ref, w_ref, o_ref):
        z = cg_ref[...].astype(F32) * xb_ref[...].astype(F32)
        o_ref[...] = (bg_ref[...].astype(F32) * _conv_fwd(z, w_ref, SHORT_CONV)).astype(BF16)

    return pl.pallas_call(
        body, name=name, grid=(Bn,),
        in_specs=[_seq_spec(S, B_WIDTH, off['b'] + j * B_WIDTH) for j in range(3)] + [_full((SHORT_CONV, B_WIDTH))],
        out_specs=_seq_spec(S, B_WIDTH, 0), out_shape=jax.ShapeDtypeStruct((Bn, S, B_WIDTH), BF16),
        compiler_params=_params("parallel"),
    )(proj3, proj3, proj3, w)


def _sconv_bwd(proj3, w, dyb3, off, *, name):
    Bn, S, _ = proj3.shape

    def body(bg_ref, cg_ref, xb_ref, w_ref, do_ref, din_ref, dw_ref):
        @pl.when(pl.program_id(0) == 0)
        def _():
            dw_ref[...] = jnp.zeros_like(dw_ref)

        cg, xb = cg_ref[...].astype(F32), xb_ref[...].astype(F32)
        z = cg * xb
        do = do_ref[...].astype(F32)
        din_ref[:, 0:B_WIDTH] = (do * _conv_fwd(z, w_ref, SHORT_CONV)).astype(BF16)
        dz = _conv_bwd(do * bg_ref[...].astype(F32), z, w_ref, dw_ref, SHORT_CONV)
        din_ref[:, B_WIDTH:2 * B_WIDTH] = (dz * xb).astype(BF16)
        din_ref[:, 2 * B_WIDTH:3 * B_WIDTH] = (dz * cg).astype(BF16)

    return pl.pallas_call(
        body, name=name, grid=(Bn,),
        in_specs=[_seq_spec(S, B_WIDTH, off['b'] + j * B_WIDTH) for j in range(3)]
        + [_full((SHORT_CONV, B_WIDTH)), _seq_spec(S, B_WIDTH, 0)],
        out_specs=[_seq_spec(S, 3 * B_WIDTH, 0), _full((SHORT_CONV, B_WIDTH))],
        out_shape=[jax.ShapeDtypeStruct((Bn, S, 3 * B_WIDTH), BF16), jax.ShapeDtypeStruct((SHORT_CONV, B_WIDTH), F32)],
        compiler_params=_params("arbitrary"),
    )(proj3, proj3, proj3, w, dyb3)


def _cconv_pre(cin_ref, w_ref, cb_ref):
    x = cin_ref[...].astype(F32)
    a, gt = x[:, 0:C_WIDTH], x[:, C_WIDTH:2 * C_WIDTH]
    sg = jax.nn.sigmoid(gt)
    glu = a * sg
    y0 = _conv_fwd(glu, w_ref, CONF_CONV) + cb_ref[...]
    mu = jnp.mean(y0, axis=-1, keepdims=True)
    xc = y0 - mu
    rs = lax.rsqrt(jnp.mean(xc * xc, axis=-1, keepdims=True) + EPS)
    return a, sg, glu, xc * rs, rs


def _cconv_fwd(proj3, w, cbias, lg, lb, off, *, name):
    Bn, S, _ = proj3.shape

    def body(cin_ref, w_ref, cb_ref, lg_ref, lb_ref, o_ref):
        _, _, _, xh, _ = _cconv_pre(cin_ref, w_ref, cb_ref)
        ln = xh * lg_ref[...] + lb_ref[...]
        o_ref[...] = (ln * jax.nn.sigmoid(ln)).astype(BF16)

    return pl.pallas_call(
        body, name=name, grid=(Bn,),
        in_specs=[_seq_spec(S, 2 * C_WIDTH, off['c']), _full((CONF_CONV, C_WIDTH)), _full((1, C_WIDTH)),
                  _full((1, C_WIDTH)), _full((1, C_WIDTH))],
        out_specs=_seq_spec(S, C_WIDTH, 0), out_shape=jax.ShapeDtypeStruct((Bn, S, C_WIDTH), BF16),
        compiler_params=_params("parallel"),
    )(proj3, w, cbias, lg, lb)


def _cconv_bwd(proj3, w, cbias, lg, lb, dyc3, off, *, name):
    Bn, S, _ = proj3.shape

    def body(cin_ref, w_ref, cb_ref, lg_ref, lb_ref, do_ref, din_ref, dw_ref, dcb_ref, dlg_ref, dlb_ref):
        @pl.when(pl.program_id(0) == 0)
        def _():
            for r in (dw_ref, dcb_ref, dlg_ref, dlb_ref):
                r[...] = jnp.zeros_like(r)

        a, sg, glu, xh, rs = _cconv_pre(cin_ref, w_ref, cb_ref)
        ln = xh * lg_ref[...] + lb_ref[...]
        sl = jax.nn.sigmoid(ln)
        dln = do_ref[...].astype(F32) * (sl * (1.0 + ln * (1.0 - sl)))
        dlg_ref[...] += _colsum(dln * xh)
        dlb_ref[...] += _colsum(dln)
        dxh = dln * lg_ref[...]
        dy0 = rs * (dxh - jnp.mean(dxh, axis=-1, keepdims=True) - xh * jnp.mean(dxh * xh, axis=-1, keepdims=True))
        dcb_ref[...] += _colsum(dy0)
        dglu = _conv_bwd(dy0, glu, w_ref, dw_ref, CONF_CONV)
        din_ref[:, 0:C_WIDTH] = (dglu * sg).astype(BF16)
        din_ref[:, C_WIDTH:2 * C_WIDTH] = (dglu * a * sg * (1.0 - sg)).astype(BF16)

    vec = _full((1, C_WIDTH))
    return pl.pallas_call(
        body, name=name, grid=(Bn,),
        in_specs=[_seq_spec(S, 2 * C_WIDTH, off['c']), _full((CONF_CONV, C_WIDTH)), vec, vec, vec, _seq_spec(S, C_WIDTH, 0)],
        out_specs=[_seq_spec(S, 2 * C_WIDTH, 0), _full((CONF_CONV, C_WIDTH)), vec, vec, vec],
        out_shape=[jax.ShapeDtypeStruct((Bn, S, 2 * C_WIDTH), BF16), jax.ShapeDtypeStruct((CONF_CONV, C_WIDTH), F32)]
        + [jax.ShapeDtypeStruct((1, C_WIDTH), F32)] * 3,
        compiler_params=_params("arbitrary"),
    )(proj3, w, cbias, lg, lb, dyc3)


def _swa_band(x_ref, g, nb):
    xb = x_ref[:, HEAD_DIM * g:HEAD_DIM * (g + 1)].reshape(nb, Q_BLOCK, HEAD_DIM)
    prev = jnp.concatenate([jnp.zeros((1, Q_BLOCK, HEAD_DIM), xb.dtype), xb[:-1]], axis=0)
    return jnp.concatenate([prev, xb], axis=1)


def _swa_probs(q_ref, kband, bias_ref, sk_ref, h, nb):
    qh = (q_ref[:, HEAD_DIM * h:HEAD_DIM * (h + 1)] * SCALE).reshape(nb, Q_BLOCK, HEAD_DIM)
    s = jnp.einsum('nqd,nsd->nqs', qh, kband, preferred_element_type=F32) + bias_ref[h][None]
    shape = (nb, Q_BLOCK, 2 * Q_BLOCK)
    n = lax.broadcasted_iota(jnp.int32, shape, 0)
    dist = lax.broadcasted_iota(jnp.int32, shape, 1) + Q_BLOCK - lax.broadcasted_iota(jnp.int32, shape, 2)
    col = lax.broadcasted_iota(jnp.int32, shape, 2)
    valid = (dist >= 0) & (dist < WINDOW) & ((n > 0) | (col >= Q_BLOCK))
    s = jnp.where(valid, s, NEG_INF)
    sink = sk_ref[h:h + 1, 0:1].reshape(1, 1, 1)
    m = jnp.maximum(jnp.max(s, axis=-1, keepdims=True), sink)
    e = jnp.exp(s - m)
    es = jnp.exp(sink - m)
    inv = 1.0 / (jnp.sum(e, axis=-1, keepdims=True) + es)
    return qh, e * inv, es * inv


def _swa_fwd(proj3, band_bias, sinks, off, *, name):
    Bn, S, _ = proj3.shape
    nb = S // Q_BLOCK

    def body(q_ref, k_ref, v_ref, bias_ref, sk_ref, o_ref):
        for g in range(D_KV_HEADS):
            kband, vband = _swa_band(k_ref, g, nb), _swa_band(v_ref, g, nb)
            for h in range(g * D_GROUP, (g + 1) * D_GROUP):
                _, p, _ = _swa_probs(q_ref, kband, bias_ref, sk_ref, h, nb)
                out = jnp.einsum('nqs,nsd->nqd', p.astype(BF16), vband, preferred_element_type=F32)
                o_ref[:, HEAD_DIM * h:HEAD_DIM * (h + 1)] = out.reshape(S, HEAD_DIM).astype(BF16)

    kcol = off['d'] + D_WIDTH
    return pl.pallas_call(
        body, name=name, grid=(Bn,),
        in_specs=[_seq_spec(S, D_WIDTH, off['d']), _seq_spec(S, D_KV_WIDTH, kcol), _seq_spec(S, D_KV_WIDTH, kcol + D_KV_WIDTH),
                  _full((D_Q_HEADS, Q_BLOCK, 2 * Q_BLOCK)), _full((D_Q_HEADS, 128))],
        out_specs=_seq_spec(S, D_WIDTH, 0), out_shape=jax.ShapeDtypeStruct((Bn, S, D_WIDTH), BF16),
        compiler_params=_params("parallel"),
    )(proj3, proj3, proj3, band_bias, sinks)


def _swa_bwd(proj3, band_bias, sinks, dyd3, off, *, name):
    Bn, S, _ = proj3.shape
    nb = S // Q_BLOCK
    scale = HEAD_DIM ** -0.5

    def body(q_ref, k_ref, v_ref, bias_ref, sk_ref, do_ref, dqkv_ref, dband_ref, dsk_ref):
        @pl.when(pl.program_id(0) == 0)
        def _():
            dband_ref[...] = jnp.zeros_like(dband_ref)
            dsk_ref[...] = jnp.zeros_like(dsk_ref)

        def unband(acc):
            prev, cur = acc[:, 0:Q_BLOCK, :], acc[:, Q_BLOCK:2 * Q_BLOCK, :]
            nxt = jnp.concatenate([prev[1:], jnp.zeros((1, Q_BLOCK, HEAD_DIM), F32)], axis=0)
            return (cur + nxt).reshape(S, HEAD_DIM).astype(BF16)

        for g in range(D_KV_HEADS):
            kband, vband = _swa_band(k_ref, g, nb), _swa_band(v_ref, g, nb)
            dkb = jnp.zeros((nb, 2 * Q_BLOCK, HEAD_DIM), F32)
            dvb = jnp.zeros((nb, 2 * Q_BLOCK, HEAD_DIM), F32)
            for h in range(g * D_GROUP, (g + 1) * D_GROUP):
                hs = slice(HEAD_DIM * h, HEAD_DIM * (h + 1))
                qh, p, ps = _swa_probs(q_ref, kband, bias_ref, sk_ref, h, nb)
                doh = do_ref[:, hs].reshape(nb, Q_BLOCK, HEAD_DIM)
                dp = jnp.einsum('nqd,nsd->nqs', doh, vband, preferred_element_type=F32)
                delta = jnp.sum(p * dp, axis=-1, keepdims=True)
                ds = p * (dp - delta)
                dsink = jnp.sum(jnp.sum(-ps * delta, axis=0), axis=0, keepdims=True)
                dsk_ref[h:h + 1, :] += jnp.broadcast_to(dsink, (1, 128))
                dband_ref[h] += jnp.sum(ds, axis=0)
                dsb = ds.astype(BF16)
                dq = jnp.einsum('nqs,nsd->nqd', dsb, kband, preferred_element_type=F32) * scale
                dqkv_ref[:, hs] = dq.reshape(S, HEAD_DIM).astype(BF16)
                dkb = dkb + jnp.einsum('nqs,nqd->nsd', dsb, qh, preferred_element_type=F32)
                dvb = dvb + jnp.einsum('nqs,nqd->nsd', p.astype(BF16), doh, preferred_element_type=F32)
            dqkv_ref[:, D_WIDTH + HEAD_DIM * g:D_WIDTH + HEAD_DIM * (g + 1)] = unband(dkb)
            dqkv_ref[:, D_WIDTH + D_KV_WIDTH + HEAD_DIM * g:D_WIDTH + D_KV_WIDTH + HEAD_DIM * (g + 1)] = unband(dvb)

    kcol = off['d'] + D_WIDTH
    wq = D_WIDTH + 2 * D_KV_WIDTH
    return pl.pallas_call(
        body, name=name, grid=(Bn,),
        in_specs=[_seq_spec(S, D_WIDTH, off['d']), _seq_spec(S, D_KV_WIDTH, kcol), _seq_spec(S, D_KV_WIDTH, kcol + D_KV_WIDTH),
                  _full((D_Q_HEADS, Q_BLOCK, 2 * Q_BLOCK)), _full((D_Q_HEADS, 128)), _seq_spec(S, D_WIDTH, 0)],
        out_specs=[_seq_spec(S, wq, 0), _full((D_Q_HEADS, Q_BLOCK, 2 * Q_BLOCK)), _full((D_Q_HEADS, 128))],
        out_shape=[jax.ShapeDtypeStruct((Bn, S, wq), BF16), jax.ShapeDtypeStruct((D_Q_HEADS, Q_BLOCK, 2 * Q_BLOCK), F32),
                   jax.ShapeDtypeStruct((D_Q_HEADS, 128), F32)],
        compiler_params=_params("arbitrary"),
    )(proj3, proj3, proj3, band_bias, sinks, dyd3)


ASSEMBLE_BLOCK = 1024


def _assemble(dproj, pieces, col0, *, name):
    T = dproj.shape[0]
    widths = [q.shape[1] for q in pieces]
    starts = [sum(widths[:k]) for k in range(len(widths))]
    bw = ASSEMBLE_BLOCK
    assert sum(widths) % bw == 0 and col0 % bw == 0 and col0 + sum(widths) == dproj.shape[1]
    nblk = sum(widths) // bw
    tm = _tile(T, 512, 256, 128)

    def body(_, *refs):
        out_ref = refs[-1]
        for jj in range(nblk):
            @pl.when(pl.program_id(1) == jj)
            def _():
                lo, hi = jj * bw, (jj + 1) * bw
                for r, s0, w in zip(refs[:-1], starts, widths):
                    a, b = max(lo, s0), min(hi, s0 + w)
                    if a < b:
                        out_ref[:, a - lo:b - lo] = r[:, a - s0:b - s0]

    return pl.pallas_call(
        body, name=name, grid=(T // tm, nblk),
        in_specs=[ANY] + [pl.BlockSpec((tm, w), lambda i, j: (i, 0)) for w in widths],
        out_specs=pl.BlockSpec((tm, bw), lambda i, j: (i, col0 // bw + j)),
        out_shape=jax.ShapeDtypeStruct(dproj.shape, BF16), input_output_aliases={0: 0},
        compiler_params=_params("parallel", "arbitrary"),
    )(dproj, *pieces)


def _relbias_grad(dband, onehot, *, name):
    L, H, n = dband.shape
    R = onehot.shape[0]

    def body(d_ref, oh_ref, out_ref):
        tot = d_ref[0]
        for l in range(1, L):
            tot = tot + d_ref[l]
        out_ref[...] = lax.dot_general(oh_ref[...], tot, (((1,), (1,)), ((), ())), preferred_element_type=F32,
                                       precision=lax.Precision.HIGHEST)

    return pl.pallas_call(
        body, name=name, out_shape=jax.ShapeDtypeStruct((R, H), F32),
        compiler_params=pltpu.CompilerParams(vmem_limit_bytes=VMEM_LIMIT_V7X),
    )(dband, onehot)


def _all_reduce(buf, *, name):
    R, C = buf.shape

    def body(x_ref, o_ref, land, send, recv):
        me = _my_id()
        land[pl.ds(me, 1)] = x_ref[...][None]
        sends = []
        for k in range(1, N_DEV):
            to, _ = _peer(k)
            cp = pltpu.make_async_remote_copy(src_ref=x_ref, dst_ref=land.at[me], send_sem=send.at[k - 1],
                                              recv_sem=recv.at[k - 1], device_id=to, device_id_type=MESH)
            cp.start()
            sends.append(cp)
        for k in range(1, N_DEV):
            frm, frm_id = _peer(k)
            pltpu.make_async_remote_copy(src_ref=x_ref, dst_ref=land.at[frm_id], send_sem=send.at[k - 1],
                                         recv_sem=recv.at[k - 1], device_id=frm, device_id_type=MESH).wait_recv()
        for cp in sends:
            cp.wait_send()
        acc = land[0]
        for d in range(1, N_DEV):
            acc = acc + land[d]
        o_ref[...] = acc

    vmem = pl.BlockSpec(memory_space=pltpu.VMEM)
    return pl.pallas_call(
        body, name=name, in_specs=[vmem], out_specs=vmem, out_shape=jax.ShapeDtypeStruct((R, C), F32),
        scratch_shapes=[pltpu.VMEM((N_DEV, R, C), F32), pltpu.SemaphoreType.DMA((N_DEV - 1,)),
                        pltpu.SemaphoreType.DMA((N_DEV - 1,))],
        compiler_params=pltpu.CompilerParams(vmem_limit_bytes=VMEM_LIMIT_V7X),
    )(buf)


def _row_tile(rows, row_bytes, align):
    fits = [t for t in range(align, rows + 1, align) if rows % t == 0]
    small = [t for t in fits if t * row_bytes <= 2**20]
    return max(small) if small else (min(fits) if fits else rows)


def _sum8(recvs, *, name):
    L = len(recvs)
    _, rows, C = recvs[0].shape
    tr = _row_tile(rows, C * 4 * L, 16)

    def body(*refs):
        o_ref = refs[-1]
        for l, r_ref in enumerate(refs[:-1]):
            acc = r_ref[0].astype(F32)
            for d in range(1, N_DEV):
                acc = acc + r_ref[d].astype(F32)
            o_ref[l] = acc

    return pl.pallas_call(
        body, name=name, grid=(rows // tr,),
        in_specs=[pl.BlockSpec((N_DEV, tr, C), lambda i: (0, i, 0))] * L,
        out_specs=pl.BlockSpec((L, tr, C), lambda i: (0, i, 0)),
        out_shape=jax.ShapeDtypeStruct((L, rows, C), F32), compiler_params=_params("parallel"),
    )(*recvs)


def _band_bias(rel_bias, onehot, *, name):
    R, H = rel_bias.shape

    def body(rb_ref, oh_ref, out_ref):
        out_ref[...] = lax.dot_general(rb_ref[...], oh_ref[...], (((0,), (0,)), ((), ())), preferred_element_type=F32,
                                       precision=lax.Precision.HIGHEST)

    return pl.pallas_call(
        body, name=name, out_shape=jax.ShapeDtypeStruct((H, onehot.shape[1]), F32),
        compiler_params=pltpu.CompilerParams(vmem_limit_bytes=VMEM_LIMIT_V7X),
    )(rel_bias, onehot)


def _adamw(g, w, m, v, *, name):
    rows, C = g.shape
    tr = _row_tile(rows, C * 4, 8)

    def body(g_ref, w_ref, m_ref, v_ref, d_ref, mo_ref, vo_ref):
        gt = g_ref[...]
        mn = ADAM_B1 * m_ref[...] + (1.0 - ADAM_B1) * gt
        vn = ADAM_B2 * v_ref[...] + (1.0 - ADAM_B2) * jnp.square(gt)
        m_hat = mn / (1.0 - ADAM_B1 ** ADAM_STEP)
        v_hat = vn / (1.0 - ADAM_B2 ** ADAM_STEP)
        d_ref[...] = -ADAM_LR * (m_hat / (jnp.sqrt(v_hat) + ADAM_EPS) + ADAM_WD * w_ref[...])
        mo_ref[...] = mn
        vo_ref[...] = vn

    spec = pl.BlockSpec((tr, C), lambda i: (i, 0))
    return pl.pallas_call(
        body, name=name, grid=(rows // tr,), in_specs=[spec] * 4, out_specs=[spec] * 3,
        out_shape=[jax.ShapeDtypeStruct((rows, C), F32)] * 3, compiler_params=_params("parallel"),
    )(g, w, m, v)


def _in_splits():
    a_f = 3 * A_WIDTH
    b = a_f + A_HEADS
    c = b + 3 * B_WIDTH
    d = c + 2 * C_WIDTH
    gates = d + D_WIDTH + 2 * D_KV_WIDTH
    return a_f, b, c, d, gates


def _permute_in(w):
    a_f, b, c, d, gates = _in_splits()
    pad = jnp.zeros(w.shape[:-1] + (AF_PAD - A_HEADS,), w.dtype)
    return jnp.concatenate([w[..., gates:], w[..., :a_f], w[..., b:c], w[..., c:d], w[..., d:gates], w[..., a_f:b], pad], axis=-1)


def _unpermute_in(g, D):
    off = _layout(D)
    return jnp.concatenate([g[..., off['a']:off['b']], g[..., off['af']:off['af'] + A_HEADS], g[..., off['b']:off['c']],
                            g[..., off['c']:off['d']], g[..., off['d']:off['af']], g[..., :off['a']]], axis=-1)


def _bucket_onehot():
    dist = np.maximum(np.arange(Q_BLOCK)[:, None] + Q_BLOCK - np.arange(2 * Q_BLOCK)[None, :], 0)
    max_exact = REL_BUCKETS // 2
    large = max_exact + (np.log(np.maximum(dist, 1).astype(np.float32) / np.float32(max_exact))
                         / np.float32(math.log(REL_MAX_DIST / max_exact)) * np.float32(REL_BUCKETS - max_exact)).astype(np.int32)
    bucket = np.where(dist < max_exact, dist, np.minimum(large, REL_BUCKETS - 1))
    return (bucket.reshape(1, -1) == np.arange(REL_BUCKETS)[:, None]).astype(np.float32)


GATHER_SEQ = [('ffn1_up', ['ffn1_w_gu']), ('ffn1_down', ['ffn1_w_down']), ('proj', ['w_in']),
              ('fox', ['w_br_a', 'w_br_b', 'w_br_c', 'w_br_d', 'w_o', 'w_ple_gate', 'w_ple']),
              ('ffn2_up', ['ffn2_w_gu']), ('ffn2_down', ['ffn2_w_down'])]
GATHER_AHEAD = 2
SCATTER_HOSTS = {'ffn2_down_bwd': ['w_ple', 'w_ple_gate'], 'ffn2_up_bwd': ['ffn2_w_gu'],
                 'fox_bwd': ['ffn2_w_down', 'w_o', 'w_br_a', 'w_br_b', 'w_br_c', 'w_br_d'], 'proj_bwd': ['w_in'],
                 'ffn1_down_bwd': [], 'ffn1_up_bwd': ['ffn1_w_gu']}


def _pack(parts):
    flat = jnp.concatenate([q.reshape(-1).astype(F32) for q in parts])
    return jnp.pad(flat, (0, (-flat.shape[0]) % 1024)).reshape(-1, 128)


def _unpack(buf, shapes):
    flat, out, pos = buf.reshape(-1), [], 0
    for s in shapes:
        n = math.prod(s)
        out.append(flat[pos:pos + n].reshape(s))
        pos += n
    return out


def kernel(x, p, ffn1_norm_pre, ffn1_w_gu, ffn1_w_down, ffn1_norm_post, mix_norm_pre, w_in, b_forget, b_gate, conv_short, conv_dw, conv_dw_bias, conv_ln_gain, conv_ln_bias, attn_sinks, rel_bias, w_br_a, w_br_b, w_br_c, w_br_d, w_o, mix_norm_post, ffn2_norm_pre, ffn2_w_gu, ffn2_w_down, ffn2_norm_post, ple_norm_gate, w_ple_gate, w_ple, ple_norm_post, loss_target, m_ffn1_norm_pre, m_ffn1_w_gu, m_ffn1_w_down, m_ffn1_norm_post, m_mix_norm_pre, m_w_in, m_b_forget, m_b_gate, m_conv_short, m_conv_dw, m_conv_dw_bias, m_conv_ln_gain, m_conv_ln_bias, m_attn_sinks, m_rel_bias, m_w_br_a, m_w_br_b, m_w_br_c, m_w_br_d, m_w_o, m_mix_norm_post, m_ffn2_norm_pre, m_ffn2_w_gu, m_ffn2_w_down, m_ffn2_norm_post, m_ple_norm_gate, m_w_ple_gate, m_w_ple, m_ple_norm_post, v_ffn1_norm_pre, v_ffn1_w_gu, v_ffn1_w_down, v_ffn1_norm_post, v_mix_norm_pre, v_w_in, v_b_forget, v_b_gate, v_conv_short, v_conv_dw, v_conv_dw_bias, v_conv_ln_gain, v_conv_ln_bias, v_attn_sinks, v_rel_bias, v_w_br_a, v_w_br_b, v_w_br_c, v_w_br_d, v_w_o, v_mix_norm_post, v_ffn2_norm_pre, v_ffn2_w_gu, v_ffn2_w_down, v_ffn2_norm_post, v_ple_norm_gate, v_w_ple_gate, v_w_ple, v_ple_norm_post):
    a = dict(locals())
    Bn, S, D = x.shape
    T = Bn * S
    L, E = p.shape[0], p.shape[-1]
    F = ffn1_w_down.shape[1] * N_DEV
    off = _layout(D)
    PW = off['end']
    me = _my_id()

    shard = {n: a[n].astype(BF16) for n in BIG}
    shard['ffn1_w_gu'] = jnp.swapaxes(ffn1_w_gu, 1, 2).astype(BF16)
    shard['ffn2_w_gu'] = jnp.swapaxes(ffn2_w_gu, 1, 2).astype(BF16)
    shard['w_in'] = _permute_in(w_in).astype(BF16)
    is_col = lambda names: [n in COL_SHARDED for n in names]
    head = [n for _, names in GATHER_SEQ[:GATHER_AHEAD] for n in names]
    first = _Exchange("gather", [shard[n][0] for n in head], is_col(head))
    full = [dict(zip(head, first.run_alone("gather_head")))] + [{} for _ in range(1, L)]
    hosts = [h_ for h_, _ in GATHER_SEQ]

    def with_gather(fn, host, i, *args, **kw):
        li, lj = divmod(i * len(hosts) + hosts.index(host) + GATHER_AHEAD, len(hosts))
        names = GATHER_SEQ[lj][1]
        rider = _Exchange("gather", [shard[n][li] for n in names], is_col(names)) if li < L else None
        out = fn(*args, rider=rider, **kw)
        if rider is not None:
            full[li].update(zip(names, rider.result))
        return out

    cw = conv_short.shape[2]
    conv_full = [lax.dynamic_update_slice(jnp.zeros(c.shape[:2] + (cw * N_DEV,), F32), c, (0, 0, me * cw))
                 for c in (conv_short, conv_dw)]
    conv_shapes = [c.shape for c in conv_full]
    cs_all, cdw_all = _unpack(_all_reduce(_pack(conv_full), name="gather_conv"), conv_shapes)

    onehot = jnp.asarray(_bucket_onehot())
    band_bias = _band_bias(rel_bias, onehot, name="band_bias").reshape(D_Q_HEADS, Q_BLOCK, 2 * Q_BLOCK)

    def vec(name, i):
        return a[name][i][None]

    def lay(i):
        return dict(
            bfor=jnp.pad(b_forget[i], (0, AF_LANES - A_HEADS))[None], bgate=b_gate[i].reshape(N_BRANCH, D),
            cs=cs_all[i], cdw=cdw_all[i], cb=conv_dw_bias[i][None], lg=conv_ln_gain[i][None], lb=conv_ln_bias[i][None],
            sinks=jnp.broadcast_to(attn_sinks[i][:, None], (D_Q_HEADS, 128)),
            wbrs=[full[i][n] for n in ('w_br_a', 'w_br_b', 'w_br_c', 'w_br_d')], pe=p[i].reshape(T, E))

    def wgu(k, i):
        return full[i][f'ffn{k}_w_gu'].reshape(2, F, D)

    h = x.reshape(T, D)
    saved = []
    for i in range(L):
        s = dict(h0=h)
        s['gu1'], s['n1'] = with_gather(_rms_mm, 'ffn1_up', i, h, vec('ffn1_norm_pre', i), wgu(1, i), nt=True,
                                        out_dtype=BF16, save_n=True, name=f"ffn1_up_{i}")
        s['h1'], s['f1'] = with_gather(_ffn_down, 'ffn1_down', i, s['gu1'], full[i]['ffn1_w_down'], h,
                                       vec('ffn1_norm_post', i), name=f"ffn1_down_{i}")
        win = full[i]['w_in']
        proj, s['u'], af = with_gather(_rms_mm, 'proj', i, s['h1'], vec('mix_norm_pre', i), win[None], nt=False,
                                       out_dtype=BF16, save_n=True, f32_cols=(off['af'], AF_LANES), name=f"proj_{i}")
        s['proj'] = proj.reshape(T, PW)
        s['proj3'], s['af3'] = proj.reshape(Bn, S, PW), af.reshape(Bn, S, AF_LANES)
        q = lay(i)
        ya = with_gather(_fox_fwd, 'fox', i, s['proj3'], s['af3'], q['bfor'], off, name=f"fox_{i}")
        yb = _sconv_fwd(s['proj3'], q['cs'], off, name=f"sconv_{i}")
        yc = _cconv_fwd(s['proj3'], q['cdw'], q['cb'], q['lg'], q['lb'], off, name=f"cconv_{i}")
        yd = _swa_fwd(s['proj3'], band_bias, q['sinks'], off, name=f"swa_{i}")
        s['ys'] = [y.reshape(T, y.shape[-1]) for y in (ya, yb, yc, yd)]
        s['h2'], s['o'], s['merged'] = _merge_out(s['ys'], s['proj'], q['bgate'], q['wbrs'], full[i]['w_o'], s['h1'],
                                                  vec('mix_norm_post', i), name=f"merge_{i}")
        s['gu2'], s['n2'] = with_gather(_rms_mm, 'ffn2_up', i, s['h2'], vec('ffn2_norm_pre', i), wgu(2, i), nt=True,
                                        out_dtype=BF16, save_n=True, name=f"ffn2_up_{i}")
        s['h3'], s['f2'] = with_gather(_ffn_down, 'ffn2_down', i, s['gu2'], full[i]['ffn2_w_down'], s['h2'],
                                       vec('ffn2_norm_post', i), name=f"ffn2_down_{i}")
        h = _ple(s['h3'], q['pe'], vec('ple_norm_gate', i), full[i]['w_ple_gate'], full[i]['w_ple'],
                 vec('ple_norm_post', i), name=f"ple_{i}")
        saved.append(s)

    lpart, dh = _loss_head(h, loss_target.reshape(T, D), name="loss_head")
    loss = lax.psum(lpart[0, 0], MESH_AXES)

    gbuf = [{} for _ in range(L)]
    recv = [{} for _ in range(L)]
    sg = {n: [None] * L for n in WEIGHTS if n not in BIG and n != 'rel_bias'}
    dbands = [None] * L

    def wgrad(n, i, a_, b_, **kw):
        late = n == 'ffn1_w_gu' and i == 0
        rider = _Exchange("scatter", [gbuf[0]['ffn1_w_down']], [False]) if late else None
        gbuf[i][n] = _mm_tn(a_, b_, name=f"d_{n}_{i}", rider=rider, **kw)
        if late:
            recv[0]['ffn1_w_down'] = rider.result[0]

    def with_scatter(fn, host, i, *args, **kw):
        items = [(i, n) for n in SCATTER_HOSTS[host]]
        if host == 'ffn2_down_bwd' and i + 1 < L:
            items.append((i + 1, 'ffn1_w_down'))
        rider = _Exchange("scatter", [gbuf[l][n] for l, n in items], is_col([n for _, n in items])) if items else None
        out = fn(*args, rider=rider, **kw)
        if rider is not None:
            for (l, n), r in zip(items, rider.result):
                recv[l][n] = r
        return out

    def ffn_bwd(k, i, dh_out, s, h_in):
        dgu, df, sg[f'ffn{k}_norm_post'][i] = with_scatter(
            _ffn_down_bwd, f'ffn{k}_down_bwd', i, dh_out, s[f'f{k}'], vec(f'ffn{k}_norm_post', i), full[i][f'ffn{k}_w_down'],
            s[f'gu{k}'], name=f"ffn{k}_down_bwd_{i}")
        wgrad(f'ffn{k}_w_down', i, s[f'gu{k}'], df, swiglu=True)
        wgrad(f'ffn{k}_w_gu', i, dgu, s[f'n{k}'])
        dh_in, sg[f'ffn{k}_norm_pre'][i] = with_scatter(
            _mm_rmsbwd, f'ffn{k}_up_bwd', i, dgu, wgu(k, i), dh_out, h_in, vec(f'ffn{k}_norm_pre', i), nt=False,
            name=f"ffn{k}_up_bwd_{i}")
        return dh_in

    for i in reversed(range(L)):
        s, q = saved[i], lay(i)
        dh, de, dpgl, npg, sg['ple_norm_gate'][i], sg['ple_norm_post'][i] = _ple_bwd(
            dh, s['h3'], q['pe'], vec('ple_norm_gate', i), full[i]['w_ple_gate'], full[i]['w_ple'], vec('ple_norm_post', i),
            name=f"ple_bwd_{i}")
        wgrad('w_ple', i, q['pe'][None], de)
        wgrad('w_ple_gate', i, npg[None], dpgl)
        dh = ffn_bwd(2, i, dh, s, s['h2'])
        do, dz, *dys, dgates, sg['mix_norm_post'][i], dbg = _merge_out_bwd(
            dh, s['o'], vec('mix_norm_post', i), full[i]['w_o'], s['ys'], s['proj'], q['bgate'], q['wbrs'],
            name=f"merge_bwd_{i}")
        sg['b_gate'][i] = dbg.reshape(-1)
        wgrad('w_o', i, s['merged'][None], do)
        for b, n in enumerate(('w_br_a', 'w_br_b', 'w_br_c', 'w_br_d')):
            wgrad(n, i, s['ys'][b][None], dz, b_plane=b)
        dy3 = [d.reshape(Bn, S, d.shape[-1]) for d in dys]
        da, daf, dbf = with_scatter(_fox_bwd, 'fox_bwd', i, s['proj3'], s['af3'], q['bfor'], dy3[0], off,
                                    name=f"fox_bwd_{i}")
        sg['b_forget'][i] = dbf[0, :A_HEADS]
        db, sg['conv_short'][i] = _sconv_bwd(s['proj3'], q['cs'], dy3[1], off, name=f"sconv_bwd_{i}")
        dc, sg['conv_dw'][i], sg['conv_dw_bias'][i], sg['conv_ln_gain'][i], sg['conv_ln_bias'][i] = _cconv_bwd(
            s['proj3'], q['cdw'], q['cb'], q['lg'], q['lb'], dy3[2], off, name=f"cconv_bwd_{i}")
        dd, dbands[i], dsk = _swa_bwd(s['proj3'], band_bias, q['sinks'], dy3[3], off, name=f"swa_bwd_{i}")
        sg['attn_sinks'][i] = dsk[:, 0]
        dproj = _assemble(dgates, [t.reshape(T, t.shape[-1]) for t in (da, db, dc, dd, daf)], off['a'], name=f"dproj_{i}")
        wgrad('w_in', i, s['u'][None], dproj)
        dh, sg['mix_norm_pre'][i] = with_scatter(
            _mm_rmsbwd, 'proj_bwd', i, dproj[None], full[i]['w_in'][None], dh, s['h1'], vec('mix_norm_pre', i), nt=True,
            name=f"proj_bwd_{i}")
        dh = ffn_bwd(1, i, dh, s, s['h0'])
    grad_x = dh.reshape(Bn, S, D)

    d_rel = _relbias_grad(jnp.stack(dbands).reshape(L, D_Q_HEADS, -1), onehot, name="relbias_grad")
    small_g = [d_rel if n == 'rel_bias' else jnp.stack(sg[n]).reshape(a[n].shape) for n in SMALL]
    n_small_rows = _pack(small_g).shape[0]
    conv_g = [jnp.stack(sg[n]) for n in CONV_SHARDED]
    red = _all_reduce(jnp.concatenate([_pack(small_g), _pack(conv_g)]), name="reduce_small")
    g_small_buf = red[:n_small_rows]
    conv_gfull = _unpack(red[n_small_rows:], conv_shapes)
    conv_gloc = [lax.dynamic_slice_in_dim(g, me * cw, cw, axis=2) for g in conv_gfull]

    grads, deltas, new_m, new_v = {}, {}, {}, {}
    small_shapes = [a[n].shape for n in SMALL]
    res = _adamw(g_small_buf, *[_pack([a[pre + n] for n in SMALL]) for pre in ('', 'm_', 'v_')], name="adamw_small")
    for dst, buf in zip((grads, deltas, new_m, new_v), (g_small_buf,) + tuple(res)):
        dst.update(zip(SMALL, _unpack(buf, small_shapes)))
    loc_shapes = [a[n].shape for n in CONV_SHARDED]
    g_conv_buf = _pack(conv_gloc)
    res = _adamw(g_conv_buf, *[_pack([a[pre + n] for n in CONV_SHARDED]) for pre in ('', 'm_', 'v_')], name="adamw_conv")
    for dst, buf in zip((grads, deltas, new_m, new_v), (g_conv_buf,) + tuple(res)):
        dst.update(zip(CONV_SHARDED, _unpack(buf, loc_shapes)))

    for n in BIG:
        g = _sum8([recv[l][n] for l in range(L)], name=f"sum_{n}")
        if n in ('ffn1_w_gu', 'ffn2_w_gu'):
            g = jnp.swapaxes(g, 1, 2)
        elif n == 'w_in':
            g = _unpermute_in(g, D)
        C = g.shape[-1]
        res = _adamw(g.reshape(-1, C), *[a[pre + n].reshape(-1, C) for pre in ('', 'm_', 'v_')], name=f"adamw_{n}")
        grads[n] = g
        deltas[n], new_m[n], new_v[n] = [t.reshape(g.shape) for t in res]

    return (loss, grad_x, *[grads[n] for n in WEIGHTS], *[deltas[n] for n in WEIGHTS],
            *[new_m[n] for n in WEIGHTS], *[new_v[n] for n in WEIGHTS])
```

```python
import functools
import math

import jax
import jax.numpy as jnp
import numpy as np
from jax import lax
from jax.experimental import pallas as pl
from jax.experimental.pallas import tpu as pltpu

F32 = jnp.float32
BF16 = jnp.bfloat16

EPS = 1e-6
NEG_INF = -1e30
HEAD_DIM = 64
SCALE = HEAD_DIM ** -0.5
A_HEADS = 4
A_WIDTH = A_HEADS * HEAD_DIM
B_WIDTH = 256
C_WIDTH = 256
SHORT_CONV = 3
CONF_CONV = 31
D_Q_HEADS = 8
D_KV_HEADS = 2
D_GROUP = D_Q_HEADS // D_KV_HEADS
D_WIDTH = D_Q_HEADS * HEAD_DIM
D_KV_WIDTH = D_KV_HEADS * HEAD_DIM
WINDOW = 128
Q_BLOCK = 128
N_BRANCH = 4
REL_BUCKETS = 32
REL_MAX_DIST = 128
AF_PAD = 256
AF_LANES = 128

ADAM_LR = 0.001
ADAM_B1 = 0.9
ADAM_B2 = 0.999
ADAM_EPS = 1e-08
ADAM_WD = 0.01
ADAM_STEP = 10

N_DEV = 8
MESH_AXES = ("x", "y", "c")
VMEM_LIMIT_V7X = 56 * 2**20
MESH = pl.DeviceIdType.MESH

WEIGHTS = ['ffn1_norm_pre', 'ffn1_w_gu', 'ffn1_w_down', 'ffn1_norm_post', 'mix_norm_pre', 'w_in', 'b_forget',
           'b_gate', 'conv_short', 'conv_dw', 'conv_dw_bias', 'conv_ln_gain', 'conv_ln_bias', 'attn_sinks',
           'rel_bias', 'w_br_a', 'w_br_b', 'w_br_c', 'w_br_d', 'w_o', 'mix_norm_post', 'ffn2_norm_pre',
           'ffn2_w_gu', 'ffn2_w_down', 'ffn2_norm_post', 'ple_norm_gate', 'w_ple_gate', 'w_ple', 'ple_norm_post']
ARG_NAMES = ['x', 'p'] + WEIGHTS + ['loss_target'] + ['m_' + n for n in WEIGHTS] + ['v_' + n for n in WEIGHTS]
BIG = ['ffn1_w_gu', 'ffn1_w_down', 'w_in', 'w_br_a', 'w_br_b', 'w_br_c', 'w_br_d', 'w_o', 'ffn2_w_gu',
       'ffn2_w_down', 'w_ple_gate', 'w_ple']
COL_SHARDED = ('w_br_a', 'w_br_b', 'w_br_c', 'w_br_d', 'w_ple')
CONV_SHARDED = ('conv_short', 'conv_dw')
SMALL = [n for n in WEIGHTS if n not in BIG and n not in CONV_SHARDED]


def _tile(n, *prefs):
    for t in prefs:
        if n % t == 0:
            return t
    return n


def _chunks(n, width=768):
    return [(c, min(c + width, n)) for c in range(0, n, width)]


def _params(*sem):
    return pltpu.CompilerParams(dimension_semantics=sem, vmem_limit_bytes=VMEM_LIMIT_V7X)


def _dot(a, b):
    return jnp.dot(a, b, preferred_element_type=F32)


def _dot_nt(a, b):
    return lax.dot_general(a, b, (((1,), (1,)), ((), ())), preferred_element_type=F32)


def _dot_tn(a, b):
    return lax.dot_general(a, b, (((0,), (0,)), ((), ())), preferred_element_type=F32)


def _rstd(x):
    return lax.rsqrt(jnp.mean(x * x, axis=-1, keepdims=True) + EPS)


def _rms_bwd(dy, x, r, g):
    xh = x * r
    dxh = dy * g
    dx = r * (dxh - xh * jnp.mean(dxh * xh, axis=-1, keepdims=True))
    return dx, dy * xh


def _colsum(v):
    return jnp.sum(v, axis=0, keepdims=True)


def _my_id():
    return lax.axis_index("x") * 4 + lax.axis_index("y") * 2 + lax.axis_index("c")


def _peer(k):
    coords = []
    for bit, axis in zip((4, 2, 1), MESH_AXES):
        me = lax.axis_index(axis)
        coords.append(1 - me if k & bit else me)
    return tuple(coords), coords[0] * 4 + coords[1] * 2 + coords[2]


def _window(ref, col, d, size):
    start = pl.multiple_of(d * size, 8)
    return ref.at[:, pl.ds(start, size)] if col else ref.at[pl.ds(start, size), :]


ANY = pl.BlockSpec(memory_space=pl.ANY)


class _Exchange:
    def __init__(self, kind, arrays, cols):
        self.kind, self.arrays, self.cols = kind, list(arrays), list(cols)
        n = len(self.arrays)
        if kind == "gather":
            self.sizes = [a.shape[1] if c else a.shape[0] for a, c in zip(self.arrays, cols)]
            self.out_shape = [jax.ShapeDtypeStruct((a.shape[0], a.shape[1] * N_DEV) if c else (a.shape[0] * N_DEV, a.shape[1]),
                                                   a.dtype) for a, c in zip(self.arrays, cols)]
        else:
            self.sizes = [a.shape[1] // N_DEV if c else a.shape[0] // N_DEV for a, c in zip(self.arrays, cols)]
            self.out_shape = [jax.ShapeDtypeStruct((N_DEV, a.shape[0], s) if c else (N_DEV, s, a.shape[1]), a.dtype)
                              for a, c, s in zip(self.arrays, cols, self.sizes)]
        self.scratch = [pltpu.SemaphoreType.DMA((n, N_DEV - 1)), pltpu.SemaphoreType.DMA((n, N_DEV - 1)),
                        pltpu.SemaphoreType.DMA((n,))]
        self.result = None

    def _src(self, ins, w, d):
        return ins[w] if self.kind == "gather" else _window(ins[w], self.cols[w], d, self.sizes[w])

    def _dst(self, outs, w, d):
        return _window(outs[w], self.cols[w], d, self.sizes[w]) if self.kind == "gather" else outs[w].at[d]

    def _copies(self, ins, outs, send, recv, loc):
        me = _my_id()
        n = len(self.arrays)
        two_level = self.kind == "gather"
        local = [pltpu.make_async_copy(self._src(ins, w, me), self._dst(outs, w, me), loc.at[w]) for w in range(n)]
        sends, arrivals, forwards, handed = [], [], [], []
        sibling, _ = _peer(1)
        for k in range(1, N_DEV):
            peer, peer_id = _peer(k)
            for w in range(n):
                sems = dict(send_sem=send.at[w, k - 1], recv_sem=recv.at[w, k - 1])
                if two_level and k > 1 and k % 2 == 1:
                    _, via_id = _peer(k - 1)
                    forwards.append(pltpu.make_async_remote_copy(
                        src_ref=self._dst(outs, w, via_id), dst_ref=self._dst(outs, w, via_id), device_id=sibling,
                        device_id_type=MESH, **sems))
                    handed.append(pltpu.make_async_remote_copy(
                        src_ref=self._dst(outs, w, peer_id), dst_ref=self._dst(outs, w, peer_id), device_id=sibling,
                        device_id_type=MESH, **sems))
                    continue
                at = dict(device_id=peer, device_id_type=MESH, **sems)
                sends.append(pltpu.make_async_remote_copy(src_ref=self._src(ins, w, peer_id), dst_ref=self._dst(outs, w, me), **at))
                arrivals.append(pltpu.make_async_remote_copy(src_ref=self._src(ins, w, me), dst_ref=self._dst(outs, w, peer_id), **at))
        return local, sends, arrivals, forwards, handed

    def start(self, ins, outs, send, recv, loc):
        local, sends, _, _, _ = self._copies(ins, outs, send, recv, loc)
        for cp in local + sends:
            cp.start()

    def wait(self, ins, outs, send, recv, loc):
        local, sends, arrivals, forwards, handed = self._copies(ins, outs, send, recv, loc)
        for cp in arrivals:
            cp.wait_recv()
        for cp in forwards:
            cp.start()
        for cp in handed:
            cp.wait_recv()
        for cp in sends + forwards:
            cp.wait_send()
        for cp in local:
            cp.wait()

    def run_alone(self, name):
        n = len(self.arrays)

        def body(*refs):
            self.start(refs[:n], refs[n:2 * n], *refs[2 * n:])
            self.wait(refs[:n], refs[n:2 * n], *refs[2 * n:])

        self.result = pl.pallas_call(body, name=name, in_specs=[ANY] * n, out_specs=[ANY] * n, out_shape=self.out_shape,
                                     scratch_shapes=self.scratch)(*self.arrays)
        return self.result


def _call(body, args, *, name, grid, in_specs, out_specs, out_shape, scratch_shapes=(), sem, rider=None):
    if rider is None:
        return pl.pallas_call(body, name=name, grid=grid, in_specs=in_specs, out_specs=out_specs, out_shape=out_shape,
                              scratch_shapes=list(scratch_shapes), compiler_params=_params(*sem))(*args)
    n_in, n_out, n_scr, n_r = len(in_specs), len(out_shape), len(scratch_shapes), len(rider.arrays)

    def both(*refs):
        ins, r_in = refs[:n_in], refs[n_in:n_in + n_r]
        outs, r_out = refs[n_in + n_r:n_in + n_r + n_out], refs[n_in + n_r + n_out:n_in + 2 * n_r + n_out]
        scr, sems = refs[n_in + 2 * n_r + n_out:n_in + 2 * n_r + n_out + n_scr], refs[n_in + 2 * n_r + n_out + n_scr:]
        first = functools.reduce(lambda p, q: p & q, [pl.program_id(d) == 0 for d in range(len(grid))])
        last = functools.reduce(lambda p, q: p & q, [pl.program_id(d) == g - 1 for d, g in enumerate(grid)])

        @pl.when(first)
        def _():
            rider.start(r_in, r_out, *sems)

        body(*ins, *outs, *scr)

        @pl.when(last)
        def _():
            rider.wait(r_in, r_out, *sems)

    res = pl.pallas_call(
        both, name=name, grid=grid, in_specs=list(in_specs) + [ANY] * n_r, out_specs=list(out_specs) + [ANY] * n_r,
        out_shape=list(out_shape) + rider.out_shape, scratch_shapes=list(scratch_shapes) + rider.scratch,
        compiler_params=_params(*(("arbitrary",) * len(grid))))(*args, *rider.arrays)
    rider.result = res[n_out:]
    return res[:n_out]


def _rms_mm(h, g, w, *, nt, out_dtype, save_n, name, rider=None, f32_cols=None):
    T, D = h.shape
    P = w.shape[0]
    N = w.shape[1] if nt else w.shape[2]
    tm = _tile(T, 1024, 512, 256, 128)
    tn = _tile(N, 2816, 1792, 2048, 1408, 1024, 512, 256, 128)

    if f32_cols is not None:
        c0, cw = f32_cols
        jt, lo = divmod(c0, tn)
        assert P == 1 and lo + cw <= tn

    def body(h_ref, g_ref, w_ref, y_ref, *rest):
        n_scr = rest[-1]

        @pl.when((pl.program_id(1) == 0) & (pl.program_id(2) == 0))
        def _():
            x = h_ref[...]
            n = (x * _rstd(x) * g_ref[...]).astype(BF16)
            n_scr[...] = n
            if save_n:
                rest[0][...] = n

        wt = w_ref[...]
        y = _dot_nt(n_scr[...], wt) if nt else _dot(n_scr[...], wt)
        y_ref[...] = y.astype(out_dtype)
        if f32_cols is not None:
            @pl.when(pl.program_id(2) == jt)
            def _():
                rest[-2][...] = y[:, lo:lo + cw]

    w_spec = (pl.BlockSpec((None, tn, D), lambda i, p, j: (p, j, 0)) if nt
              else pl.BlockSpec((None, D, tn), lambda i, p, j: (p, 0, j)))
    out_shape = [jax.ShapeDtypeStruct((P, T, N), out_dtype)]
    out_specs = [pl.BlockSpec((None, tm, tn), lambda i, p, j: (p, i, j))]
    if save_n:
        out_shape.append(jax.ShapeDtypeStruct((T, D), BF16))
        out_specs.append(pl.BlockSpec((tm, D), lambda i, p, j: (i, 0)))
    if f32_cols is not None:
        out_shape.append(jax.ShapeDtypeStruct((T, cw), F32))
        out_specs.append(pl.BlockSpec((tm, cw), lambda i, p, j: (i, 0)))
    res = _call(
        body, (h, g, w), name=name, grid=(T // tm, P, N // tn),
        in_specs=[pl.BlockSpec((tm, D), lambda i, p, j: (i, 0)), pl.BlockSpec((1, D), lambda i, p, j: (0, 0)), w_spec],
        out_specs=out_specs, out_shape=out_shape, scratch_shapes=[pltpu.VMEM((tm, D), BF16)],
        sem=("parallel", "arbitrary", "arbitrary"), rider=rider)
    return res if save_n else res[0]


def _ffn_down(gu, wd, h, gpost, *, name, rider=None):
    _, T, F = gu.shape
    D = wd.shape[1]
    tm = _tile(T, 512, 256, 128)
    tk = _tile(F, 2816, 1408, 1024, 512, 256, 128)
    nk = F // tk

    def body(g_ref, u_ref, wd_ref, h_ref, gp_ref, hn_ref, f_ref, acc):
        k = pl.program_id(1)

        @pl.when(k == 0)
        def _():
            acc[...] = jnp.zeros_like(acc)

        part = None
        for c0, c1 in _chunks(tk):
            gt = g_ref[:, c0:c1].astype(F32)
            a = (gt * jax.nn.sigmoid(gt) * u_ref[:, c0:c1].astype(F32)).astype(BF16)
            d = _dot(a, wd_ref[c0:c1, :])
            part = d if part is None else part + d
        acc[...] += part

        @pl.when(k == nk - 1)
        def _():
            f = acc[...]
            f_ref[...] = f
            hn_ref[...] = h_ref[...] + 0.5 * (f * _rstd(f) * gp_ref[...])

    return _call(
        body, (gu, gu, wd, h, gpost), name=name, grid=(T // tm, nk),
        in_specs=[pl.BlockSpec((None, tm, tk), lambda i, k: (0, i, k)), pl.BlockSpec((None, tm, tk), lambda i, k: (1, i, k)),
                  pl.BlockSpec((tk, D), lambda i, k: (k, 0)), pl.BlockSpec((tm, D), lambda i, k: (i, 0)),
                  pl.BlockSpec((1, D), lambda i, k: (0, 0))],
        out_specs=[pl.BlockSpec((tm, D), lambda i, k: (i, 0)), pl.BlockSpec((tm, D), lambda i, k: (i, 0))],
        out_shape=[jax.ShapeDtypeStruct((T, D), F32), jax.ShapeDtypeStruct((T, D), F32)],
        scratch_shapes=[pltpu.VMEM((tm, D), F32)], sem=("parallel", "arbitrary"), rider=rider)


def _ffn_down_bwd(dh, f, gpost, wd, gu, *, name, rider=None):
    _, T, F = gu.shape
    D = wd.shape[1]
    tm = _tile(T, 256, 128)
    tn = _tile(F, 2816, 1408, 1024, 512, 256, 128)

    def body(dh_ref, f_ref, gp_ref, wd_ref, g_ref, u_ref, dgu_ref, df_ref, dgp_ref, df_scr):
        i, j = pl.program_id(0), pl.program_id(1)

        @pl.when((i == 0) & (j == 0))
        def _():
            dgp_ref[...] = jnp.zeros_like(dgp_ref)

        @pl.when(j == 0)
        def _():
            x = f_ref[...]
            dx, dgn = _rms_bwd(0.5 * dh_ref[...], x, _rstd(x), gp_ref[...])
            dgp_ref[...] += _colsum(dgn)
            df = dx.astype(BF16)
            df_scr[...] = df
            df_ref[...] = df

        for c0, c1 in _chunks(tn):
            dact = _dot_nt(df_scr[...], wd_ref[c0:c1, :])
            gt = g_ref[:, c0:c1].astype(F32)
            ut = u_ref[:, c0:c1].astype(F32)
            sg = jax.nn.sigmoid(gt)
            dgu_ref[0, :, c0:c1] = (dact * ut * (sg * (1.0 + gt * (1.0 - sg)))).astype(BF16)
            dgu_ref[1, :, c0:c1] = (dact * (gt * sg)).astype(BF16)

    return _call(
        body, (dh, f, gpost, wd, gu, gu), name=name, grid=(T // tm, F // tn),
        in_specs=[pl.BlockSpec((tm, D), lambda i, j: (i, 0)), pl.BlockSpec((tm, D), lambda i, j: (i, 0)),
                  pl.BlockSpec((1, D), lambda i, j: (0, 0)), pl.BlockSpec((tn, D), lambda i, j: (j, 0)),
                  pl.BlockSpec((None, tm, tn), lambda i, j: (0, i, j)), pl.BlockSpec((None, tm, tn), lambda i, j: (1, i, j))],
        out_specs=[pl.BlockSpec((2, tm, tn), lambda i, j: (0, i, j)), pl.BlockSpec((tm, D), lambda i, j: (i, 0)),
                   pl.BlockSpec((1, D), lambda i, j: (0, 0))],
        out_shape=[jax.ShapeDtypeStruct((2, T, F), BF16), jax.ShapeDtypeStruct((T, D), BF16),
                   jax.ShapeDtypeStruct((1, D), F32)],
        scratch_shapes=[pltpu.VMEM((tm, D), BF16)], sem=("arbitrary", "arbitrary"), rider=rider)


def _mm_rmsbwd(a, b, dh_in, h, g, *, nt, name, rider=None):
    P, T, K = a.shape
    D = h.shape[1]
    tm = _tile(T, 1024, 512, 256, 128)
    tk = _tile(K, 1792, 1408, 2048, 1024, 512, 256, 128)
    nk = K // tk

    def body(a_ref, b_ref, dh_ref, h_ref, g_ref, out_ref, dg_ref, acc):
        i, p, k = pl.program_id(0), pl.program_id(1), pl.program_id(2)

        @pl.when((i == 0) & (p == 0) & (k == 0))
        def _():
            dg_ref[...] = jnp.zeros_like(dg_ref)

        @pl.when((p == 0) & (k == 0))
        def _():
            acc[...] = jnp.zeros_like(acc)

        acc[...] += _dot_nt(a_ref[...], b_ref[...]) if nt else _dot(a_ref[...], b_ref[...])

        @pl.when((p == P - 1) & (k == nk - 1))
        def _():
            x = h_ref[...]
            dx, dgn = _rms_bwd(acc[...], x, _rstd(x), g_ref[...])
            dg_ref[...] += _colsum(dgn)
            out_ref[...] = dh_ref[...] + dx

    b_spec = (pl.BlockSpec((None, D, tk), lambda i, p, k: (p, 0, k)) if nt
              else pl.BlockSpec((None, tk, D), lambda i, p, k: (p, k, 0)))
    return _call(
        body, (a, b, dh_in, h, g), name=name, grid=(T // tm, P, nk),
        in_specs=[pl.BlockSpec((None, tm, tk), lambda i, p, k: (p, i, k)), b_spec,
                  pl.BlockSpec((tm, D), lambda i, p, k: (i, 0)), pl.BlockSpec((tm, D), lambda i, p, k: (i, 0)),
                  pl.BlockSpec((1, D), lambda i, p, k: (0, 0))],
        out_specs=[pl.BlockSpec((tm, D), lambda i, p, k: (i, 0)), pl.BlockSpec((1, D), lambda i, p, k: (0, 0))],
        out_shape=[jax.ShapeDtypeStruct((T, D), F32), jax.ShapeDtypeStruct((1, D), F32)],
        scratch_shapes=[pltpu.VMEM((tm, D), F32)], sem=("arbitrary", "arbitrary", "arbitrary"), rider=rider)


def _mm_tn(a, b, *, swiglu=False, b_plane=0, name, rider=None):
    T, N = b.shape[-2:]
    K = a.shape[2]
    P = 1 if swiglu else a.shape[0]
    tk = _tile(K, 2816, 1024, 512, 256, 128)
    tn = _tile(N, 1792, 1024, 512, 256, 128)
    tt = _tile(T, 512, 256, 128)
    nt_ = T // tt
    nkb = K // tk

    def body(*refs):
        if swiglu:
            g_ref, u_ref, b_ref = refs[:3]
        else:
            a_ref, b_ref = refs[:2]
        out_ref, acc = refs[-2], refs[-1]
        t = pl.program_id(3)

        @pl.when(t == 0)
        def _():
            acc[...] = jnp.zeros_like(acc)

        if swiglu:
            bt = b_ref[...].astype(BF16)
            for c0, c1 in _chunks(tk):
                gt = g_ref[:, c0:c1].astype(F32)
                at = (gt * jax.nn.sigmoid(gt) * u_ref[:, c0:c1].astype(F32)).astype(BF16)
                acc[c0:c1, :] += _dot_tn(at, bt)
        else:
            acc[...] += _dot_tn(a_ref[...].astype(BF16), b_ref[...].astype(BF16))

        @pl.when(t == nt_ - 1)
        def _():
            out_ref[...] = acc[...].astype(BF16)

    if swiglu:
        a_specs = [pl.BlockSpec((None, tt, tk), lambda p, i, j, t: (0, t, i)),
                   pl.BlockSpec((None, tt, tk), lambda p, i, j, t: (1, t, i))]
        a_args = [a, a]
    else:
        a_specs = [pl.BlockSpec((None, tt, tk), lambda p, i, j, t: (p, t, i))]
        a_args = [a]
    if b.ndim == 3:
        in_specs = a_specs + [pl.BlockSpec((None, tt, tn), lambda p, i, j, t: (b_plane, t, j))]
    else:
        in_specs = a_specs + [pl.BlockSpec((tt, tn), lambda p, i, j, t: (t, j))]
    return _call(
        body, (*a_args, b), name=name, grid=(P, nkb, N // tn, nt_), in_specs=in_specs,
        out_specs=[pl.BlockSpec((tk, tn), lambda p, i, j, t: (p * nkb + i, j))],
        out_shape=[jax.ShapeDtypeStruct((P * K, N), BF16)], scratch_shapes=[pltpu.VMEM((tk, tn), F32)],
        sem=("parallel", "parallel", "parallel", "arbitrary"), rider=rider)[0]


def _row(D):
    return pl.BlockSpec((1, D), lambda i: (0, 0))


def _full(shape):
    return pl.BlockSpec(shape, lambda i: (0,) * len(shape))


def _merge_out(ys, proj, bgate, wbrs, wo, h, gpost, *, name):
    T, D = h.shape
    tm = _tile(T, 512, 256, 128)

    def body(ya, yb, yc, yd, g0, g1, g2, g3, bg_ref, wa, wb, wc, wd_, wo_ref, h_ref, gp_ref, hn_ref, o_ref, mg_ref):
        merged = jnp.zeros((tm, D), F32)
        for b, (y_ref, gt_ref, w_ref) in enumerate(zip((ya, yb, yc, yd), (g0, g1, g2, g3), (wa, wb, wc, wd_))):
            gate = jax.nn.sigmoid(gt_ref[...].astype(F32) + bg_ref[b:b + 1, :])
            merged = merged + gate * _dot(y_ref[...], w_ref[...])
        mb = merged.astype(BF16)
        mg_ref[...] = mb
        o = _dot(mb, wo_ref[...])
        o_ref[...] = o
        hn_ref[...] = h_ref[...] + o * _rstd(o) * gp_ref[...]

    tok = lambda w: pl.BlockSpec((tm, w), lambda i: (i, 0))
    gate_specs = [pl.BlockSpec((tm, D), lambda i, b=b: (i, b)) for b in range(N_BRANCH)]
    return pl.pallas_call(
        body, name=name, grid=(T // tm,),
        in_specs=[tok(A_WIDTH), tok(B_WIDTH), tok(C_WIDTH), tok(D_WIDTH)] + gate_specs
        + [_full((N_BRANCH, D))] + [_full(w.shape) for w in wbrs] + [_full((D, D)), tok(D), _row(D)],
        out_specs=[tok(D), tok(D), tok(D)],
        out_shape=[jax.ShapeDtypeStruct((T, D), F32), jax.ShapeDtypeStruct((T, D), F32), jax.ShapeDtypeStruct((T, D), BF16)],
        compiler_params=_params("parallel"),
    )(*ys, proj, proj, proj, proj, bgate, *wbrs, wo, h, gpost)


def _merge_out_bwd(dh, o, gpost, wo, ys, proj, bgate, wbrs, *, name):
    T, D = o.shape
    tm = _tile(T, 256, 128)
    widths = (A_WIDTH, B_WIDTH, C_WIDTH, D_WIDTH)

    def body(dh_ref, o_ref, gp_ref, wo_ref, ya, yb, yc, yd, g0, g1, g2, g3, bg_ref, wa, wb, wc, wd_,
             do_ref, dz_ref, dya, dyb, dyc, dyd, dgt_ref, dgp_ref, dbg_ref):
        @pl.when(pl.program_id(0) == 0)
        def _():
            dgp_ref[...] = jnp.zeros_like(dgp_ref)
            dbg_ref[...] = jnp.zeros_like(dbg_ref)

        x = o_ref[...]
        do, dgn = _rms_bwd(dh_ref[...], x, _rstd(x), gp_ref[...])
        dgp_ref[...] += _colsum(dgn)
        dob = do.astype(BF16)
        do_ref[...] = dob
        dmerged = _dot_nt(dob, wo_ref[...])
        for b, (y_ref, gt_ref, w_ref, dy_ref) in enumerate(zip((ya, yb, yc, yd), (g0, g1, g2, g3), (wa, wb, wc, wd_),
                                                               (dya, dyb, dyc, dyd))):
            gate = jax.nn.sigmoid(gt_ref[...].astype(F32) + bg_ref[b:b + 1, :])
            z = _dot(y_ref[...], w_ref[...])
            dz = (dmerged * gate).astype(BF16)
            dz_ref[b] = dz
            dy_ref[...] = _dot_nt(dz, w_ref[...]).astype(BF16)
            dgate = dmerged * z * gate * (1.0 - gate)
            dgt_ref[:, b * D:(b + 1) * D] = dgate.astype(BF16)
            dbg_ref[b:b + 1, :] += _colsum(dgate)

    tok = lambda w: pl.BlockSpec((tm, w), lambda i: (i, 0))
    gate_specs = [pl.BlockSpec((tm, D), lambda i, b=b: (i, b)) for b in range(N_BRANCH)]
    return pl.pallas_call(
        body, name=name, grid=(T // tm,),
        in_specs=[tok(D), tok(D), _row(D), _full((D, D))] + [tok(w) for w in widths] + gate_specs
        + [_full((N_BRANCH, D))] + [_full(w.shape) for w in wbrs],
        out_specs=[tok(D), pl.BlockSpec((N_BRANCH, tm, D), lambda i: (0, i, 0))] + [tok(w) for w in widths]
        + [tok(N_BRANCH * D), _row(D), _full((N_BRANCH, D))],
        out_shape=[jax.ShapeDtypeStruct((T, D), BF16), jax.ShapeDtypeStruct((N_BRANCH, T, D), BF16)]
        + [jax.ShapeDtypeStruct((T, w), BF16) for w in widths]
        + [jax.ShapeDtypeStruct((T, proj.shape[1]), BF16), jax.ShapeDtypeStruct((1, D), F32),
           jax.ShapeDtypeStruct((N_BRANCH, D), F32)],
        compiler_params=_params("arbitrary"),
    )(dh, o, gpost, wo, *ys, proj, proj, proj, proj, bgate, *wbrs)


def _ple(h, pe, ggate, wpg, wple, gpost, *, name):
    T, D = h.shape
    E = pe.shape[1]
    tm = _tile(T, 512, 256, 128)

    def body(h_ref, p_ref, gg_ref, wpg_ref, wple_ref, gp_ref, out_ref):
        x = h_ref[...]
        n = (x * _rstd(x) * gg_ref[...]).astype(BF16)
        pg = jax.nn.sigmoid(_dot(n, wpg_ref[...]))
        e = _dot(p_ref[...].astype(BF16), wple_ref[...])
        out_ref[...] = x + pg * (e * _rstd(e) * gp_ref[...])

    tok = lambda w: pl.BlockSpec((tm, w), lambda i: (i, 0))
    return pl.pallas_call(
        body, name=name, grid=(T // tm,),
        in_specs=[tok(D), tok(E), _row(D), _full((D, D)), _full((E, D)), _row(D)],
        out_specs=tok(D), out_shape=jax.ShapeDtypeStruct((T, D), F32),
        compiler_params=_params("parallel"),
    )(h, pe, ggate, wpg, wple, gpost)


def _ple_bwd(dh, h, pe, ggate, wpg, wple, gpost, *, name):
    T, D = h.shape
    E = pe.shape[1]
    tm = _tile(T, 256, 128)

    def body(dh_ref, h_ref, p_ref, gg_ref, wpg_ref, wple_ref, gp_ref, dhi_ref, de_ref, dpgl_ref, n_ref, dgg_ref, dgp_ref):
        @pl.when(pl.program_id(0) == 0)
        def _():
            dgg_ref[...] = jnp.zeros_like(dgg_ref)
            dgp_ref[...] = jnp.zeros_like(dgp_ref)

        dh = dh_ref[...]
        x = h_ref[...]
        r = _rstd(x)
        n = (x * r * gg_ref[...]).astype(BF16)
        n_ref[...] = n
        pg = jax.nn.sigmoid(_dot(n, wpg_ref[...]))
        e = _dot(p_ref[...].astype(BF16), wple_ref[...])
        re = _rstd(e)
        de, dgn = _rms_bwd(dh * pg, e, re, gp_ref[...])
        dgp_ref[...] += _colsum(dgn)
        de_ref[...] = de.astype(BF16)
        dpgl = (dh * (e * re * gp_ref[...]) * pg * (1.0 - pg)).astype(BF16)
        dpgl_ref[...] = dpgl
        dn = _dot_nt(dpgl, wpg_ref[...])
        dx, dgn2 = _rms_bwd(dn, x, r, gg_ref[...])
        dgg_ref[...] += _colsum(dgn2)
        dhi_ref[...] = dh + dx

    tok = lambda w: pl.BlockSpec((tm, w), lambda i: (i, 0))
    return pl.pallas_call(
        body, name=name, grid=(T // tm,),
        in_specs=[tok(D), tok(D), tok(E), _row(D), _full((D, D)), _full((E, D)), _row(D)],
        out_specs=[tok(D), tok(D), tok(D), tok(D), _row(D), _row(D)],
        out_shape=[jax.ShapeDtypeStruct((T, D), F32), jax.ShapeDtypeStruct((T, D), BF16), jax.ShapeDtypeStruct((T, D), BF16),
                   jax.ShapeDtypeStruct((T, D), BF16), jax.ShapeDtypeStruct((1, D), F32), jax.ShapeDtypeStruct((1, D), F32)],
        compiler_params=_params("arbitrary"),
    )(dh, h, pe, ggate, wpg, wple, gpost)


def _loss_head(y, target, *, name):
    T, D = y.shape
    tm = _tile(T, 512, 256, 128)

    def body(y_ref, t_ref, l_ref, dy_ref):
        @pl.when(pl.program_id(0) == 0)
        def _():
            l_ref[...] = jnp.zeros_like(l_ref)

        err = y_ref[...] - t_ref[...]
        dy_ref[...] = err / D
        l_ref[...] += 0.5 * jnp.sum(jnp.mean(err * err, axis=-1, keepdims=True), axis=0, keepdims=True)

    tok = pl.BlockSpec((tm, D), lambda i: (i, 0))
    return pl.pallas_call(
        body, name=name, grid=(T // tm,),
        in_specs=[tok, tok], out_specs=[_full((8, 128)), tok],
        out_shape=[jax.ShapeDtypeStruct((8, 128), F32), jax.ShapeDtypeStruct((T, D), F32)],
        compiler_params=_params("arbitrary"),
    )(y, target)


def _layout(D):
    off = {'gates': 0}
    off['a'] = N_BRANCH * D
    off['b'] = off['a'] + 3 * A_WIDTH
    off['c'] = off['b'] + 3 * B_WIDTH
    off['d'] = off['c'] + 2 * C_WIDTH
    off['af'] = off['d'] + D_WIDTH + 2 * D_KV_WIDTH
    off['end'] = off['af'] + AF_PAD
    return off


def _seq_spec(S, width, col):
    assert col % width == 0
    return pl.BlockSpec((None, S, width), lambda b: (b, 0, col // width))


def _split3(x):
    hi = x.astype(BF16)
    r1 = x - hi.astype(F32)
    mid = r1.astype(BF16)
    lo = (r1 - mid.astype(F32)).astype(BF16)
    return hi, mid, lo


def _fox_cumsum(af_ref, bf_ref, c_scr, ct_scr):
    S = af_ref.shape[0]
    cb = _tile(S, 256, 128)
    tril = (lax.broadcasted_iota(jnp.int32, (cb, cb), 0) >= lax.broadcasted_iota(jnp.int32, (cb, cb), 1)).astype(BF16)
    carry = jnp.zeros((1, AF_LANES), F32)
    for j in range(S // cb):
        rows = slice(j * cb, (j + 1) * cb)
        hi, mid, lo = _split3(jax.nn.log_sigmoid(af_ref[rows, :] + bf_ref[...]))
        cblk = _dot(tril, hi) + _dot(tril, mid) + _dot(tril, lo) + carry
        c_scr[rows, :] = cblk
        carry = cblk[cb - 1:cb, :]
    ct_scr[...] = c_scr[...].T


def _fox_probs(q_ref, k_ref, c_scr, ct_scr, h, i, bq):
    end = (i + 1) * bq
    qs, hs = slice(i * bq, end), slice(HEAD_DIM * h, HEAD_DIM * (h + 1))
    s = _dot_nt(q_ref[qs, hs] * SCALE, k_ref[0:end, hs])
    s = s + (c_scr[qs, h:h + 1] - ct_scr[h:h + 1, 0:end])
    row = i * bq + lax.broadcasted_iota(jnp.int32, (bq, end), 0)
    col = lax.broadcasted_iota(jnp.int32, (bq, end), 1)
    s = jnp.where(row >= col, s, NEG_INF)
    e = jnp.exp(s - jnp.max(s, axis=-1, keepdims=True))
    return e * (1.0 / jnp.sum(e, axis=-1, keepdims=True))


def _fox_fwd(proj3, af3, bfor, off, *, name, rider=None):
    Bn, S, _ = proj3.shape
    bq = _tile(S, 128)

    def body(q_ref, k_ref, v_ref, af_ref, bf_ref, o_ref, c_scr, ct_scr):
        _fox_cumsum(af_ref, bf_ref, c_scr, ct_scr)
        for h in range(A_HEADS):
            hs = slice(HEAD_DIM * h, HEAD_DIM * (h + 1))
            for i in range(S // bq):
                p = _fox_probs(q_ref, k_ref, c_scr, ct_scr, h, i, bq)
                o_ref[i * bq:(i + 1) * bq, hs] = _dot(p.astype(BF16), v_ref[0:(i + 1) * bq, hs]).astype(BF16)

    return _call(
        body, (proj3, proj3, proj3, af3, bfor), name=name, grid=(Bn,),
        in_specs=[_seq_spec(S, A_WIDTH, off['a']), _seq_spec(S, A_WIDTH, off['a'] + A_WIDTH),
                  _seq_spec(S, A_WIDTH, off['a'] + 2 * A_WIDTH), _seq_spec(S, AF_LANES, 0), _full((1, AF_LANES))],
        out_specs=[_seq_spec(S, A_WIDTH, 0)], out_shape=[jax.ShapeDtypeStruct((Bn, S, A_WIDTH), BF16)],
        scratch_shapes=[pltpu.VMEM((S, AF_LANES), F32), pltpu.VMEM((AF_LANES, S), F32)],
        sem=("parallel",), rider=rider)[0]


def _fox_bwd(proj3, af3, bfor, dya3, off, *, name, rider=None):
    Bn, S, _ = proj3.shape
    bq = _tile(S, 128)
    cb = _tile(S, 256, 128)
    scale = HEAD_DIM ** -0.5

    def body(q_ref, k_ref, v_ref, af_ref, bf_ref, do_ref, dqkv_ref, da_ref, dbf_ref,
             c_scr, ct_scr, dk_scr, dv_scr, dc_scr, dct_scr):
        @pl.when(pl.program_id(0) == 0)
        def _():
            dbf_ref[...] = jnp.zeros_like(dbf_ref)

        _fox_cumsum(af_ref, bf_ref, c_scr, ct_scr)
        dk_scr[...] = jnp.zeros_like(dk_scr)
        dv_scr[...] = jnp.zeros_like(dv_scr)
        dc_scr[...] = jnp.zeros_like(dc_scr)
        dct_scr[...] = jnp.zeros_like(dct_scr)
        for h in range(A_HEADS):
            hs = slice(HEAD_DIM * h, HEAD_DIM * (h + 1))
            for i in range(S // bq):
                end = (i + 1) * bq
                qs = slice(i * bq, end)
                p = _fox_probs(q_ref, k_ref, c_scr, ct_scr, h, i, bq)
                doh = do_ref[qs, hs]
                dp = _dot_nt(doh, v_ref[0:end, hs])
                ds = p * (dp - jnp.sum(p * dp, axis=-1, keepdims=True))
                dsb = ds.astype(BF16)
                dqkv_ref[qs, hs] = (_dot(dsb, k_ref[0:end, hs]) * SCALE).astype(BF16)
                dk_scr[0:end, hs] += _dot_tn(dsb, q_ref[qs, hs] * SCALE)
                dv_scr[0:end, hs] += _dot_tn(p.astype(BF16), doh)
                dc_scr[qs, h:h + 1] += jnp.sum(ds, axis=-1, keepdims=True)
                dct_scr[h:h + 1, 0:end] += -jnp.sum(ds, axis=0, keepdims=True)
        dqkv_ref[:, A_WIDTH:2 * A_WIDTH] = dk_scr[...].astype(BF16)
        dqkv_ref[:, 2 * A_WIDTH:3 * A_WIDTH] = dv_scr[...].astype(BF16)
        dc_scr[...] += dct_scr[...].T
        triu = (lax.broadcasted_iota(jnp.int32, (cb, cb), 0) <= lax.broadcasted_iota(jnp.int32, (cb, cb), 1)).astype(BF16)
        carry = jnp.zeros((1, AF_LANES), F32)
        dbf = jnp.zeros((1, AF_LANES), F32)
        for j in reversed(range(S // cb)):
            rows = slice(j * cb, (j + 1) * cb)
            hi, mid, lo = _split3(dc_scr[rows, :])
            dlf = _dot(triu, hi) + _dot(triu, mid) + _dot(triu, lo) + carry
            carry = dlf[0:1, :]
            da = dlf * jax.nn.sigmoid(-(af_ref[rows, :] + bf_ref[...]))
            dbf = dbf + _colsum(da)
            da_ref[rows, 0:AF_LANES] = da.astype(BF16)
        da_ref[:, AF_LANES:AF_PAD] = jnp.zeros((S, AF_PAD - AF_LANES), BF16)
        dbf_ref[...] += dbf

    return _call(
        body, (proj3, proj3, proj3, af3, bfor, dya3), name=name, grid=(Bn,),
        in_specs=[_seq_spec(S, A_WIDTH, off['a']), _seq_spec(S, A_WIDTH, off['a'] + A_WIDTH),
                  _seq_spec(S, A_WIDTH, off['a'] + 2 * A_WIDTH), _seq_spec(S, AF_LANES, 0), _full((1, AF_LANES)),
                  _seq_spec(S, A_WIDTH, 0)],
        out_specs=[_seq_spec(S, 3 * A_WIDTH, 0), _seq_spec(S, AF_PAD, 0), _full((1, AF_LANES))],
        out_shape=[jax.ShapeDtypeStruct((Bn, S, 3 * A_WIDTH), BF16), jax.ShapeDtypeStruct((Bn, S, AF_PAD), BF16),
                   jax.ShapeDtypeStruct((1, AF_LANES), F32)],
        scratch_shapes=[pltpu.VMEM((S, AF_LANES), F32), pltpu.VMEM((AF_LANES, S), F32), pltpu.VMEM((S, A_WIDTH), F32),
                        pltpu.VMEM((S, A_WIDTH), F32), pltpu.VMEM((S, AF_LANES), F32), pltpu.VMEM((AF_LANES, S), F32)],
        sem=("arbitrary",), rider=rider)


def _shift_down(z, s):
    if s == 0:
        return z
    row = lax.broadcasted_iota(jnp.int32, z.shape, 0)
    return jnp.where(row >= s, pltpu.roll(z, s, 0), 0.0)


def _shift_up(z, s):
    if s == 0:
        return z
    n = z.shape[0]
    row = lax.broadcasted_iota(jnp.int32, z.shape, 0)
    return jnp.where(row < n - s, pltpu.roll(z, n - s, 0), 0.0)


def _conv_fwd(z, w_ref, K):
    acc = jnp.zeros_like(z)
    for k in range(K):
        acc = acc + w_ref[k:k + 1, :] * _shift_down(z, K - 1 - k)
    return acc


def _conv_bwd(dy, z, w_ref, dw_ref, K):
    dz = jnp.zeros_like(z)
    for k in range(K):
        dz = dz + w_ref[k:k + 1, :] * _shift_up(dy, K - 1 - k)
        dw_ref[k:k + 1, :] += _colsum(dy * _shift_down(z, K - 1 - k))
    return dz


def _sconv_fwd(proj3, w, off, *, name):
    Bn, S, _ = proj3.shape

    def body(bg_ref, cg_ref, xb_ref, w_ref, o_ref):
        z = cg_ref[...].astype(F32) * xb_ref[...].astype(F32)
        o_ref[...] = (bg_ref[...].astype(F32) * _conv_fwd(z, w_ref, SHORT_CONV)).astype(BF16)

    return pl.pallas_call(
        body, name=name, grid=(Bn,),
        in_specs=[_seq_spec(S, B_WIDTH, off['b'] + j * B_WIDTH) for j in range(3)] + [_full((SHORT_CONV, B_WIDTH))],
        out_specs=_seq_spec(S, B_WIDTH, 0), out_shape=jax.ShapeDtypeStruct((Bn, S, B_WIDTH), BF16),
        compiler_params=_params("parallel"),
    )(proj3, proj3, proj3, w)


def _sconv_bwd(proj3, w, dyb3, off, *, name):
    Bn, S, _ = proj3.shape

    def body(bg_ref, cg_ref, xb_ref, w_ref, do_ref, din_ref, dw_ref):
        @pl.when(pl.program_id(0) == 0)
        def _():
            dw_ref[...] = jnp.zeros_like(dw_ref)

        cg, xb = cg_ref[...].astype(F32), xb_ref[...].astype(F32)
        z = cg * xb
        do = do_ref[...].astype(F32)
        din_ref[:, 0:B_WIDTH] = (do * _conv_fwd(z, w_ref, SHORT_CONV)).astype(BF16)
        dz = _conv_bwd(do * bg_ref[...].astype(F32), z, w_ref, dw_ref, SHORT_CONV)
        din_ref[:, B_WIDTH:2 * B_WIDTH] = (dz * xb).astype(BF16)
        din_ref[:, 2 * B_WIDTH:3 * B_WIDTH] = (dz * cg).astype(BF16)

    return pl.pallas_call(
        body, name=name, grid=(Bn,),
        in_specs=[_seq_spec(S, B_WIDTH, off['b'] + j * B_WIDTH) for j in range(3)]
        + [_full((SHORT_CONV, B_WIDTH)), _seq_spec(S, B_WIDTH, 0)],
        out_specs=[_seq_spec(S, 3 * B_WIDTH, 0), _full((SHORT_CONV, B_WIDTH))],
        out_shape=[jax.ShapeDtypeStruct((Bn, S, 3 * B_WIDTH), BF16), jax.ShapeDtypeStruct((SHORT_CONV, B_WIDTH), F32)],
        compiler_params=_params("arbitrary"),
    )(proj3, proj3, proj3, w, dyb3)


def _cconv_pre(cin_ref, w_ref, cb_ref):
    x = cin_ref[...].astype(F32)
    a, gt = x[:, 0:C_WIDTH], x[:, C_WIDTH:2 * C_WIDTH]
    sg = jax.nn.sigmoid(gt)
    glu = a * sg
    y0 = _conv_fwd(glu, w_ref, CONF_CONV) + cb_ref[...]
    mu = jnp.mean(y0, axis=-1, keepdims=True)
    xc = y0 - mu
    rs = lax.rsqrt(jnp.mean(xc * xc, axis=-1, keepdims=True) + EPS)
    return a, sg, glu, xc * rs, rs


def _cconv_fwd(proj3, w, cbias, lg, lb, off, *, name):
    Bn, S, _ = proj3.shape

    def body(cin_ref, w_ref, cb_ref, lg_ref, lb_ref, o_ref):
        _, _, _, xh, _ = _cconv_pre(cin_ref, w_ref, cb_ref)
        ln = xh * lg_ref[...] + lb_ref[...]
        o_ref[...] = (ln * jax.nn.sigmoid(ln)).astype(BF16)

    return pl.pallas_call(
        body, name=name, grid=(Bn,),
        in_specs=[_seq_spec(S, 2 * C_WIDTH, off['c']), _full((CONF_CONV, C_WIDTH)), _full((1, C_WIDTH)),
                  _full((1, C_WIDTH)), _full((1, C_WIDTH))],
        out_specs=_seq_spec(S, C_WIDTH, 0), out_shape=jax.ShapeDtypeStruct((Bn, S, C_WIDTH), BF16),
        compiler_params=_params("parallel"),
    )(proj3, w, cbias, lg, lb)


def _cconv_bwd(proj3, w, cbias, lg, lb, dyc3, off, *, name):
    Bn, S, _ = proj3.shape

    def body(cin_ref, w_ref, cb_ref, lg_ref, lb_ref, do_ref, din_ref, dw_ref, dcb_ref, dlg_ref, dlb_ref):
        @pl.when(pl.program_id(0) == 0)
        def _():
            for r in (dw_ref, dcb_ref, dlg_ref, dlb_ref):
                r[...] = jnp.zeros_like(r)

        a, sg, glu, xh, rs = _cconv_pre(cin_ref, w_ref, cb_ref)
        ln = xh * lg_ref[...] + lb_ref[...]
        sl = jax.nn.sigmoid(ln)
        dln = do_ref[...].astype(F32) * (sl * (1.0 + ln * (1.0 - sl)))
        dlg_ref[...] += _colsum(dln * xh)
        dlb_ref[...] += _colsum(dln)
        dxh = dln * lg_ref[...]
        dy0 = rs * (dxh - jnp.mean(dxh, axis=-1, keepdims=True) - xh * jnp.mean(dxh * xh, axis=-1, keepdims=True))
        dcb_ref[...] += _colsum(dy0)
        dglu = _conv_bwd(dy0, glu, w_ref, dw_ref, CONF_CONV)
        din_ref[:, 0:C_WIDTH] = (dglu * sg).astype(BF16)
        din_ref[:, C_WIDTH:2 * C_WIDTH] = (dglu * a * sg * (1.0 - sg)).astype(BF16)

    vec = _full((1, C_WIDTH))
    return pl.pallas_call(
        body, name=name, grid=(Bn,),
        in_specs=[_seq_spec(S, 2 * C_WIDTH, off['c']), _full((CONF_CONV, C_WIDTH)), vec, vec, vec, _seq_spec(S, C_WIDTH, 0)],
        out_specs=[_seq_spec(S, 2 * C_WIDTH, 0), _full((CONF_CONV, C_WIDTH)), vec, vec, vec],
        out_shape=[jax.ShapeDtypeStruct((Bn, S, 2 * C_WIDTH), BF16), jax.ShapeDtypeStruct((CONF_CONV, C_WIDTH), F32)]
        + [jax.ShapeDtypeStruct((1, C_WIDTH), F32)] * 3,
        compiler_params=_params("arbitrary"),
    )(proj3, w, cbias, lg, lb, dyc3)


def _swa_band(x_ref, g, nb):
    xb = x_ref[:, HEAD_DIM * g:HEAD_DIM * (g + 1)].reshape(nb, Q_BLOCK, HEAD_DIM)
    prev = jnp.concatenate([jnp.zeros((1, Q_BLOCK, HEAD_DIM), xb.dtype), xb[:-1]], axis=0)
    return jnp.concatenate([prev, xb], axis=1)


def _swa_probs(q_ref, kband, bias_ref, sk_ref, h, nb):
    qh = (q_ref[:, HEAD_DIM * h:HEAD_DIM * (h + 1)] * SCALE).reshape(nb, Q_BLOCK, HEAD_DIM)
    s = jnp.einsum('nqd,nsd->nqs', qh, kband, preferred_element_type=F32) + bias_ref[h][None]
    shape = (nb, Q_BLOCK, 2 * Q_BLOCK)
    n = lax.broadcasted_iota(jnp.int32, shape, 0)
    dist = lax.broadcasted_iota(jnp.int32, shape, 1) + Q_BLOCK - lax.broadcasted_iota(jnp.int32, shape, 2)
    col = lax.broadcasted_iota(jnp.int32, shape, 2)
    valid = (dist >= 0) & (dist < WINDOW) & ((n > 0) | (col >= Q_BLOCK))
    s = jnp.where(valid, s, NEG_INF)
    sink = sk_ref[h:h + 1, 0:1].reshape(1, 1, 1)
    m = jnp.maximum(jnp.max(s, axis=-1, keepdims=True), sink)
    e = jnp.exp(s - m)
    es = jnp.exp(sink - m)
    inv = 1.0 / (jnp.sum(e, axis=-1, keepdims=True) + es)
    return qh, e * inv, es * inv


def _swa_fwd(proj3, band_bias, sinks, off, *, name):
    Bn, S, _ = proj3.shape
    nb = S // Q_BLOCK

    def body(q_ref, k_ref, v_ref, bias_ref, sk_ref, o_ref):
        for g in range(D_KV_HEADS):
            kband, vband = _swa_band(k_ref, g, nb), _swa_band(v_ref, g, nb)
            for h in range(g * D_GROUP, (g + 1) * D_GROUP):
                _, p, _ = _swa_probs(q_ref, kband, bias_ref, sk_ref, h, nb)
                out = jnp.einsum('nqs,nsd->nqd', p.astype(BF16), vband, preferred_element_type=F32)
                o_ref[:, HEAD_DIM * h:HEAD_DIM * (h + 1)] = out.reshape(S, HEAD_DIM).astype(BF16)

    kcol = off['d'] + D_WIDTH
    return pl.pallas_call(
        body, name=name, grid=(Bn,),
        in_specs=[_seq_spec(S, D_WIDTH, off['d']), _seq_spec(S, D_KV_WIDTH, kcol), _seq_spec(S, D_KV_WIDTH, kcol + D_KV_WIDTH),
                  _full((D_Q_HEADS, Q_BLOCK, 2 * Q_BLOCK)), _full((D_Q_HEADS, 128))],
        out_specs=_seq_spec(S, D_WIDTH, 0), out_shape=jax.ShapeDtypeStruct((Bn, S, D_WIDTH), BF16),
        compiler_params=_params("parallel"),
    )(proj3, proj3, proj3, band_bias, sinks)


def _swa_bwd(proj3, band_bias, sinks, dyd3, off, *, name):
    Bn, S, _ = proj3.shape
    nb = S // Q_BLOCK
    scale = HEAD_DIM ** -0.5

    def body(q_ref, k_ref, v_ref, bias_ref, sk_ref, do_ref, dqkv_ref, dband_ref, dsk_ref):
        @pl.when(pl.program_id(0) == 0)
        def _():
            dband_ref[...] = jnp.zeros_like(dband_ref)
            dsk_ref[...] = jnp.zeros_like(dsk_ref)

        def unband(acc):
            prev, cur = acc[:, 0:Q_BLOCK, :], acc[:, Q_BLOCK:2 * Q_BLOCK, :]
            nxt = jnp.concatenate([prev[1:], jnp.zeros((1, Q_BLOCK, HEAD_DIM), F32)], axis=0)
            return (cur + nxt).reshape(S, HEAD_DIM).astype(BF16)

        for g in range(D_KV_HEADS):
            kband, vband = _swa_band(k_ref, g, nb), _swa_band(v_ref, g, nb)
            dkb = jnp.zeros((nb, 2 * Q_BLOCK, HEAD_DIM), F32)
            dvb = jnp.zeros((nb, 2 * Q_BLOCK, HEAD_DIM), F32)
            for h in range(g * D_GROUP, (g + 1) * D_GROUP):
                hs = slice(HEAD_DIM * h, HEAD_DIM * (h + 1))
                qh, p, ps = _swa_probs(q_ref, kband, bias_ref, sk_ref, h, nb)
                doh = do_ref[:, hs].reshape(nb, Q_BLOCK, HEAD_DIM)
                dp = jnp.einsum('nqd,nsd->nqs', doh, vband, preferred_element_type=F32)
                delta = jnp.sum(p * dp, axis=-1, keepdims=True)
                ds = p * (dp - delta)
                dsink = jnp.sum(jnp.sum(-ps * delta, axis=0), axis=0, keepdims=True)
                dsk_ref[h:h + 1, :] += jnp.broadcast_to(dsink, (1, 128))
                dband_ref[h] += jnp.sum(ds, axis=0)
                dsb = ds.astype(BF16)
                dq = jnp.einsum('nqs,nsd->nqd', dsb, kband, preferred_element_type=F32) * scale
                dqkv_ref[:, hs] = dq.reshape(S, HEAD_DIM).astype(BF16)
                dkb = dkb + jnp.einsum('nqs,nqd->nsd', dsb, qh, preferred_element_type=F32)
                dvb = dvb + jnp.einsum('nqs,nqd->nsd', p.astype(BF16), doh, preferred_element_type=F32)
            dqkv_ref[:, D_WIDTH + HEAD_DIM * g:D_WIDTH + HEAD_DIM * (g + 1)] = unband(dkb)
            dqkv_ref[:, D_WIDTH + D_KV_WIDTH + HEAD_DIM * g:D_WIDTH + D_KV_WIDTH + HEAD_DIM * (g + 1)] = unband(dvb)

    kcol = off['d'] + D_WIDTH
    wq = D_WIDTH + 2 * D_KV_WIDTH
    return pl.pallas_call(
        body, name=name, grid=(Bn,),
        in_specs=[_seq_spec(S, D_WIDTH, off['d']), _seq_spec(S, D_KV_WIDTH, kcol), _seq_spec(S, D_KV_WIDTH, kcol + D_KV_WIDTH),
                  _full((D_Q_HEADS, Q_BLOCK, 2 * Q_BLOCK)), _full((D_Q_HEADS, 128)), _seq_spec(S, D_WIDTH, 0)],
        out_specs=[_seq_spec(S, wq, 0), _full((D_Q_HEADS, Q_BLOCK, 2 * Q_BLOCK)), _full((D_Q_HEADS, 128))],
        out_shape=[jax.ShapeDtypeStruct((Bn, S, wq), BF16), jax.ShapeDtypeStruct((D_Q_HEADS, Q_BLOCK, 2 * Q_BLOCK), F32),
                   jax.ShapeDtypeStruct((D_Q_HEADS, 128), F32)],
        compiler_params=_params("arbitrary"),
    )(proj3, proj3, proj3, band_bias, sinks, dyd3)


ASSEMBLE_BLOCK = 1024


def _assemble(dproj, pieces, col0, *, name):
    T = dproj.shape[0]
    widths = [q.shape[1] for q in pieces]
    starts = [sum(widths[:k]) for k in range(len(widths))]
    bw = ASSEMBLE_BLOCK
    assert sum(widths) % bw == 0 and col0 % bw == 0 and col0 + sum(widths) == dproj.shape[1]
    nblk = sum(widths) // bw
    tm = _tile(T, 512, 256, 128)

    def body(_, *refs):
        out_ref = refs[-1]
        for jj in range(nblk):
            @pl.when(pl.program_id(1) == jj)
            def _():
                lo, hi = jj * bw, (jj + 1) * bw
                for r, s0, w in zip(refs[:-1], starts, widths):
                    a, b = max(lo, s0), min(hi, s0 + w)
                    if a < b:
                        out_ref[:, a - lo:b - lo] = r[:, a - s0:b - s0]

    return pl.pallas_call(
        body, name=name, grid=(T // tm, nblk),
        in_specs=[ANY] + [pl.BlockSpec((tm, w), lambda i, j: (i, 0)) for w in widths],
        out_specs=pl.BlockSpec((tm, bw), lambda i, j: (i, col0 // bw + j)),
        out_shape=jax.ShapeDtypeStruct(dproj.shape, BF16), input_output_aliases={0: 0},
        compiler_params=_params("parallel", "arbitrary"),
    )(dproj, *pieces)


def _relbias_grad(dband, onehot, *, name):
    L, H, n = dband.shape
    R = onehot.shape[0]

    def body(d_ref, oh_ref, out_ref):
        tot = d_ref[0]
        for l in range(1, L):
            tot = tot + d_ref[l]
        out_ref[...] = lax.dot_general(oh_ref[...], tot, (((1,), (1,)), ((), ())), preferred_element_type=F32,
                                       precision=lax.Precision.HIGHEST)

    return pl.pallas_call(
        body, name=name, out_shape=jax.ShapeDtypeStruct((R, H), F32),
        compiler_params=pltpu.CompilerParams(vmem_limit_bytes=VMEM_LIMIT_V7X),
    )(dband, onehot)


def _all_reduce(buf, *, name):
    R, C = buf.shape

    def body(x_ref, o_ref, land, send, recv):
        me = _my_id()
        land[pl.ds(me, 1)] = x_ref[...][None]
        sends = []
        for k in range(1, N_DEV):
            to, _ = _peer(k)
            cp = pltpu.make_async_remote_copy(src_ref=x_ref, dst_ref=land.at[me], send_sem=send.at[k - 1],
                                              recv_sem=recv.at[k - 1], device_id=to, device_id_type=MESH)
            cp.start()
            sends.append(cp)
        for k in range(1, N_DEV):
            frm, frm_id = _peer(k)
            pltpu.make_async_remote_copy(src_ref=x_ref, dst_ref=land.at[frm_id], send_sem=send.at[k - 1],
                                         recv_sem=recv.at[k - 1], device_id=frm, device_id_type=MESH).wait_recv()
        for cp in sends:
            cp.wait_send()
        acc = land[0]
        for d in range(1, N_DEV):
            acc = acc + land[d]
        o_ref[...] = acc

    vmem = pl.BlockSpec(memory_space=pltpu.VMEM)
    return pl.pallas_call(
        body, name=name, in_specs=[vmem], out_specs=vmem, out_shape=jax.ShapeDtypeStruct((R, C), F32),
        scratch_shapes=[pltpu.VMEM((N_DEV, R, C), F32), pltpu.SemaphoreType.DMA((N_DEV - 1,)),
                        pltpu.SemaphoreType.DMA((N_DEV - 1,))],
        compiler_params=pltpu.CompilerParams(vmem_limit_bytes=VMEM_LIMIT_V7X),
    )(buf)


def _row_tile(rows, row_bytes, align):
    fits = [t for t in range(align, rows + 1, align) if rows % t == 0]
    small = [t for t in fits if t * row_bytes <= 2**20]
    return max(small) if small else (min(fits) if fits else rows)


def _sum8(recvs, *, name):
    L = len(recvs)
    _, rows, C = recvs[0].shape
    tr = _row_tile(rows, C * 4 * L, 16)

    def body(*refs):
        o_ref = refs[-1]
        for l, r_ref in enumerate(refs[:-1]):
            acc = r_ref[0].astype(F32)
            for d in range(1, N_DEV):
                acc = acc + r_ref[d].astype(F32)
            o_ref[l] = acc

    return pl.pallas_call(
        body, name=name, grid=(rows // tr,),
        in_specs=[pl.BlockSpec((N_DEV, tr, C), lambda i: (0, i, 0))] * L,
        out_specs=pl.BlockSpec((L, tr, C), lambda i: (0, i, 0)),
        out_shape=jax.ShapeDtypeStruct((L, rows, C), F32), compiler_params=_params("parallel"),
    )(*recvs)


def _band_bias(rel_bias, onehot, *, name):
    R, H = rel_bias.shape

    def body(rb_ref, oh_ref, out_ref):
        out_ref[...] = lax.dot_general(rb_ref[...], oh_ref[...], (((0,), (0,)), ((), ())), preferred_element_type=F32,
                                       precision=lax.Precision.HIGHEST)

    return pl.pallas_call(
        body, name=name, out_shape=jax.ShapeDtypeStruct((H, onehot.shape[1]), F32),
        compiler_params=pltpu.CompilerParams(vmem_limit_bytes=VMEM_LIMIT_V7X),
    )(rel_bias, onehot)


def _adamw(g, w, m, v, *, name):
    rows, C = g.shape
    tr = _row_tile(rows, C * 4, 8)

    def body(g_ref, w_ref, m_ref, v_ref, d_ref, mo_ref, vo_ref):
        gt = g_ref[...]
        mn = ADAM_B1 * m_ref[...] + (1.0 - ADAM_B1) * gt
        vn = ADAM_B2 * v_ref[...] + (1.0 - ADAM_B2) * jnp.square(gt)
        m_hat = mn / (1.0 - ADAM_B1 ** ADAM_STEP)
        v_hat = vn / (1.0 - ADAM_B2 ** ADAM_STEP)
        d_ref[...] = -ADAM_LR * (m_hat / (jnp.sqrt(v_hat) + ADAM_EPS) + ADAM_WD * w_ref[...])
        mo_ref[...] = mn
        vo_ref[...] = vn

    spec = pl.BlockSpec((tr, C), lambda i: (i, 0))
    return pl.pallas_call(
        body, name=name, grid=(rows // tr,), in_specs=[spec] * 4, out_specs=[spec] * 3,
        out_shape=[jax.ShapeDtypeStruct((rows, C), F32)] * 3, compiler_params=_params("parallel"),
    )(g, w, m, v)


def _in_splits():
    a_f = 3 * A_WIDTH
    b = a_f + A_HEADS
    c = b + 3 * B_WIDTH
    d = c + 2 * C_WIDTH
    gates = d + D_WIDTH + 2 * D_KV_WIDTH
    return a_f, b, c, d, gates


def _permute_in(w):
    a_f, b, c, d, gates = _in_splits()
    pad = jnp.zeros(w.shape[:-1] + (AF_PAD - A_HEADS,), w.dtype)
    return jnp.concatenate([w[..., gates:], w[..., :a_f], w[..., b:c], w[..., c:d], w[..., d:gates], w[..., a_f:b], pad], axis=-1)


def _unpermute_in(g, D):
    off = _layout(D)
    return jnp.concatenate([g[..., off['a']:off['b']], g[..., off['af']:off['af'] + A_HEADS], g[..., off['b']:off['c']],
                            g[..., off['c']:off['d']], g[..., off['d']:off['af']], g[..., :off['a']]], axis=-1)


def _bucket_onehot():
    dist = np.maximum(np.arange(Q_BLOCK)[:, None] + Q_BLOCK - np.arange(2 * Q_BLOCK)[None, :], 0)
    max_exact = REL_BUCKETS // 2
    large = max_exact + (np.log(np.maximum(dist, 1).astype(np.float32) / np.float32(max_exact))
                         / np.float32(math.log(REL_MAX_DIST / max_exact)) * np.float32(REL_BUCKETS - max_exact)).astype(np.int32)
    bucket = np.where(dist < max_exact, dist, np.minimum(large, REL_BUCKETS - 1))
    return (bucket.reshape(1, -1) == np.arange(REL_BUCKETS)[:, None]).astype(np.float32)


GATHER_SEQ = [('ffn1_up', ['ffn1_w_gu']), ('ffn1_down', ['ffn1_w_down']), ('proj', ['w_in']),
              ('fox', ['w_br_a', 'w_br_b', 'w_br_c', 'w_br_d', 'w_o', 'w_ple_gate', 'w_ple']),
              ('ffn2_up', ['ffn2_w_gu']), ('ffn2_down', ['ffn2_w_down'])]
GATHER_AHEAD = 2
SCATTER_HOSTS = {'ffn2_down_bwd': ['w_ple', 'w_ple_gate'], 'ffn2_up_bwd': ['ffn2_w_gu'],
                 'fox_bwd': ['ffn2_w_down', 'w_o', 'w_br_a', 'w_br_b', 'w_br_c', 'w_br_d'], 'proj_bwd': ['w_in'],
                 'ffn1_down_bwd': [], 'ffn1_up_bwd': ['ffn1_w_gu']}


def _pack(parts):
    flat = jnp.concatenate([q.reshape(-1).astype(F32) for q in parts])
    return jnp.pad(flat, (0, (-flat.shape[0]) % 1024)).reshape(-1, 128)


def _unpack(buf, shapes):
    flat, out, pos = buf.reshape(-1), [], 0
    for s in shapes:
        n = math.prod(s)
        out.append(flat[pos:pos + n].reshape(s))
        pos += n
    return out


def kernel(x, p, ffn1_norm_pre, ffn1_w_gu, ffn1_w_down, ffn1_norm_post, mix_norm_pre, w_in, b_forget, b_gate, conv_short, conv_dw, conv_dw_bias, conv_ln_gain, conv_ln_bias, attn_sinks, rel_bias, w_br_a, w_br_b, w_br_c, w_br_d, w_o, mix_norm_post, ffn2_norm_pre, ffn2_w_gu, ffn2_w_down, ffn2_norm_post, ple_norm_gate, w_ple_gate, w_ple, ple_norm_post, loss_target, m_ffn1_norm_pre, m_ffn1_w_gu, m_ffn1_w_down, m_ffn1_norm_post, m_mix_norm_pre, m_w_in, m_b_forget, m_b_gate, m_conv_short, m_conv_dw, m_conv_dw_bias, m_conv_ln_gain, m_conv_ln_bias, m_attn_sinks, m_rel_bias, m_w_br_a, m_w_br_b, m_w_br_c, m_w_br_d, m_w_o, m_mix_norm_post, m_ffn2_norm_pre, m_ffn2_w_gu, m_ffn2_w_down, m_ffn2_norm_post, m_ple_norm_gate, m_w_ple_gate, m_w_ple, m_ple_norm_post, v_ffn1_norm_pre, v_ffn1_w_gu, v_ffn1_w_down, v_ffn1_norm_post, v_mix_norm_pre, v_w_in, v_b_forget, v_b_gate, v_conv_short, v_conv_dw, v_conv_dw_bias, v_conv_ln_gain, v_conv_ln_bias, v_attn_sinks, v_rel_bias, v_w_br_a, v_w_br_b, v_w_br_c, v_w_br_d, v_w_o, v_mix_norm_post, v_ffn2_norm_pre, v_ffn2_w_gu, v_ffn2_w_down, v_ffn2_norm_post, v_ple_norm_gate, v_w_ple_gate, v_w_ple, v_ple_norm_post):
    a = dict(locals())
    Bn, S, D = x.shape
    T = Bn * S
    L, E = p.shape[0], p.shape[-1]
    F = ffn1_w_down.shape[1] * N_DEV
    off = _layout(D)
    PW = off['end']
    me = _my_id()

    shard = {n: a[n].astype(BF16) for n in BIG}
    shard['ffn1_w_gu'] = jnp.swapaxes(ffn1_w_gu, 1, 2).astype(BF16)
    shard['ffn2_w_gu'] = jnp.swapaxes(ffn2_w_gu, 1, 2).astype(BF16)
    shard['w_in'] = _permute_in(w_in).astype(BF16)
    is_col = lambda names: [n in COL_SHARDED for n in names]
    head = [n for _, names in GATHER_SEQ[:GATHER_AHEAD] for n in names]
    first = _Exchange("gather", [shard[n][0] for n in head], is_col(head))
    full = [dict(zip(head, first.run_alone("gather_head")))] + [{} for _ in range(1, L)]
    hosts = [h_ for h_, _ in GATHER_SEQ]

    def with_gather(fn, host, i, *args, **kw):
        li, lj = divmod(i * len(hosts) + hosts.index(host) + GATHER_AHEAD, len(hosts))
        names = GATHER_SEQ[lj][1]
        rider = _Exchange("gather", [shard[n][li] for n in names], is_col(names)) if li < L else None
        out = fn(*args, rider=rider, **kw)
        if rider is not None:
            full[li].update(zip(names, rider.result))
        return out

    cw = conv_short.shape[2]
    conv_full = [lax.dynamic_update_slice(jnp.zeros(c.shape[:2] + (cw * N_DEV,), F32), c, (0, 0, me * cw))
                 for c in (conv_short, conv_dw)]
    conv_shapes = [c.shape for c in conv_full]
    cs_all, cdw_all = _unpack(_all_reduce(_pack(conv_full), name="gather_conv"), conv_shapes)

    onehot = jnp.asarray(_bucket_onehot())
    band_bias = _band_bias(rel_bias, onehot, name="band_bias").reshape(D_Q_HEADS, Q_BLOCK, 2 * Q_BLOCK)

    def vec(name, i):
        return a[name][i][None]

    def lay(i):
        return dict(
            bfor=jnp.pad(b_forget[i], (0, AF_LANES - A_HEADS))[None], bgate=b_gate[i].reshape(N_BRANCH, D),
            cs=cs_all[i], cdw=cdw_all[i], cb=conv_dw_bias[i][None], lg=conv_ln_gain[i][None], lb=conv_ln_bias[i][None],
            sinks=jnp.broadcast_to(attn_sinks[i][:, None], (D_Q_HEADS, 128)),
            wbrs=[full[i][n] for n in ('w_br_a', 'w_br_b', 'w_br_c', 'w_br_d')], pe=p[i].reshape(T, E))

    def wgu(k, i):
        return full[i][f'ffn{k}_w_gu'].reshape(2, F, D)

    h = x.reshape(T, D)
    saved = []
    for i in range(L):
        s = dict(h0=h)
        s['gu1'], s['n1'] = with_gather(_rms_mm, 'ffn1_up', i, h, vec('ffn1_norm_pre', i), wgu(1, i), nt=True,
                                        out_dtype=BF16, save_n=True, name=f"ffn1_up_{i}")
        s['h1'], s['f1'] = with_gather(_ffn_down, 'ffn1_down', i, s['gu1'], full[i]['ffn1_w_down'], h,
                                       vec('ffn1_norm_post', i), name=f"ffn1_down_{i}")
        win = full[i]['w_in']
        proj, s['u'], af = with_gather(_rms_mm, 'proj', i, s['h1'], vec('mix_norm_pre', i), win[None], nt=False,
                                       out_dtype=BF16, save_n=True, f32_cols=(off['af'], AF_LANES), name=f"proj_{i}")
        s['proj'] = proj.reshape(T, PW)
        s['proj3'], s['af3'] = proj.reshape(Bn, S, PW), af.reshape(Bn, S, AF_LANES)
        q = lay(i)
        ya = with_gather(_fox_fwd, 'fox', i, s['proj3'], s['af3'], q['bfor'], off, name=f"fox_{i}")
        yb = _sconv_fwd(s['proj3'], q['cs'], off, name=f"sconv_{i}")
        yc = _cconv_fwd(s['proj3'], q['cdw'], q['cb'], q['lg'], q['lb'], off, name=f"cconv_{i}")
        yd = _swa_fwd(s['proj3'], band_bias, q['sinks'], off, name=f"swa_{i}")
        s['ys'] = [y.reshape(T, y.shape[-1]) for y in (ya, yb, yc, yd)]
        s['h2'], s['o'], s['merged'] = _merge_out(s['ys'], s['proj'], q['bgate'], q['wbrs'], full[i]['w_o'], s['h1'],
                                                  vec('mix_norm_post', i), name=f"merge_{i}")
        s['gu2'], s['n2'] = with_gather(_rms_mm, 'ffn2_up', i, s['h2'], vec('ffn2_norm_pre', i), wgu(2, i), nt=True,
                                        out_dtype=BF16, save_n=True, name=f"ffn2_up_{i}")
        s['h3'], s['f2'] = with_gather(_ffn_down, 'ffn2_down', i, s['gu2'], full[i]['ffn2_w_down'], s['h2'],
                                       vec('ffn2_norm_post', i), name=f"ffn2_down_{i}")
        h = _ple(s['h3'], q['pe'], vec('ple_norm_gate', i), full[i]['w_ple_gate'], full[i]['w_ple'],
                 vec('ple_norm_post', i), name=f"ple_{i}")
        saved.append(s)

    lpart, dh = _loss_head(h, loss_target.reshape(T, D), name="loss_head")
    loss = lax.psum(lpart[0, 0], MESH_AXES)

    gbuf = [{} for _ in range(L)]
    recv = [{} for _ in range(L)]
    sg = {n: [None] * L for n in WEIGHTS if n not in BIG and n != 'rel_bias'}
    dbands = [None] * L

    def wgrad(n, i, a_, b_, **kw):
        late = n == 'ffn1_w_gu' and i == 0
        rider = _Exchange("scatter", [gbuf[0]['ffn1_w_down']], [False]) if late else None
        gbuf[i][n] = _mm_tn(a_, b_, name=f"d_{n}_{i}", rider=rider, **kw)
        if late:
            recv[0]['ffn1_w_down'] = rider.result[0]

    def with_scatter(fn, host, i, *args, **kw):
        items = [(i, n) for n in SCATTER_HOSTS[host]]
        if host == 'ffn2_down_bwd' and i + 1 < L:
            items.append((i + 1, 'ffn1_w_down'))
        rider = _Exchange("scatter", [gbuf[l][n] for l, n in items], is_col([n for _, n in items])) if items else None
        out = fn(*args, rider=rider, **kw)
        if rider is not None:
            for (l, n), r in zip(items, rider.result):
                recv[l][n] = r
        return out

    def ffn_bwd(k, i, dh_out, s, h_in):
        dgu, df, sg[f'ffn{k}_norm_post'][i] = with_scatter(
            _ffn_down_bwd, f'ffn{k}_down_bwd', i, dh_out, s[f'f{k}'], vec(f'ffn{k}_norm_post', i), full[i][f'ffn{k}_w_down'],
            s[f'gu{k}'], name=f"ffn{k}_down_bwd_{i}")
        wgrad(f'ffn{k}_w_down', i, s[f'gu{k}'], df, swiglu=True)
        wgrad(f'ffn{k}_w_gu', i, dgu, s[f'n{k}'])
        dh_in, sg[f'ffn{k}_norm_pre'][i] = with_scatter(
            _mm_rmsbwd, f'ffn{k}_up_bwd', i, dgu, wgu(k, i), dh_out, h_in, vec(f'ffn{k}_norm_pre', i), nt=False,
            name=f"ffn{k}_up_bwd_{i}")
        return dh_in

    for i in reversed(range(L)):
        s, q = saved[i], lay(i)
        dh, de, dpgl, npg, sg['ple_norm_gate'][i], sg['ple_norm_post'][i] = _ple_bwd(
            dh, s['h3'], q['pe'], vec('ple_norm_gate', i), full[i]['w_ple_gate'], full[i]['w_ple'], vec('ple_norm_post', i),
            name=f"ple_bwd_{i}")
        wgrad('w_ple', i, q['pe'][None], de)
        wgrad('w_ple_gate', i, npg[None], dpgl)
        dh = ffn_bwd(2, i, dh, s, s['h2'])
        do, dz, *dys, dgates, sg['mix_norm_post'][i], dbg = _merge_out_bwd(
            dh, s['o'], vec('mix_norm_post', i), full[i]['w_o'], s['ys'], s['proj'], q['bgate'], q['wbrs'],
            name=f"merge_bwd_{i}")
        sg['b_gate'][i] = dbg.reshape(-1)
        wgrad('w_o', i, s['merged'][None], do)
        for b, n in enumerate(('w_br_a', 'w_br_b', 'w_br_c', 'w_br_d')):
            wgrad(n, i, s['ys'][b][None], dz, b_plane=b)
        dy3 = [d.reshape(Bn, S, d.shape[-1]) for d in dys]
        da, daf, dbf = with_scatter(_fox_bwd, 'fox_bwd', i, s['proj3'], s['af3'], q['bfor'], dy3[0], off,
                                    name=f"fox_bwd_{i}")
        sg['b_forget'][i] = dbf[0, :A_HEADS]
        db, sg['conv_short'][i] = _sconv_bwd(s['proj3'], q['cs'], dy3[1], off, name=f"sconv_bwd_{i}")
        dc, sg['conv_dw'][i], sg['conv_dw_bias'][i], sg['conv_ln_gain'][i], sg['conv_ln_bias'][i] = _cconv_bwd(
            s['proj3'], q['cdw'], q['cb'], q['lg'], q['lb'], dy3[2], off, name=f"cconv_bwd_{i}")
        dd, dbands[i], dsk = _swa_bwd(s['proj3'], band_bias, q['sinks'], dy3[3], off, name=f"swa_bwd_{i}")
        sg['attn_sinks'][i] = dsk[:, 0]
        dproj = _assemble(dgates, [t.reshape(T, t.shape[-1]) for t in (da, db, dc, dd, daf)], off['a'], name=f"dproj_{i}")
        wgrad('w_in', i, s['u'][None], dproj)
        dh, sg['mix_norm_pre'][i] = with_scatter(
            _mm_rmsbwd, 'proj_bwd', i, dproj[None], full[i]['w_in'][None], dh, s['h1'], vec('mix_norm_pre', i), nt=True,
            name=f"proj_bwd_{i}")
        dh = ffn_bwd(1, i, dh, s, s['h0'])
    grad_x = dh.reshape(Bn, S, D)

    d_rel = _relbias_grad(jnp.stack(dbands).reshape(L, D_Q_HEADS, -1), onehot, name="relbias_grad")
    small_g = [d_rel if n == 'rel_bias' else jnp.stack(sg[n]).reshape(a[n].shape) for n in SMALL]
    n_small_rows = _pack(small_g).shape[0]
    conv_g = [jnp.stack(sg[n]) for n in CONV_SHARDED]
    red = _all_reduce(jnp.concatenate([_pack(small_g), _pack(conv_g)]), name="reduce_small")
    g_small_buf = red[:n_small_rows]
    conv_gfull = _unpack(red[n_small_rows:], conv_shapes)
    conv_gloc = [lax.dynamic_slice_in_dim(g, me * cw, cw, axis=2) for g in conv_gfull]

    grads, deltas, new_m, new_v = {}, {}, {}, {}
    small_shapes = [a[n].shape for n in SMALL]
    res = _adamw(g_small_buf, *[_pack([a[pre + n] for n in SMALL]) for pre in ('', 'm_', 'v_')], name="adamw_small")
    for dst, buf in zip((grads, deltas, new_m, new_v), (g_small_buf,) + tuple(res)):
        dst.update(zip(SMALL, _unpack(buf, small_shapes)))
    loc_shapes = [a[n].shape for n in CONV_SHARDED]
    g_conv_buf = _pack(conv_gloc)
    res = _adamw(g_conv_buf, *[_pack([a[pre + n] for n in CONV_SHARDED]) for pre in ('', 'm_', 'v_')], name="adamw_conv")
    for dst, buf in zip((grads, deltas, new_m, new_v), (g_conv_buf,) + tuple(res)):
        dst.update(zip(CONV_SHARDED, _unpack(buf, loc_shapes)))

    for n in BIG:
        g = _sum8([recv[l][n] for l in range(L)], name=f"sum_{n}")
        if n in ('ffn1_w_gu', 'ffn2_w_gu'):
            g = jnp.swapaxes(g, 1, 2)
        elif n == 'w_in':
            g = _unpermute_in(g, D)
        C = g.shape[-1]
        res = _adamw(g.reshape(-1, C), *[a[pre + n].reshape(-1, C) for pre in ('', 'm_', 'v_')], name=f"adamw_{n}")
        grads[n] = g
        deltas[n], new_m[n], new_v[n] = [t.reshape(g.shape) for t in res]

    return (loss, grad_x, *[grads[n] for n in WEIGHTS], *[deltas[n] for n in WEIGHTS],
            *[new_m[n] for n in WEIGHTS], *[new_v[n] for n in WEIGHTS])
```
